```python
import jax, jax.numpy as jnp
from jax import lax
import numpy as np

D_MODEL = 1024
BATCH = 1
SEQ = 16384
DEPTH = 2

RET_HEADS = 4
RET_QK_DIM = 128
RET_V_DIM = 256
RET_CHUNK = 128
ROPE_BASE = 10000.0
RET_Q = RET_HEADS * RET_QK_DIM
RET_V = RET_HEADS * RET_V_DIM
LRU_WIDTH = D_MODEL
LRU_BLOCKS = 8
LRU_BLOCK_DIM = LRU_WIDTH // LRU_BLOCKS
CONV_WIDTH = 4
LRU_C = 8.0
MEM_LEN = 256
XATTN_HEADS = 4
XATTN_HEAD_DIM = D_MODEL // XATTN_HEADS
PEER_HEADS = 8
PEER_KEYS = 128
PEER_EXPERTS = PEER_KEYS * PEER_KEYS
PEER_TOPK = 16
PEER_KEY_DIM = 256
PEER_HALF = PEER_KEY_DIM // 2
PEER_TOKEN_BLOCK = 128
EPS = 1e-6
IN_WIDTHS = (RET_Q, RET_Q, RET_V, RET_V, LRU_WIDTH, LRU_WIDTH, D_MODEL, D_MODEL)
IN_COLS = 2 * RET_Q + 2 * RET_V + 2 * LRU_WIDTH + 2 * D_MODEL

kernel_name = "hybrid_retention_rglru_peer_block"


def _rmsnorm(x, g):
    xf = x.astype(jnp.float32)
    y = xf * lax.rsqrt(jnp.mean(xf * xf, axis=-1, keepdims=True) + EPS) * g.astype(jnp.float32)
    return y.astype(x.dtype)


def _split_cols(z, widths):
    outs = []
    off = 0
    for w in widths:
        outs.append(z[..., off:off + w])
        off += w
    return outs


def _rotary(x, positions):
    d = x.shape[-1]
    inv_freq = ROPE_BASE ** (-jnp.arange(0, d, 2, dtype=jnp.float32) / d)
    ang = positions.astype(jnp.float32)[:, :, None] * inv_freq
    cos = jnp.cos(ang)[:, :, None, :]
    sin = jnp.sin(ang)[:, :, None, :]
    x1, x2 = x[..., : d // 2], x[..., d // 2:]
    return jnp.concatenate([x1 * cos - x2 * sin, x2 * cos + x1 * sin], axis=-1)


def _retention(q, k, v, positions):
    B, T, _ = q.shape
    H, dk, dv, C = RET_HEADS, RET_QK_DIM, RET_V_DIM, RET_CHUNK
    n = T // C
    q = _rotary(q.astype(jnp.float32).reshape(B, T, H, dk), positions)
    k = _rotary(k.astype(jnp.float32).reshape(B, T, H, dk), positions) * (dk ** -0.5)
    v = v.astype(jnp.float32).reshape(B, T, H, dv)
    log_g = jnp.log(1.0 - 2.0 ** (-5.0 - jnp.arange(H, dtype=jnp.float32)))
    qc = q.reshape(B, n, C, H, dk)
    kc = k.reshape(B, n, C, H, dk)
    vc = v.reshape(B, n, C, H, dv)
    i = jnp.arange(C, dtype=jnp.float32)
    diff = i[:, None] - i[None, :]
    decay = jnp.where(diff[None] >= 0, jnp.exp(diff[None] * log_g[:, None, None]), 0.0)
    scores = jnp.einsum('bnihd,bnjhd->bnhij', qc, kc) * decay
    inner = jnp.einsum('bnhij,bnjhe->bnihe', scores, vc)
    k_decay = jnp.exp((C - 1.0 - i)[:, None] * log_g[None, :])
    kv = jnp.einsum('bnjhd,bnjhe->nbhde', kc * k_decay[:, :, None], vc)
    chunk_decay = jnp.exp(C * log_g)[None, :, None, None]

    def step(state, kv_c):
        return chunk_decay * state + kv_c, state

    _, s_prev = lax.scan(step, jnp.zeros((B, H, dk, dv), jnp.float32), kv)
    q_decay = jnp.exp((i + 1.0)[:, None] * log_g[None, :])
    cross = jnp.einsum('bnihd,nbhde->bnihe', qc * q_decay[:, :, None], s_prev)
    o = (inner + cross).reshape(B, T, H, dv)
    mu = jnp.mean(o, axis=-1, keepdims=True)
    var = jnp.mean(jnp.square(o - mu), axis=-1, keepdims=True)
    o = (o - mu) * lax.rsqrt(var + EPS)
    return o.reshape(B, T, H * dv)


def _rg_lru_branch(x, positions, conv_w, conv_b, lru_wa, lru_ba, lru_wx, lru_bx, lru_lam):
    B, T, W = x.shape
    xf = x.astype(jnp.float32)
    xc = lax.conv_general_dilated(
        xf, conv_w.astype(jnp.float32)[:, None, :], window_strides=(1,),
        padding=[(CONV_WIDTH - 1, 0)], dimension_numbers=('NWC', 'WIO', 'NWC'),
        feature_group_count=W) + conv_b.astype(jnp.float32)
    xb = xc.reshape(B, T, LRU_BLOCKS, LRU_BLOCK_DIM)
    gate_r = jax.nn.sigmoid(jnp.einsum('btnc,ncd->btnd', xb, lru_wa.astype(jnp.float32))
                            + lru_ba.astype(jnp.float32)).reshape(B, T, W)
    gate_i = jax.nn.sigmoid(jnp.einsum('btnc,ncd->btnd', xb, lru_wx.astype(jnp.float32))
                            + lru_bx.astype(jnp.float32)).reshape(B, T, W)
    log_a = -LRU_C * gate_r * jax.nn.softplus(-lru_lam.astype(jnp.float32))
    a = jnp.exp(log_a)
    mult = jnp.sqrt(-jnp.expm1(2.0 * log_a))
    reset = (positions == 0)[:, :, None]
    a = jnp.where(reset, 0.0, a)
    mult = jnp.where(reset, 1.0, mult)
    b = xc * gate_i * mult

    def combine(c1, c2):
        a1, b1 = c1
        a2, b2 = c2
        return a1 * a2, a2 * b1 + b2

    _, h = lax.associative_scan(combine, (a, b), axis=1)
    return h


def _token_mixer(x, positions, g_mix, w_in, w_ret_br, w_rnn_br, w_mix_out,
                 conv_w, conv_b, lru_wa, lru_ba, lru_wx, lru_bx, lru_lam):
    h = _rmsnorm(x, g_mix)
    z = h @ w_in
    q, k, v, g_ret, x_rnn, y_rnn, gate_a, gate_b = _split_cols(z, IN_WIDTHS)
    ret = (jax.nn.silu(g_ret.astype(jnp.float32)) * _retention(q, k, v, positions)).astype(x.dtype)
    p_ret = ret @ w_ret_br
    rnn = _rg_lru_branch(x_rnn, positions, conv_w, conv_b, lru_wa, lru_ba, lru_wx, lru_bx, lru_lam)
    rnn = (rnn * jax.nn.gelu(y_rnn.astype(jnp.float32))).astype(x.dtype)
    p_rnn = rnn @ w_rnn_br
    merged = jax.nn.sigmoid(gate_a) * p_ret + jax.nn.sigmoid(gate_b) * p_rnn
    return merged @ w_mix_out


def _memory_cross_attention(x, mem, g_x, g_mem, w_xq, w_xk, w_xv, w_xo):
    B, T, D = x.shape
    M = mem.shape[1]
    h = _rmsnorm(x, g_x)
    m = _rmsnorm(mem, g_mem)
    q = (h @ w_xq).reshape(B, T, XATTN_HEADS, XATTN_HEAD_DIM)
    k = (m @ w_xk).reshape(B, M, XATTN_HEADS, XATTN_HEAD_DIM)
    v = (m @ w_xv).reshape(B, M, XATTN_HEADS, XATTN_HEAD_DIM)
    s = jnp.einsum('bthd,bmhd->bhtm', q, k).astype(jnp.float32) * (XATTN_HEAD_DIM ** -0.5)
    p = jax.nn.softmax(s, axis=-1).astype(v.dtype)
    o = jnp.einsum('bhtm,bmhd->bthd', p, v).reshape(B, T, D)
    return o @ w_xo


def _peer(x, g_ffn, w_pq, sub_k1, sub_k2, peer_u, peer_v):
    B, T, D = x.shape
    N = B * T
    K = PEER_TOPK
    hf = _rmsnorm(x, g_ffn).reshape(N, D)
    q = (hf @ w_pq).reshape(N, PEER_HEADS, PEER_KEY_DIM)
    q1, q2 = q[..., :PEER_HALF], q[..., PEER_HALF:]
    s1 = jnp.einsum('thd,hnd->thn', q1, sub_k1).astype(jnp.float32)
    s2 = jnp.einsum('thd,hnd->thn', q2, sub_k2).astype(jnp.float32)
    v1, i1 = lax.top_k(s1, K)
    v2, i2 = lax.top_k(s2, K)
    cand = (v1[..., :, None] + v2[..., None, :]).reshape(N, PEER_HEADS, K * K)
    vals, ci = lax.top_k(cand, K)
    e1 = jnp.take_along_axis(i1, ci // K, axis=-1)
    e2 = jnp.take_along_axis(i2, ci % K, axis=-1)
    idx = (e1 * PEER_KEYS + e2).reshape(N, PEER_HEADS * K)
    gates = jax.nn.softmax(vals, axis=-1).reshape(N, PEER_HEADS * K)
    nb = N // PEER_TOKEN_BLOCK

    def block(args):
        xb, ib, gb = args
        ub = jnp.take(peer_u, ib, axis=0)
        act = jax.nn.gelu(jnp.einsum('tkd,td->tk', ub, xb).astype(jnp.float32))
        vb = jnp.take(peer_v, ib, axis=0)
        return jnp.einsum('tk,tkd->td', (gb * act).astype(vb.dtype), vb)

    out = lax.map(block, (hf.reshape(nb, PEER_TOKEN_BLOCK, D),
                          idx.reshape(nb, PEER_TOKEN_BLOCK, PEER_HEADS * K),
                          gates.reshape(nb, PEER_TOKEN_BLOCK, PEER_HEADS * K)))
    return out.reshape(B, T, D)


def setup_inputs(seed: int = 0) -> dict:
    key = jax.random.key(seed)
    ks = jax.random.split(key, 32)
    f32 = jnp.float32
    L, D = DEPTH, D_MODEL

    def nrm(k, shape, fan_in):
        return jax.random.normal(k, shape, f32) * (fan_in ** -0.5)

    def gain(k, shape):
        return 1.0 + 0.02 * jax.random.normal(k, shape, f32)

    u = jax.random.uniform(ks[13], (L, LRU_WIDTH), f32, 0.9, 0.999)
    s = u ** (1.0 / LRU_C)
    lam = jnp.log(s) - jnp.log1p(-s)
    return {
        "x": jax.random.normal(ks[0], (BATCH, SEQ, D), f32),
        "mem": jax.random.normal(ks[1], (BATCH, MEM_LEN, D), f32),
        "positions": jnp.broadcast_to(jnp.arange(SEQ, dtype=jnp.int32)[None, :], (BATCH, SEQ)),
        "g_mix": gain(ks[2], (L, D)),
        "w_in": nrm(ks[3], (L, D, IN_COLS), D),
        "w_ret_br": nrm(ks[4], (L, RET_V, D), RET_V),
        "w_rnn_br": nrm(ks[5], (L, LRU_WIDTH, D), LRU_WIDTH),
        "w_mix_out": nrm(ks[6], (L, D, D), D),
        "conv_w": nrm(ks[7], (L, CONV_WIDTH, LRU_WIDTH), CONV_WIDTH),
        "conv_b": 0.01 * jax.random.normal(ks[8], (L, LRU_WIDTH), f32),
        "lru_wa": nrm(ks[9], (L, LRU_BLOCKS, LRU_BLOCK_DIM, LRU_BLOCK_DIM), LRU_BLOCK_DIM),
        "lru_ba": 0.01 * jax.random.normal(ks[10], (L, LRU_BLOCKS, LRU_BLOCK_DIM), f32),
        "lru_wx": nrm(ks[11], (L, LRU_BLOCKS, LRU_BLOCK_DIM, LRU_BLOCK_DIM), LRU_BLOCK_DIM),
        "lru_bx": 0.01 * jax.random.normal(ks[12], (L, LRU_BLOCKS, LRU_BLOCK_DIM), f32),
        "lru_lam": lam,
        "g_x": gain(ks[14], (L, D)),
        "g_mem": gain(ks[15], (L, D)),
        "w_xq": nrm(ks[16], (L, D, D), D),
        "w_xk": nrm(ks[17], (L, D, D), D),
        "w_xv": nrm(ks[18], (L, D, D), D),
        "w_xo": nrm(ks[19], (L, D, D), D),
        "g_ffn": gain(ks[20], (L, D)),
        "w_pq": nrm(ks[21], (L, D, PEER_HEADS * PEER_KEY_DIM), D),
        "sub_k1": nrm(ks[22], (L, PEER_HEADS, PEER_KEYS, PEER_HALF), PEER_HALF),
        "sub_k2": nrm(ks[23], (L, PEER_HEADS, PEER_KEYS, PEER_HALF), PEER_HALF),
        "peer_u": nrm(ks[24], (L, PEER_EXPERTS, D), D),
        "peer_v": nrm(ks[25], (L, PEER_EXPERTS, D), D),
        "g_final": gain(ks[26], (D,)),
    }


def reference(x, mem, positions, g_mix, w_in, w_ret_br, w_rnn_br, w_mix_out,
              conv_w, conv_b, lru_wa, lru_ba, lru_wx, lru_bx, lru_lam,
              g_x, g_mem, w_xq, w_xk, w_xv, w_xo,
              g_ffn, w_pq, sub_k1, sub_k2, peer_u, peer_v, g_final):
    for l in range(DEPTH):
        x = x + _token_mixer(x, positions, g_mix[l], w_in[l], w_ret_br[l], w_rnn_br[l], w_mix_out[l],
                             conv_w[l], conv_b[l], lru_wa[l], lru_ba[l], lru_wx[l], lru_bx[l], lru_lam[l])
        x = x + _memory_cross_attention(x, mem, g_x[l], g_mem[l], w_xq[l], w_xk[l], w_xv[l], w_xo[l])
        x = x + _peer(x, g_ffn[l], w_pq[l], sub_k1[l], sub_k2[l], peer_u[l], peer_v[l])
    return _rmsnorm(x, g_final)
```

```python
import functools
import math

import numpy as np
import jax
import jax.numpy as jnp
from jax import lax
from jax.experimental import pallas as pl
from jax.experimental.pallas import tpu as pltpu

F32 = jnp.float32
BF16 = jnp.bfloat16

D_MODEL = 1024
EPS = 1e-6
RET_HEADS = 4
RET_QK_DIM = 128
RET_V_DIM = 256
RET_CHUNK = 128
ROPE_BASE = 10000.0
LRU_BLOCKS = 8
LRU_BLOCK_DIM = D_MODEL // LRU_BLOCKS
CONV_WIDTH = 4
LRU_C = 8.0
XATTN_HEADS = 4
XATTN_HEAD_DIM = D_MODEL // XATTN_HEADS
PEER_HEADS = 8
PEER_KEYS = 128
PEER_TOPK = 16
PEER_HALF = 128

LANES = 128
SUBLANES = 8
VMEM_LIMIT_BYTES = 56 * 1024 * 1024

TM_PROJ = 512
TN_PROJ = 1024
TB_RET = 256
TB_LRU = 256
TM_PEER = 512
NA_PEER = 4
TL_SORT = 128


def _params(*sem):
    return pltpu.CompilerParams(dimension_semantics=sem, vmem_limit_bytes=VMEM_LIMIT_BYTES)


def _rms(x, g):
    return x * lax.rsqrt(jnp.mean(x * x, axis=-1, keepdims=True) + EPS) * g


def _gelu(x):
    c = math.sqrt(2.0 / math.pi)
    return 0.5 * x * (1.0 + jnp.tanh(c * (x + 0.044715 * (x * x * x))))


def _dot(a, b):
    return jnp.dot(a, b, preferred_element_type=F32)


def _dot_nt(a, b):
    return lax.dot_general(a, b, (((1,), (1,)), ((), ())), preferred_element_type=F32)


def _rms_matmul_kernel(x_ref, g_ref, w_ref, o_ref, h_ref):
    @pl.when(pl.program_id(1) == 0)
    def _():
        h_ref[...] = _rms(x_ref[...], g_ref[...]).astype(BF16)

    o_ref[...] = _dot(h_ref[...], w_ref[...]).astype(o_ref.dtype)


def _rms_matmul(x, g, w, *, tm, tn, out_dtype):
    T, K = x.shape
    N = w.shape[1]
    return pl.pallas_call(
        _rms_matmul_kernel,
        grid=(T // tm, N // tn),
        in_specs=[pl.BlockSpec((tm, K), lambda i, j: (i, 0)),
                  pl.BlockSpec((1, K), lambda i, j: (0, 0)),
                  pl.BlockSpec((K, tn), lambda i, j: (0, j))],
        out_specs=pl.BlockSpec((tm, tn), lambda i, j: (i, j)),
        out_shape=jax.ShapeDtypeStruct((T, N), out_dtype),
        scratch_shapes=[pltpu.VMEM((tm, K), BF16)],
        compiler_params=_params("parallel", "arbitrary"),
        name="rms_matmul",
    )(x, g, w)


def _rope_kernel(pos_ref, invf_ref, cos_ref, sin_ref):
    ang = pos_ref[...].astype(F32) * invf_ref[...]
    lane = lax.broadcasted_iota(jnp.int32, ang.shape, 1)
    s = jnp.sin(ang)
    cos_ref[...] = jnp.cos(ang)
    sin_ref[...] = jnp.where(lane < RET_QK_DIM // 2, -s, s)


def _rope_tables(pos_col, tb):
    T = pos_col.shape[0]
    half = RET_QK_DIM // 2
    inv_freq = ROPE_BASE ** (-jnp.arange(0, RET_QK_DIM, 2, dtype=F32) / RET_QK_DIM)
    invf2 = jnp.concatenate([inv_freq, inv_freq]).reshape(1, 2 * half)
    return pl.pallas_call(
        _rope_kernel,
        grid=(T // tb,),
        in_specs=[pl.BlockSpec((tb, 1), lambda i: (i, 0)),
                  pl.BlockSpec((1, RET_QK_DIM), lambda i: (0, 0))],
        out_specs=[pl.BlockSpec((tb, RET_QK_DIM), lambda i: (i, 0))] * 2,
        out_shape=[jax.ShapeDtypeStruct((T, RET_QK_DIM), F32)] * 2,
        compiler_params=_params("parallel"),
        name="rope_tables",
    )(pos_col, invf2)


def _retention_kernel(q_ref, k_ref, v_ref, g_ref, cos_ref, sin_ref, dec_ref, kdec_ref, qdec_ref,
                      o_ref, s_ref, *, chunk_decay):
    C, dk, dv = RET_CHUNK, RET_QK_DIM, RET_V_DIM

    @pl.when(pl.program_id(0) == 0)
    def _():
        s_ref[...] = jnp.zeros_like(s_ref)

    kscale = dk ** -0.5
    for c in range(q_ref.shape[0] // C):
        rows = slice(c * C, (c + 1) * C)
        cos = cos_ref[rows, :]
        sin = sin_ref[rows, :]
        for h in range(RET_HEADS):
            q = q_ref[rows, h * dk:(h + 1) * dk]
            k = k_ref[rows, h * dk:(h + 1) * dk]
            v = v_ref[rows, h * dv:(h + 1) * dv].astype(BF16)
            qr = q * cos + pltpu.roll(q, dk // 2, 1) * sin
            kr = (k * cos + pltpu.roll(k, dk // 2, 1) * sin) * kscale
            scores = _dot_nt(qr.astype(BF16), kr.astype(BF16)) * dec_ref[h]
            inner = _dot(scores.astype(BF16), v)
            state = s_ref[h]
            cross = _dot((qr * qdec_ref[h]).astype(BF16), state.astype(BF16))
            kd_t = jnp.transpose(kr * kdec_ref[h]).astype(BF16)
            s_ref[h] = chunk_decay[h] * state + _dot(kd_t, v)
            o = inner + cross
            mu = jnp.mean(o, axis=-1, keepdims=True)
            oc = o - mu
            var = jnp.mean(oc * oc, axis=-1, keepdims=True)
            o = oc * lax.rsqrt(var + EPS)
            g = g_ref[rows, h * dv:(h + 1) * dv]
            o_ref[rows, h * dv:(h + 1) * dv] = (g * jax.nn.sigmoid(g) * o).astype(o_ref.dtype)


def _retention(z, cos2, sin2, tb):
    T = z.shape[0]
    H, C, dk, dv = RET_HEADS, RET_CHUNK, RET_QK_DIM, RET_V_DIM
    log_g = jnp.log(1.0 - 2.0 ** (-5.0 - jnp.arange(H, dtype=F32)))
    i = jnp.arange(C, dtype=F32)
    diff = i[:, None] - i[None, :]
    decay = jnp.where(diff[None] >= 0, jnp.exp(diff[None] * log_g[:, None, None]), 0.0)
    k_decay = jnp.exp((C - 1.0 - i)[None, :] * log_g[:, None])
    q_decay = jnp.exp((i + 1.0)[None, :] * log_g[:, None])
    kdec = jnp.broadcast_to(k_decay[:, :, None], (H, C, dk))
    qdec = jnp.broadcast_to(q_decay[:, :, None], (H, C, dk))
    log_g_np = np.log(1.0 - 2.0 ** (-5.0 - np.arange(H, dtype=np.float32))).astype(np.float32)
    chunk_decay = tuple(float(np.exp(np.float32(C) * lg)) for lg in log_g_np)
    nq = H * dk
    const3 = lambda i: (0, 0, 0)
    return pl.pallas_call(
        functools.partial(_retention_kernel, chunk_decay=chunk_decay),
        grid=(T // tb,),
        in_specs=[pl.BlockSpec((tb, nq), lambda i: (i, 0)),
                  pl.BlockSpec((tb, nq), lambda i: (i, 1)),
                  pl.BlockSpec((tb, H * dv), lambda i: (i, 1)),
                  pl.BlockSpec((tb, H * dv), lambda i: (i, 2)),
                  pl.BlockSpec((tb, dk), lambda i: (i, 0)),
                  pl.BlockSpec((tb, dk), lambda i: (i, 0)),
                  pl.BlockSpec((H, C, C), const3),
                  pl.BlockSpec((H, C, dk), const3),
                  pl.BlockSpec((H, C, dk), const3)],
        out_specs=pl.BlockSpec((tb, H * dv), lambda i: (i, 0)),
        out_shape=jax.ShapeDtypeStruct((T, H * dv), BF16),
        scratch_shapes=[pltpu.VMEM((H, dk, dv), F32)],
        compiler_params=_params("arbitrary"),
        name="retention",
    )(z, z, z, z, cos2, sin2, decay, kdec, qdec)


def _rglru_kernel(x_ref, y_ref, pos_ref, cw_ref, cb_ref, wa_ref, ba_ref, wx_ref, bx_ref, lam_ref,
                  o_ref, xpad_ref, a_ref, b_ref, hc_ref):
    tb = x_ref.shape[0]
    pad = tb // 2
    halo = SUBLANES

    @pl.when(pl.program_id(0) == 0)
    def _():
        xpad_ref[0:halo, :] = jnp.zeros((halo, D_MODEL), F32)
        hc_ref[...] = jnp.zeros_like(hc_ref)
        a_ref[0:pad, :] = jnp.ones((pad, D_MODEL), F32)
        b_ref[0:pad, :] = jnp.zeros((pad, D_MODEL), F32)

    xpad_ref[halo:halo + tb, :] = x_ref[...]
    xc = cb_ref[...] + cw_ref[3:4, :] * xpad_ref[halo:halo + tb, :]
    for w in range(CONV_WIDTH - 1):
        off = halo - (CONV_WIDTH - 1) + w
        xc = xc + cw_ref[w:w + 1, :] * xpad_ref[off:off + tb, :]
    xpad_ref[0:halo, :] = xpad_ref[tb:tb + halo, :]

    reset = pos_ref[...] == 0
    lam = lam_ref[...]
    nl = -lam
    sp = jnp.maximum(nl, 0.0) + jnp.log(1.0 + jnp.exp(-jnp.abs(nl)))
    bd = LRU_BLOCK_DIM
    for n in range(LRU_BLOCKS):
        cols = slice(n * bd, (n + 1) * bd)
        xcn = xc[:, cols]
        xb = xcn.astype(BF16)
        gate_r = jax.nn.sigmoid(_dot(xb, wa_ref[n]) + ba_ref[:, cols])
        gate_i = jax.nn.sigmoid(_dot(xb, wx_ref[n]) + bx_ref[:, cols])
        log_a = (-LRU_C) * gate_r * sp[:, cols]
        a = jnp.exp(log_a)
        mult = jnp.sqrt(jnp.tanh(-log_a) * (a * a + 1.0))
        a = jnp.where(reset, 0.0, a)
        mult = jnp.where(reset, 1.0, mult)
        b = xcn * gate_i * mult
        s = 1
        while s < tb:
            a_ref[pad:pad + tb, cols] = a
            b_ref[pad:pad + tb, cols] = b
            a_sh = a_ref[pad - s:pad - s + tb, cols]
            b_sh = b_ref[pad - s:pad - s + tb, cols]
            b = a * b_sh + b
            a = a * a_sh
            s *= 2
        h = a * hc_ref[0:1, cols] + b
        hc_ref[:, cols] = jnp.broadcast_to(h[tb - 1:tb, :], (SUBLANES, bd))
        o_ref[:, cols] = (h * _gelu(y_ref[:, cols])).astype(o_ref.dtype)


def _rglru(z, pos_col, conv_w, conv_b, wa, ba, wx, bx, lam, tb):
    T = z.shape[0]
    W = D_MODEL
    row = lambda i: (0, 0)
    const3 = lambda i: (0, 0, 0)
    return pl.pallas_call(
        _rglru_kernel,
        grid=(T // tb,),
        in_specs=[pl.BlockSpec((tb, W), lambda i: (i, 3)),
                  pl.BlockSpec((tb, W), lambda i: (i, 4)),
                  pl.BlockSpec((tb, 1), lambda i: (i, 0)),
                  pl.BlockSpec((CONV_WIDTH, W), row),
                  pl.BlockSpec((1, W), row),
                  pl.BlockSpec((LRU_BLOCKS, LRU_BLOCK_DIM, LRU_BLOCK_DIM), const3),
                  pl.BlockSpec((1, W), row),
                  pl.BlockSpec((LRU_BLOCKS, LRU_BLOCK_DIM, LRU_BLOCK_DIM), const3),
                  pl.BlockSpec((1, W), row),
                  pl.BlockSpec((1, W), row)],
        out_specs=pl.BlockSpec((tb, W), lambda i: (i, 0)),
        out_shape=jax.ShapeDtypeStruct((T, W), BF16),
        scratch_shapes=[pltpu.VMEM((tb + 2 * SUBLANES, W), F32),
                        pltpu.VMEM((tb + tb // 2, W), F32),
                        pltpu.VMEM((tb + tb // 2, W), F32),
                        pltpu.VMEM((SUBLANES, W), F32)],
        compiler_params=_params("arbitrary"),
        name="rglru",
    )(z, z, pos_col, conv_w, conv_b, wa, ba, wx, bx, lam)


def _merge_kernel(x_ref, ret_ref, rnn_ref, ga_ref, gb_ref, wr_ref, wn_ref, wo_ref, o_ref):
    p_ret = _dot(ret_ref[...], wr_ref[...])
    p_rnn = _dot(rnn_ref[...], wn_ref[...])
    merged = jax.nn.sigmoid(ga_ref[...]) * p_ret + jax.nn.sigmoid(gb_ref[...]) * p_rnn
    o_ref[...] = x_ref[...] + _dot(merged.astype(BF16), wo_ref[...])


def _merge(x, ret, rnn, z, w_ret, w_rnn, w_out, tm):
    T, D = x.shape
    tile = lambda i: (i, 0)
    wspec = pl.BlockSpec((D, D), lambda i: (0, 0))
    return pl.pallas_call(
        _merge_kernel,
        grid=(T // tm,),
        in_specs=[pl.BlockSpec((tm, D), tile), pl.BlockSpec((tm, D), tile), pl.BlockSpec((tm, D), tile),
                  pl.BlockSpec((tm, D), lambda i: (i, 5)),
                  pl.BlockSpec((tm, D), lambda i: (i, 6)),
                  wspec, wspec, wspec],
        out_specs=pl.BlockSpec((tm, D), tile),
        out_shape=jax.ShapeDtypeStruct((T, D), F32),
        compiler_params=_params("parallel"),
        name="merge",
    )(x, ret, rnn, z, z, w_ret, w_rnn, w_out)


def _xattn_kernel(x_ref, g_ref, wq_ref, kt_ref, v_ref, wo_ref, o_ref, att_ref):
    x = x_ref[...]
    q = _dot(_rms(x, g_ref[...]).astype(BF16), wq_ref[...])
    dh = XATTN_HEAD_DIM
    for h in range(XATTN_HEADS):
        cols = slice(h * dh, (h + 1) * dh)
        s = _dot(q[:, cols].astype(BF16), kt_ref[cols, :]) * (dh ** -0.5)
        s = s - jnp.max(s, axis=-1, keepdims=True)
        e = jnp.exp(s)
        p = e / jnp.sum(e, axis=-1, keepdims=True)
        att_ref[:, cols] = _dot(p.astype(BF16), v_ref[:, cols]).astype(BF16)
    o_ref[...] = x + _dot(att_ref[...], wo_ref[...])


def _xattn(x, g, wq, kt, v, wo, tm):
    T, D = x.shape
    M = v.shape[0]
    tile = lambda i: (i, 0)
    fixed = lambda i: (0, 0)
    return pl.pallas_call(
        _xattn_kernel,
        grid=(T // tm,),
        in_specs=[pl.BlockSpec((tm, D), tile), pl.BlockSpec((1, D), fixed), pl.BlockSpec((D, D), fixed),
                  pl.BlockSpec((D, M), fixed), pl.BlockSpec((M, D), fixed), pl.BlockSpec((D, D), fixed)],
        out_specs=pl.BlockSpec((tm, D), tile),
        out_shape=jax.ShapeDtypeStruct((T, D), F32),
        scratch_shapes=[pltpu.VMEM((tm, D), BF16)],
        compiler_params=_params("parallel"),
        name="xattn",
    )(x, g, wq, kt, v, wo)


def _peer_scores_kernel(x_ref, g_ref, wq_ref, k1_ref, k2_ref, hft_ref, s1_ref, s2_ref):
    hf = _rms(x_ref[...], g_ref[...])
    hft = jnp.transpose(hf).astype(BF16)
    hft_ref[...] = hft
    qt = _dot(wq_ref[...], hft)
    kd = 2 * PEER_HALF
    for h in range(PEER_HEADS):
        q1 = qt[h * kd:h * kd + PEER_HALF, :].astype(BF16)
        q2 = qt[h * kd + PEER_HALF:(h + 1) * kd, :].astype(BF16)
        s1_ref[h] = _dot(k1_ref[h], q1)
        s2_ref[h] = _dot(k2_ref[h], q2)


def _peer_scores(x, g, wq_t, k1, k2, tm):
    T, D = x.shape
    H, NK = PEER_HEADS, PEER_KEYS
    fixed = lambda i: (0, 0)
    const3 = lambda i: (0, 0, 0)
    return pl.pallas_call(
        _peer_scores_kernel,
        grid=(T // tm,),
        in_specs=[pl.BlockSpec((tm, D), lambda i: (i, 0)), pl.BlockSpec((1, D), fixed),
                  pl.BlockSpec(wq_t.shape, fixed),
                  pl.BlockSpec((H, NK, PEER_HALF), const3), pl.BlockSpec((H, NK, PEER_HALF), const3)],
        out_specs=[pl.BlockSpec((D, tm), lambda i: (0, i)),
                   pl.BlockSpec((H, NK, tm), lambda i: (0, 0, i)),
                   pl.BlockSpec((H, NK, tm), lambda i: (0, 0, i))],
        out_shape=[jax.ShapeDtypeStruct((D, T), BF16),
                   jax.ShapeDtypeStruct((H, NK, T), F32),
                   jax.ShapeDtypeStruct((H, NK, T), F32)],
        compiler_params=_params("parallel"),
        name="peer_scores",
    )(x, g, wq_t, k1, k2)


def _sort_pairs(n):
    pairs = []
    p = 1
    while p < n:
        k = p
        while k >= 1:
            for j in range(k % p, n - k, 2 * k):
                for i in range(min(k, n - j - k)):
                    if (i + j) // (2 * p) == (i + j + k) // (2 * p):
                        pairs.append((i + j, i + j + k))
            k //= 2
        p *= 2
    return pairs


def _sort_desc(vs):
    vs = list(vs)
    for lo, hi in _sort_pairs(len(vs)):
        a, b = vs[lo], vs[hi]
        vs[lo], vs[hi] = jnp.maximum(a, b), jnp.minimum(a, b)
    return vs


def _merge_top(a, b):
    n = len(a)
    c = [jnp.maximum(a[i], b[n - 1 - i]) for i in range(n)]
    k = n // 2
    while k >= 1:
        for i in range(n):
            if i & k == 0:
                x, y = c[i], c[i + k]
                c[i], c[i + k] = jnp.maximum(x, y), jnp.minimum(x, y)
        k //= 2
    return c


def _top_of_keys(s):
    groups = s.shape[0] // SUBLANES
    vs = _sort_desc([s[i * SUBLANES:(i + 1) * SUBLANES, :] for i in range(groups)])
    shift = SUBLANES // 2
    while shift >= 1:
        vs = _merge_top(vs, [pltpu.roll(v, shift, 0) for v in vs])
        shift //= 2
    return vs


def _peer_select_kernel(s1_ref, s2_ref, st_ref):
    K = PEER_TOPK
    tl = s1_ref.shape[-1]
    sub = lax.broadcasted_iota(jnp.int32, (SUBLANES, tl), 0)
    v1 = [jnp.zeros((SUBLANES, tl), F32)] * K
    v2 = [jnp.zeros((SUBLANES, tl), F32)] * K
    for h in range(PEER_HEADS):
        t1 = _top_of_keys(s1_ref[h])
        t2 = _top_of_keys(s2_ref[h])
        v1 = [jnp.where(sub == h, t, v) for t, v in zip(t1, v1)]
        v2 = [jnp.where(sub == h, t, v) for t, v in zip(t2, v2)]
    cands = [v1[i] + v2[j] for i in range(K) for j in range(K // (i + 1))]
    n = 1
    while n < len(cands):
        n *= 2
    cands = cands + [jnp.full((SUBLANES, tl), -jnp.inf, F32)] * (n - len(cands))
    top = _sort_desc(cands)[:K]
    z = jnp.zeros((SUBLANES, tl), F32)
    for t in top:
        z = z + jnp.exp(t - top[0])
    st_ref[0] = top[K - 1]
    st_ref[1] = v1[0]
    st_ref[2] = v2[0]
    st_ref[3] = 1.0 / z


def _peer_select(s1, s2, tl):
    H, NK, T = s1.shape
    return pl.pallas_call(
        _peer_select_kernel,
        grid=(T // tl,),
        in_specs=[pl.BlockSpec((H, NK, tl), lambda i: (0, 0, i))] * 2,
        out_specs=pl.BlockSpec((4, H, tl), lambda i: (0, 0, i)),
        out_shape=jax.ShapeDtypeStruct((4, H, T), F32),
        compiler_params=_params("parallel"),
        name="peer_select",
    )(s1, s2)


def _peer_dense_kernel(x_ref, hft_ref, s1_ref, s2_ref, st_ref, u_ref, vt_ref, o_ref,
                       p1_ref, e2_ref, acc_ref, w_ref, *, na):
    j = pl.program_id(1)
    NK = PEER_KEYS
    tm = hft_ref.shape[1]

    @pl.when(j == 0)
    def _():
        acc_ref[...] = jnp.zeros_like(acc_ref)
        for h in range(PEER_HEADS):
            m1 = st_ref[1, h:h + 1, :]
            m2 = st_ref[2, h:h + 1, :]
            zinv = st_ref[3, h:h + 1, :]
            p1_ref[h] = jnp.exp(s1_ref[h] - m1) * zinv
            e2_ref[h] = jnp.exp(s2_ref[h] - m2)

    act = _dot(u_ref[...], hft_ref[...])
    for ai in range(na):
        a = j * na + ai
        gate = jnp.zeros((NK, tm), F32)
        for h in range(PEER_HEADS):
            s1row = s1_ref[h, pl.ds(a, 1), :]
            p1row = p1_ref[h, pl.ds(a, 1), :]
            tau = st_ref[0, h:h + 1, :]
            sel = (s1row + s2_ref[h]) >= tau
            gate = gate + jnp.where(sel, p1row * e2_ref[h], 0.0)
        rows = slice(ai * NK, (ai + 1) * NK)
        w_ref[rows, :] = (gate * _gelu(act[rows, :])).astype(BF16)
    acc_ref[...] += _dot(vt_ref[...], w_ref[...])

    @pl.when(j == pl.num_programs(1) - 1)
    def _():
        o_ref[...] = x_ref[...] + jnp.transpose(acc_ref[...])


def _peer_dense(x, hft, s1, s2, stats, u, vt, tm, na):
    T, D = x.shape
    H, NK = PEER_HEADS, PEER_KEYS
    te = na * NK
    return pl.pallas_call(
        functools.partial(_peer_dense_kernel, na=na),
        grid=(T // tm, NK // na),
        in_specs=[pl.BlockSpec((tm, D), lambda i, j: (i, 0)),
                  pl.BlockSpec((D, tm), lambda i, j: (0, i)),
                  pl.BlockSpec((H, NK, tm), lambda i, j: (0, 0, i)),
                  pl.BlockSpec((H, NK, tm), lambda i, j: (0, 0, i)),
                  pl.BlockSpec((4, H, tm), lambda i, j: (0, 0, i)),
                  pl.BlockSpec((te, D), lambda i, j: (j, 0)),
                  pl.BlockSpec((D, te), lambda i, j: (0, j))],
        out_specs=pl.BlockSpec((tm, D), lambda i, j: (i, 0)),
        out_shape=jax.ShapeDtypeStruct((T, D), F32),
        scratch_shapes=[pltpu.VMEM((H, NK, tm), F32), pltpu.VMEM((H, NK, tm), F32),
                        pltpu.VMEM((D, tm), F32), pltpu.VMEM((te, tm), BF16)],
        compiler_params=_params("parallel", "arbitrary"),
        name="peer_dense",
    )(x, hft, s1, s2, stats, u, vt)


def _rmsnorm_kernel(x_ref, g_ref, o_ref):
    o_ref[...] = _rms(x_ref[...], g_ref[...])


def _rmsnorm(x, g, tm):
    T, D = x.shape
    return pl.pallas_call(
        _rmsnorm_kernel,
        grid=(T // tm,),
        in_specs=[pl.BlockSpec((tm, D), lambda i: (i, 0)), pl.BlockSpec((1, D), lambda i: (0, 0))],
        out_specs=pl.BlockSpec((tm, D), lambda i: (i, 0)),
        out_shape=jax.ShapeDtypeStruct((T, D), F32),
        compiler_params=_params("parallel"),
        name="final_rmsnorm",
    )(x, g)


def kernel(x, mem, positions, g_mix, w_in, w_ret_br, w_rnn_br, w_mix_out, conv_w, conv_b, lru_wa, lru_ba, lru_wx, lru_bx, lru_lam, g_x, g_mem, w_xq, w_xk, w_xv, w_xo, g_ffn, w_pq, sub_k1, sub_k2, peer_u, peer_v, g_final):
    B, T, D = x.shape
    assert B == 1 and D == D_MODEL
    depth = g_mix.shape[0]
    tm = min(TM_PROJ, T)
    xs = x.reshape(T, D)
    mems = mem.reshape(mem.shape[1], D)
    pos_col = positions.reshape(T, 1)
    row = lambda a: a.reshape(1, -1)

    cos2, sin2 = _rope_tables(pos_col, tm)
    for l in range(depth):
        z = _rms_matmul(xs, row(g_mix[l]), w_in[l].astype(BF16), tm=tm, tn=TN_PROJ, out_dtype=F32)
        ret = _retention(z, cos2, sin2, min(TB_RET, T))
        rnn = _rglru(z, pos_col, conv_w[l], row(conv_b[l]), lru_wa[l].astype(BF16), row(lru_ba[l]),
                     lru_wx[l].astype(BF16), row(lru_bx[l]), row(lru_lam[l]), min(TB_LRU, T))
        xs = _merge(xs, ret, rnn, z, w_ret_br[l].astype(BF16), w_rnn_br[l].astype(BF16),
                    w_mix_out[l].astype(BF16), tm)
        kmem = _rms_matmul(mems, row(g_mem[l]), w_xk[l].astype(BF16), tm=mems.shape[0], tn=D, out_dtype=BF16)
        vmem = _rms_matmul(mems, row(g_mem[l]), w_xv[l].astype(BF16), tm=mems.shape[0], tn=D, out_dtype=BF16)
        xs = _xattn(xs, row(g_x[l]), w_xq[l].astype(BF16), kmem.T, vmem, w_xo[l].astype(BF16), tm)
        tp = min(TM_PEER, T)
        hft, s1, s2 = _peer_scores(xs, row(g_ffn[l]), w_pq[l].T.astype(BF16), sub_k1[l].astype(BF16),
                                   sub_k2[l].astype(BF16), tp)
        stats = _peer_select(s1, s2, TL_SORT)
        xs = _peer_dense(xs, hft, s1, s2, stats, peer_u[l].astype(BF16), peer_v[l].T.astype(BF16),
                         tp, NA_PEER)
    out = _rmsnorm(xs, row(g_final), tm)
    return out.reshape(B, T, D)
```

```python
import functools
import math

import numpy as np
import jax
import jax.numpy as jnp
from jax import lax
from jax.experimental import pallas as pl
from jax.experimental.pallas import tpu as pltpu

F32 = jnp.float32
BF16 = jnp.bfloat16

D_MODEL = 1024
EPS = 1e-6
RET_HEADS = 4
RET_QK_DIM = 128
RET_V_DIM = 256
RET_CHUNK = 128
ROPE_BASE = 10000.0
LRU_BLOCKS = 8
LRU_BLOCK_DIM = D_MODEL // LRU_BLOCKS
CONV_WIDTH = 4
LRU_C = 8.0
XATTN_HEADS = 4
XATTN_HEAD_DIM = D_MODEL // XATTN_HEADS
PEER_HEADS = 8
PEER_KEYS = 128
PEER_TOPK = 16
PEER_HALF = 128

LANES = 128
SUBLANES = 8
VMEM_LIMIT_BYTES = 56 * 1024 * 1024

TM_PROJ = 512
TM_IN = 1024
TN_PROJ = 1024
TB_RET = 256
TB_LRU = 256
TM_PEER = 512
NA_PEER = 4
TL_SORT = 128


def _params(*sem):
    return pltpu.CompilerParams(dimension_semantics=sem, vmem_limit_bytes=VMEM_LIMIT_BYTES)


def _rms(x, g):
    return x * lax.rsqrt(jnp.mean(x * x, axis=-1, keepdims=True) + EPS) * g


def _gelu(x):
    c = math.sqrt(2.0 / math.pi)
    return 0.5 * x * (1.0 + jnp.tanh(c * (x + 0.044715 * (x * x * x))))


def _dot(a, b):
    return jnp.dot(a, b, preferred_element_type=F32)


def _dot_nt(a, b):
    return lax.dot_general(a, b, (((1,), (1,)), ((), ())), preferred_element_type=F32)


def _rms_matmul_kernel(x_ref, g_ref, w_ref, o_ref, h_ref):
    @pl.when(pl.program_id(1) == 0)
    def _():
        h_ref[...] = _rms(x_ref[...], g_ref[...]).astype(BF16)

    o_ref[...] = _dot(h_ref[...], w_ref[...]).astype(o_ref.dtype)


def _rms_matmul(x, g, w, *, tm, tn, out_dtype):
    T, K = x.shape
    N = w.shape[1]
    return pl.pallas_call(
        _rms_matmul_kernel,
        grid=(T // tm, N // tn),
        in_specs=[pl.BlockSpec((tm, K), lambda i, j: (i, 0)),
                  pl.BlockSpec((1, K), lambda i, j: (0, 0)),
                  pl.BlockSpec((K, tn), lambda i, j: (0, j))],
        out_specs=pl.BlockSpec((tm, tn), lambda i, j: (i, j)),
        out_shape=jax.ShapeDtypeStruct((T, N), out_dtype),
        scratch_shapes=[pltpu.VMEM((tm, K), BF16)],
        compiler_params=_params("parallel", "arbitrary"),
        name="rms_matmul",
    )(x, g, w)


def _rope_kernel(pos_ref, invf_ref, cos_ref, sin_ref):
    ang = pos_ref[...].astype(F32) * invf_ref[...]
    lane = lax.broadcasted_iota(jnp.int32, ang.shape, 1)
    s = jnp.sin(ang)
    cos_ref[...] = jnp.cos(ang)
    sin_ref[...] = jnp.where(lane < RET_QK_DIM // 2, -s, s)


def _rope_tables(pos_col, tb):
    T = pos_col.shape[0]
    half = RET_QK_DIM // 2
    inv_freq = ROPE_BASE ** (-jnp.arange(0, RET_QK_DIM, 2, dtype=F32) / RET_QK_DIM)
    invf2 = jnp.concatenate([inv_freq, inv_freq]).reshape(1, 2 * half)
    return pl.pallas_call(
        _rope_kernel,
        grid=(T // tb,),
        in_specs=[pl.BlockSpec((tb, 1), lambda i: (i, 0)),
                  pl.BlockSpec((1, RET_QK_DIM), lambda i: (0, 0))],
        out_specs=[pl.BlockSpec((tb, RET_QK_DIM), lambda i: (i, 0))] * 2,
        out_shape=[jax.ShapeDtypeStruct((T, RET_QK_DIM), F32)] * 2,
        compiler_params=_params("parallel"),
        name="rope_tables",
    )(pos_col, invf2)


def _retention_kernel(q_ref, k_ref, v_ref, g_ref, cos_ref, sin_ref, dec_ref, kdec_ref, qdec_ref,
                      o_ref, s_ref, *, chunk_decay):
    C, dk, dv = RET_CHUNK, RET_QK_DIM, RET_V_DIM

    @pl.when(pl.program_id(0) == 0)
    def _():
        s_ref[...] = jnp.zeros_like(s_ref)

    kscale = dk ** -0.5
    for c in range(q_ref.shape[0] // C):
        rows = slice(c * C, (c + 1) * C)
        cos = cos_ref[rows, :]
        sin = sin_ref[rows, :]
        for h in range(RET_HEADS):
            q = q_ref[rows, h * dk:(h + 1) * dk].astype(F32)
            k = k_ref[rows, h * dk:(h + 1) * dk].astype(F32)
            v = v_ref[rows, h * dv:(h + 1) * dv].astype(BF16)
            qr = q * cos + pltpu.roll(q, dk // 2, 1) * sin
            kr = (k * cos + pltpu.roll(k, dk // 2, 1) * sin) * kscale
            scores = _dot_nt(qr.astype(BF16), kr.astype(BF16)) * dec_ref[h]
            inner = _dot(scores.astype(BF16), v)
            state = s_ref[h]
            cross = _dot((qr * qdec_ref[h]).astype(BF16), state.astype(BF16))
            kd_t = jnp.transpose(kr * kdec_ref[h]).astype(BF16)
            s_ref[h] = chunk_decay[h] * state + _dot(kd_t, v)
            o = inner + cross
            mu = jnp.mean(o, axis=-1, keepdims=True)
            oc = o - mu
            var = jnp.mean(oc * oc, axis=-1, keepdims=True)
            o = oc * lax.rsqrt(var + EPS)
            g = g_ref[rows, h * dv:(h + 1) * dv].astype(F32)
            o_ref[rows, h * dv:(h + 1) * dv] = (g * jax.nn.sigmoid(g) * o).astype(o_ref.dtype)


def _retention(z, cos2, sin2, tb):
    T = z.shape[0]
    H, C, dk, dv = RET_HEADS, RET_CHUNK, RET_QK_DIM, RET_V_DIM
    log_g = jnp.log(1.0 - 2.0 ** (-5.0 - jnp.arange(H, dtype=F32)))
    i = jnp.arange(C, dtype=F32)
    diff = i[:, None] - i[None, :]
    decay = jnp.where(diff[None] >= 0, jnp.exp(diff[None] * log_g[:, None, None]), 0.0)
    k_decay = jnp.exp((C - 1.0 - i)[None, :] * log_g[:, None])
    q_decay = jnp.exp((i + 1.0)[None, :] * log_g[:, None])
    kdec = jnp.broadcast_to(k_decay[:, :, None], (H, C, dk))
    qdec = jnp.broadcast_to(q_decay[:, :, None], (H, C, dk))
    log_g_np = np.log(1.0 - 2.0 ** (-5.0 - np.arange(H, dtype=np.float32))).astype(np.float32)
    chunk_decay = tuple(float(np.exp(np.float32(C) * lg)) for lg in log_g_np)
    nq = H * dk
    const3 = lambda i: (0, 0, 0)
    return pl.pallas_call(
        functools.partial(_retention_kernel, chunk_decay=chunk_decay),
        grid=(T // tb,),
        in_specs=[pl.BlockSpec((tb, nq), lambda i: (i, 0)),
                  pl.BlockSpec((tb, nq), lambda i: (i, 1)),
                  pl.BlockSpec((tb, H * dv), lambda i: (i, 1)),
                  pl.BlockSpec((tb, H * dv), lambda i: (i, 2)),
                  pl.BlockSpec((tb, dk), lambda i: (i, 0)),
                  pl.BlockSpec((tb, dk), lambda i: (i, 0)),
                  pl.BlockSpec((H, C, C), const3),
                  pl.BlockSpec((H, C, dk), const3),
                  pl.BlockSpec((H, C, dk), const3)],
        out_specs=pl.BlockSpec((tb, H * dv), lambda i: (i, 0)),
        out_shape=jax.ShapeDtypeStruct((T, H * dv), BF16),
        scratch_shapes=[pltpu.VMEM((H, dk, dv), F32)],
        compiler_params=_params("arbitrary"),
        name="retention",
    )(z, z, z, z, cos2, sin2, decay, kdec, qdec)


def _rglru_kernel(x_ref, y_ref, pos_ref, cw_ref, cb_ref, wa_ref, ba_ref, wx_ref, bx_ref, lam_ref,
                  o_ref, xpad_ref, a_ref, b_ref, hc_ref):
    tb = x_ref.shape[0]
    pad = tb // 2
    halo = SUBLANES

    @pl.when(pl.program_id(0) == 0)
    def _():
        xpad_ref[0:halo, :] = jnp.zeros((halo, D_MODEL), F32)
        hc_ref[...] = jnp.zeros_like(hc_ref)
        a_ref[0:pad, :] = jnp.ones((pad, D_MODEL), F32)
        b_ref[0:pad, :] = jnp.zeros((pad, D_MODEL), F32)

    xpad_ref[halo:halo + tb, :] = x_ref[...].astype(F32)
    xc = cb_ref[...] + cw_ref[3:4, :] * xpad_ref[halo:halo + tb, :]
    for w in range(CONV_WIDTH - 1):
        off = halo - (CONV_WIDTH - 1) + w
        xc = xc + cw_ref[w:w + 1, :] * xpad_ref[off:off + tb, :]
    xpad_ref[0:halo, :] = xpad_ref[tb:tb + halo, :]

    reset = pos_ref[...] == 0
    lam = lam_ref[...]
    nl = -lam
    sp = jnp.maximum(nl, 0.0) + jnp.log(1.0 + jnp.exp(-jnp.abs(nl)))
    bd = LRU_BLOCK_DIM
    for n in range(LRU_BLOCKS):
        cols = slice(n * bd, (n + 1) * bd)
        xcn = xc[:, cols]
        xb = xcn.astype(BF16)
        gate_r = jax.nn.sigmoid(_dot(xb, wa_ref[n]) + ba_ref[:, cols])
        gate_i = jax.nn.sigmoid(_dot(xb, wx_ref[n]) + bx_ref[:, cols])
        log_a = (-LRU_C) * gate_r * sp[:, cols]
        a = jnp.exp(log_a)
        mult = jnp.sqrt(jnp.tanh(-log_a) * (a * a + 1.0))
        a = jnp.where(reset, 0.0, a)
        mult = jnp.where(reset, 1.0, mult)
        b = xcn * gate_i * mult
        s = 1
        while s < tb:
            a_ref[pad:pad + tb, cols] = a
            b_ref[pad:pad + tb, cols] = b
            a_sh = a_ref[pad - s:pad - s + tb, cols]
            b_sh = b_ref[pad - s:pad - s + tb, cols]
            b = a * b_sh + b
            a = a * a_sh
            s *= 2
        h = a * hc_ref[0:1, cols] + b
        hc_ref[:, cols] = jnp.broadcast_to(h[tb - 1:tb, :], (SUBLANES, bd))
        o_ref[:, cols] = (h * _gelu(y_ref[:, cols].astype(F32))).astype(o_ref.dtype)


def _rglru(z, pos_col, conv_w, conv_b, wa, ba, wx, bx, lam, tb):
    T = z.shape[0]
    W = D_MODEL
    row = lambda i: (0, 0)
    const3 = lambda i: (0, 0, 0)
    return pl.pallas_call(
        _rglru_kernel,
        grid=(T // tb,),
        in_specs=[pl.BlockSpec((tb, W), lambda i: (i, 3)),
                  pl.BlockSpec((tb, W), lambda i: (i, 4)),
                  pl.BlockSpec((tb, 1), lambda i: (i, 0)),
                  pl.BlockSpec((CONV_WIDTH, W), row),
                  pl.BlockSpec((1, W), row),
                  pl.BlockSpec((LRU_BLOCKS, LRU_BLOCK_DIM, LRU_BLOCK_DIM), const3),
                  pl.BlockSpec((1, W), row),
                  pl.BlockSpec((LRU_BLOCKS, LRU_BLOCK_DIM, LRU_BLOCK_DIM), const3),
                  pl.BlockSpec((1, W), row),
                  pl.BlockSpec((1, W), row)],
        out_specs=pl.BlockSpec((tb, W), lambda i: (i, 0)),
        out_shape=jax.ShapeDtypeStruct((T, W), BF16),
        scratch_shapes=[pltpu.VMEM((tb + 2 * SUBLANES, W), F32),
                        pltpu.VMEM((tb + tb // 2, W), F32),
                        pltpu.VMEM((tb + tb // 2, W), F32),
                        pltpu.VMEM((SUBLANES, W), F32)],
        compiler_params=_params("arbitrary"),
        name="rglru",
    )(z, z, pos_col, conv_w, conv_b, wa, ba, wx, bx, lam)


def _merge_kernel(x_ref, ret_ref, rnn_ref, ga_ref, gb_ref, wr_ref, wn_ref, wo_ref, o_ref):
    p_ret = _dot(ret_ref[...], wr_ref[...])
    p_rnn = _dot(rnn_ref[...], wn_ref[...])
    merged = (jax.nn.sigmoid(ga_ref[...].astype(F32)) * p_ret
              + jax.nn.sigmoid(gb_ref[...].astype(F32)) * p_rnn)
    o_ref[...] = x_ref[...] + _dot(merged.astype(BF16), wo_ref[...])


def _merge(x, ret, rnn, z, w_ret, w_rnn, w_out, tm):
    T, D = x.shape
    tile = lambda i: (i, 0)
    wspec = pl.BlockSpec((D, D), lambda i: (0, 0))
    return pl.pallas_call(
        _merge_kernel,
        grid=(T // tm,),
        in_specs=[pl.BlockSpec((tm, D), tile), pl.BlockSpec((tm, D), tile), pl.BlockSpec((tm, D), tile),
                  pl.BlockSpec((tm, D), lambda i: (i, 5)),
                  pl.BlockSpec((tm, D), lambda i: (i, 6)),
                  wspec, wspec, wspec],
        out_specs=pl.BlockSpec((tm, D), tile),
        out_shape=jax.ShapeDtypeStruct((T, D), F32),
        compiler_params=_params("parallel"),
        name="merge",
    )(x, ret, rnn, z, z, w_ret, w_rnn, w_out)


def _xattn_kernel(x_ref, g_ref, wq_ref, kt_ref, v_ref, wo_ref, o_ref, att_ref):
    x = x_ref[...]
    q = _dot(_rms(x, g_ref[...]).astype(BF16), wq_ref[...])
    dh = XATTN_HEAD_DIM
    for h in range(XATTN_HEADS):
        cols = slice(h * dh, (h + 1) * dh)
        s = _dot(q[:, cols].astype(BF16), kt_ref[cols, :]) * (dh ** -0.5)
        s = s - jnp.max(s, axis=-1, keepdims=True)
        e = jnp.exp(s)
        p = e / jnp.sum(e, axis=-1, keepdims=True)
        att_ref[:, cols] = _dot(p.astype(BF16), v_ref[:, cols]).astype(BF16)
    o_ref[...] = x + _dot(att_ref[...], wo_ref[...])


def _xattn(x, g, wq, kt, v, wo, tm):
    T, D = x.shape
    M = v.shape[0]
    tile = lambda i: (i, 0)
    fixed = lambda i: (0, 0)
    return pl.pallas_call(
        _xattn_kernel,
        grid=(T // tm,),
        in_specs=[pl.BlockSpec((tm, D), tile), pl.BlockSpec((1, D), fixed), pl.BlockSpec((D, D), fixed),
                  pl.BlockSpec((D, M), fixed), pl.BlockSpec((M, D), fixed), pl.BlockSpec((D, D), fixed)],
        out_specs=pl.BlockSpec((tm, D), tile),
        out_shape=jax.ShapeDtypeStruct((T, D), F32),
        scratch_shapes=[pltpu.VMEM((tm, D), BF16)],
        compiler_params=_params("parallel"),
        name="xattn",
    )(x, g, wq, kt, v, wo)


def _peer_scores_kernel(x_ref, g_ref, wq_ref, k1_ref, k2_ref, hft_ref, s1_ref, s2_ref):
    hf = _rms(x_ref[...], g_ref[...])
    hft = jnp.transpose(hf).astype(BF16)
    hft_ref[...] = hft
    qt = _dot(wq_ref[...], hft)
    kd = 2 * PEER_HALF
    for h in range(PEER_HEADS):
        q1 = qt[h * kd:h * kd + PEER_HALF, :].astype(BF16)
        q2 = qt[h * kd + PEER_HALF:(h + 1) * kd, :].astype(BF16)
        s1_ref[h] = _dot(k1_ref[h], q1)
        s2_ref[h] = _dot(k2_ref[h], q2)


def _peer_scores(x, g, wq_t, k1, k2, tm):
    T, D = x.shape
    H, NK = PEER_HEADS, PEER_KEYS
    fixed = lambda i: (0, 0)
    const3 = lambda i: (0, 0, 0)
    return pl.pallas_call(
        _peer_scores_kernel,
        grid=(T // tm,),
        in_specs=[pl.BlockSpec((tm, D), lambda i: (i, 0)), pl.BlockSpec((1, D), fixed),
                  pl.BlockSpec(wq_t.shape, fixed),
                  pl.BlockSpec((H, NK, PEER_HALF), const3), pl.BlockSpec((H, NK, PEER_HALF), const3)],
        out_specs=[pl.BlockSpec((D, tm), lambda i: (0, i)),
                   pl.BlockSpec((H, NK, tm), lambda i: (0, 0, i)),
                   pl.BlockSpec((H, NK, tm), lambda i: (0, 0, i))],
        out_shape=[jax.ShapeDtypeStruct((D, T), BF16),
                   jax.ShapeDtypeStruct((H, NK, T), F32),
                   jax.ShapeDtypeStruct((H, NK, T), F32)],
        compiler_params=_params("parallel"),
        name="peer_scores",
    )(x, g, wq_t, k1, k2)


def _sort_pairs(n):
    pairs = []
    p = 1
    while p < n:
        k = p
        while k >= 1:
            for j in range(k % p, n - k, 2 * k):
                for i in range(min(k, n - j - k)):
                    if (i + j) // (2 * p) == (i + j + k) // (2 * p):
                        pairs.append((i + j, i + j + k))
            k //= 2
        p *= 2
    return pairs


def _sort_desc(vs):
    vs = list(vs)
    for lo, hi in _sort_pairs(len(vs)):
        a, b = vs[lo], vs[hi]
        vs[lo], vs[hi] = jnp.maximum(a, b), jnp.minimum(a, b)
    return vs


def _merge_top(a, b):
    n = len(a)
    c = [jnp.maximum(a[i], b[n - 1 - i]) for i in range(n)]
    k = n // 2
    while k >= 1:
        for i in range(n):
            if i & k == 0:
                x, y = c[i], c[i + k]
                c[i], c[i + k] = jnp.maximum(x, y), jnp.minimum(x, y)
        k //= 2
    return c


def _top_of_keys(groups, roll):
    vs = _sort_desc(groups)
    shift = SUBLANES // 2
    while shift >= 1:
        vs = _merge_top(vs, [roll(v, shift, 0) for v in vs])
        shift //= 2
    return vs


def _peer_select_math(s1, s2, roll):
    K = PEER_TOPK
    shape = s1[0][0].shape
    sub = lax.broadcasted_iota(jnp.int32, shape, 0)
    zero = jnp.zeros(shape, F32)
    one = jnp.ones(shape, F32)
    tops2 = []
    v1 = [zero] * K
    v2 = [zero] * K
    for h in range(PEER_HEADS):
        t1 = _top_of_keys(s1[h], roll)
        t2 = _top_of_keys(s2[h], roll)
        tops2.append(t2)
        v1 = [jnp.where(sub == h, t, v) for t, v in zip(t1, v1)]
        v2 = [jnp.where(sub == h, t, v) for t, v in zip(t2, v2)]
    cands = [v1[i] + v2[j] for i in range(K) for j in range(K // (i + 1))]
    n = 1
    while n < len(cands):
        n *= 2
    cands = cands + [jnp.full(shape, -jnp.inf, F32)] * (n - len(cands))
    top = _sort_desc(cands)[:K]
    z = zero
    for t in top:
        z = z + jnp.exp(t - top[0])
    tau_all, zinv_all = top[K - 1], 1.0 / z
    out = []
    for h in range(PEER_HEADS):
        bc = lambda v: jnp.broadcast_to(v[h:h + 1, :], shape)
        tau, zinv, m1, m2 = bc(tau_all), bc(zinv_all), bc(v1[0]), bc(v2[0])
        t2 = tops2[h]
        cnt, p1, rk, e2 = [], [], [], []
        for s1g, s2g in zip(s1[h], s2[h]):
            c = zero
            r = zero
            for j in range(K):
                c = c + jnp.where(s1g + t2[j] >= tau, one, zero)
                r = r + jnp.where(t2[j] > s2g, one, zero)
            cnt.append(c)
            rk.append(r)
            p1.append(jnp.exp(s1g - m1) * zinv)
            e2.append(jnp.exp(s2g - m2))
        out.append((cnt, p1, rk, e2))
    return out


def _peer_select_kernel(s1_ref, s2_ref, n_ref, p1_ref, r2_ref, e2_ref):
    ng = PEER_KEYS // SUBLANES
    grp = lambda ref, h: [ref[h, i * SUBLANES:(i + 1) * SUBLANES, :] for i in range(ng)]
    s1 = [grp(s1_ref, h) for h in range(PEER_HEADS)]
    s2 = [grp(s2_ref, h) for h in range(PEER_HEADS)]
    res = _peer_select_math(s1, s2, pltpu.roll)
    pk = 2 * SUBLANES
    for h, (cnt, p1, rk, e2) in enumerate(res):
        for i in range(ng):
            rows = slice(i * SUBLANES, (i + 1) * SUBLANES)
            n_ref[h, rows, :] = cnt[i]
            p1_ref[h, rows, :] = p1[i]
        for i in range(ng // 2):
            rows = slice(i * pk, (i + 1) * pk)
            r2_ref[h, rows, :] = jnp.concatenate([rk[2 * i], rk[2 * i + 1]], axis=0).astype(BF16)
            e2_ref[h, rows, :] = jnp.concatenate([e2[2 * i], e2[2 * i + 1]], axis=0).astype(BF16)


def _peer_select(s1, s2, tl):
    H, NK, T = s1.shape
    spec = pl.BlockSpec((H, NK, tl), lambda i: (0, 0, i))
    return pl.pallas_call(
        _peer_select_kernel,
        grid=(T // tl,),
        in_specs=[spec] * 2,
        out_specs=[spec] * 4,
        out_shape=[jax.ShapeDtypeStruct((H, NK, T), F32), jax.ShapeDtypeStruct((H, NK, T), F32),
                   jax.ShapeDtypeStruct((H, NK, T), BF16), jax.ShapeDtypeStruct((H, NK, T), BF16)],
        compiler_params=_params("parallel"),
        name="peer_select",
    )(s1, s2)


def _peer_gate_rows(act, a, n_ref, p1_ref, r2_ref, e2_ref, w_ref, row0):
    NK = PEER_KEYS
    tm = act.shape[1]
    pk = 2 * SUBLANES
    zero = jnp.zeros((pk, tm), BF16)
    nrow = [jnp.broadcast_to(n_ref[h, pl.ds(a, 1), :], (pk, tm)).astype(BF16) for h in range(PEER_HEADS)]
    prow = [jnp.broadcast_to(p1_ref[h, pl.ds(a, 1), :], (pk, tm)).astype(BF16) for h in range(PEER_HEADS)]
    for g in range(NK // pk):
        b_rows = slice(g * pk, (g + 1) * pk)
        gate = zero
        for h in range(PEER_HEADS):
            sel = r2_ref[h, b_rows, :] < nrow[h]
            gate = gate + jnp.where(sel, prow[h] * e2_ref[h, b_rows, :], zero)
        w_ref[row0 + g * pk:row0 + (g + 1) * pk, :] = gate * _gelu(act[b_rows, :].astype(BF16))


def _peer_half(a0, hft, keys, u_ref, w_ref, vt_prev_ref, w_prev_ref, acc_ref, na):
    NK = PEER_KEYS
    dr = acc_ref.shape[0] // na
    w_prev = w_prev_ref[...]
    for ai in range(na):
        act = _dot(u_ref[ai * NK:(ai + 1) * NK, :], hft)
        rows = slice(ai * dr, (ai + 1) * dr)
        acc_ref[rows, :] += _dot(vt_prev_ref[rows, :], w_prev)
        _peer_gate_rows(act, a0 + ai, *keys, w_ref, ai * NK)


def _peer_dense_kernel(x_ref, hft_ref, n_ref, p1_ref, r2_ref, e2_ref, ua_ref, ub_ref, vta_ref, vtb_ref,
                       o_ref, acc_ref, wa_ref, wb_ref, *, na):
    j = pl.program_id(1)
    last = pl.num_programs(1) - 1
    keys = (n_ref, p1_ref, r2_ref, e2_ref)

    @pl.when(j == 0)
    def _():
        acc_ref[...] = jnp.zeros_like(acc_ref)
        wb_ref[...] = jnp.zeros_like(wb_ref)

    @pl.when(j < last)
    def _():
        hft = hft_ref[...]
        _peer_half(2 * j * na, hft, keys, ua_ref, wa_ref, vtb_ref, wb_ref, acc_ref, na)
        _peer_half((2 * j + 1) * na, hft, keys, ub_ref, wb_ref, vta_ref, wa_ref, acc_ref, na)

    @pl.when(j == last)
    def _():
        acc = acc_ref[...] + _dot(vtb_ref[...], wb_ref[...])
        o_ref[...] = x_ref[...] + jnp.transpose(acc)


def _peer_dense(x, hft, n1, p1, r2, e2, u, vt, tm, na):
    T, D = x.shape
    H, NK = PEER_HEADS, PEER_KEYS
    th = na * NK
    steps = NK // (2 * na)
    kspec = pl.BlockSpec((H, NK, tm), lambda i, j: (0, 0, i))
    cur = lambda j: jnp.minimum(j, steps - 1)
    return pl.pallas_call(
        functools.partial(_peer_dense_kernel, na=na),
        grid=(T // tm, steps + 1),
        in_specs=[pl.BlockSpec((tm, D), lambda i, j: (i, 0)),
                  pl.BlockSpec((D, tm), lambda i, j: (0, i)),
                  kspec, kspec, kspec, kspec,
                  pl.BlockSpec((th, D), lambda i, j: (2 * cur(j), 0)),
                  pl.BlockSpec((th, D), lambda i, j: (2 * cur(j) + 1, 0)),
                  pl.BlockSpec((D, th), lambda i, j: (0, 2 * cur(j))),
                  pl.BlockSpec((D, th), lambda i, j: (0, jnp.maximum(2 * j - 1, 1)))],
        out_specs=pl.BlockSpec((tm, D), lambda i, j: (i, 0)),
        out_shape=jax.ShapeDtypeStruct((T, D), F32),
        scratch_shapes=[pltpu.VMEM((D, tm), F32), pltpu.VMEM((th, tm), BF16), pltpu.VMEM((th, tm), BF16)],
        compiler_params=_params("parallel", "arbitrary"),
        name="peer_dense",
    )(x, hft, n1, p1, r2, e2, u, u, vt, vt)


def _rmsnorm_kernel(x_ref, g_ref, o_ref):
    o_ref[...] = _rms(x_ref[...], g_ref[...])


def _rmsnorm(x, g, tm):
    T, D = x.shape
    return pl.pallas_call(
        _rmsnorm_kernel,
        grid=(T // tm,),
        in_specs=[pl.BlockSpec((tm, D), lambda i: (i, 0)), pl.BlockSpec((1, D), lambda i: (0, 0))],
        out_specs=pl.BlockSpec((tm, D), lambda i: (i, 0)),
        out_shape=jax.ShapeDtypeStruct((T, D), F32),
        compiler_params=_params("parallel"),
        name="final_rmsnorm",
    )(x, g)


def kernel(x, mem, positions, g_mix, w_in, w_ret_br, w_rnn_br, w_mix_out, conv_w, conv_b, lru_wa, lru_ba, lru_wx, lru_bx, lru_lam, g_x, g_mem, w_xq, w_xk, w_xv, w_xo, g_ffn, w_pq, sub_k1, sub_k2, peer_u, peer_v, g_final):
    B, T, D = x.shape
    assert B == 1 and D == D_MODEL
    depth = g_mix.shape[0]
    tm = min(TM_PROJ, T)
    xs = x.reshape(T, D)
    mems = mem.reshape(mem.shape[1], D)
    pos_col = positions.reshape(T, 1)
    row = lambda a: a.reshape(1, -1)

    cos2, sin2 = _rope_tables(pos_col, tm)
    for l in range(depth):
        z = _rms_matmul(xs, row(g_mix[l]), w_in[l].astype(BF16), tm=min(TM_IN, T), tn=TN_PROJ, out_dtype=BF16)
        ret = _retention(z, cos2, sin2, min(TB_RET, T))
        rnn = _rglru(z, pos_col, conv_w[l], row(conv_b[l]), lru_wa[l].astype(BF16), row(lru_ba[l]),
                     lru_wx[l].astype(BF16), row(lru_bx[l]), row(lru_lam[l]), min(TB_LRU, T))
        xs = _merge(xs, ret, rnn, z, w_ret_br[l].astype(BF16), w_rnn_br[l].astype(BF16),
                    w_mix_out[l].astype(BF16), tm)
        kmem = _rms_matmul(mems, row(g_mem[l]), w_xk[l].astype(BF16), tm=mems.shape[0], tn=D, out_dtype=BF16)
        vmem = _rms_matmul(mems, row(g_mem[l]), w_xv[l].astype(BF16), tm=mems.shape[0], tn=D, out_dtype=BF16)
        xs = _xattn(xs, row(g_x[l]), w_xq[l].astype(BF16), kmem.T, vmem, w_xo[l].astype(BF16), tm)
        tp = min(TM_PEER, T)
        hft, s1, s2 = _peer_scores(xs, row(g_ffn[l]), w_pq[l].T.astype(BF16), sub_k1[l].astype(BF16),
                                   sub_k2[l].astype(BF16), tp)
        n1, p1, r2, e2 = _peer_select(s1, s2, TL_SORT)
        xs = _peer_dense(xs, hft, n1, p1, r2, e2, peer_u[l].astype(BF16), peer_v[l].T.astype(BF16),
                         tp, NA_PEER)
    out = _rmsnorm(xs, row(g_final), tm)
    return out.reshape(B, T, D)
```

```python
import functools
import math

import numpy as np
import jax
import jax.numpy as jnp
from jax import lax
from jax.experimental import pallas as pl
from jax.experimental.pallas import tpu as pltpu

F32 = jnp.float32
BF16 = jnp.bfloat16

D_MODEL = 1024
EPS = 1e-6
RET_HEADS = 4
RET_QK_DIM = 128
RET_V_DIM = 256
RET_CHUNK = 128
ROPE_BASE = 10000.0
LRU_BLOCKS = 8
LRU_BLOCK_DIM = D_MODEL // LRU_BLOCKS
CONV_WIDTH = 4
LRU_C = 8.0
XATTN_HEADS = 4
XATTN_HEAD_DIM = D_MODEL // XATTN_HEADS
PEER_HEADS = 8
PEER_KEYS = 128
PEER_TOPK = 16
PEER_HALF = 128

LANES = 128
SUBLANES = 8
VMEM_LIMIT_BYTES = 56 * 1024 * 1024

TM_PROJ = 512
TM_IN = 1024
TN_PROJ = 1024
TB_RET = 256
TB_LRU = 256
TM_PEER = 512
NA_PEER = 4
TL_SORT = 128


def _params(*sem):
    return pltpu.CompilerParams(dimension_semantics=sem, vmem_limit_bytes=VMEM_LIMIT_BYTES)


def _rms(x, g):
    return x * lax.rsqrt(jnp.mean(x * x, axis=-1, keepdims=True) + EPS) * g


def _gelu(x):
    c = math.sqrt(2.0 / math.pi)
    return 0.5 * x * (1.0 + jnp.tanh(c * (x + 0.044715 * (x * x * x))))


def _dot(a, b):
    return jnp.dot(a, b, preferred_element_type=F32)


def _dot_nt(a, b):
    return lax.dot_general(a, b, (((1,), (1,)), ((), ())), preferred_element_type=F32)


def _rms_matmul_kernel(x_ref, g_ref, w_ref, o_ref, h_ref):
    @pl.when(pl.program_id(1) == 0)
    def _():
        h_ref[...] = _rms(x_ref[...], g_ref[...]).astype(BF16)

    o_ref[...] = _dot(h_ref[...], w_ref[...]).astype(o_ref.dtype)


def _rms_matmul(x, g, w, *, tm, tn, out_dtype):
    T, K = x.shape
    N = w.shape[1]
    return pl.pallas_call(
        _rms_matmul_kernel,
        grid=(T // tm, N // tn),
        in_specs=[pl.BlockSpec((tm, K), lambda i, j: (i, 0)),
                  pl.BlockSpec((1, K), lambda i, j: (0, 0)),
                  pl.BlockSpec((K, tn), lambda i, j: (0, j))],
        out_specs=pl.BlockSpec((tm, tn), lambda i, j: (i, j)),
        out_shape=jax.ShapeDtypeStruct((T, N), out_dtype),
        scratch_shapes=[pltpu.VMEM((tm, K), BF16)],
        compiler_params=_params("parallel", "arbitrary"),
        name="rms_matmul",
    )(x, g, w)


def _rope_kernel(pos_ref, invf_ref, cos_ref, sin_ref):
    ang = pos_ref[...].astype(F32) * invf_ref[...]
    lane = lax.broadcasted_iota(jnp.int32, ang.shape, 1)
    s = jnp.sin(ang)
    cos_ref[...] = jnp.cos(ang)
    sin_ref[...] = jnp.where(lane < RET_QK_DIM // 2, -s, s)


def _rope_tables(pos_col, tb):
    T = pos_col.shape[0]
    half = RET_QK_DIM // 2
    inv_freq = ROPE_BASE ** (-jnp.arange(0, RET_QK_DIM, 2, dtype=F32) / RET_QK_DIM)
    invf2 = jnp.concatenate([inv_freq, inv_freq]).reshape(1, 2 * half)
    return pl.pallas_call(
        _rope_kernel,
        grid=(T // tb,),
        in_specs=[pl.BlockSpec((tb, 1), lambda i: (i, 0)),
                  pl.BlockSpec((1, RET_QK_DIM), lambda i: (0, 0))],
        out_specs=[pl.BlockSpec((tb, RET_QK_DIM), lambda i: (i, 0))] * 2,
        out_shape=[jax.ShapeDtypeStruct((T, RET_QK_DIM), F32)] * 2,
        compiler_params=_params("parallel"),
        name="rope_tables",
    )(pos_col, invf2)


def _retention_kernel(q_ref, k_ref, v_ref, g_ref, cos_ref, sin_ref, dec_ref, kdec_ref, qdec_ref,
                      o_ref, s_ref, *, chunk_decay):
    C, dk, dv = RET_CHUNK, RET_QK_DIM, RET_V_DIM

    @pl.when(pl.program_id(0) == 0)
    def _():
        s_ref[...] = jnp.zeros_like(s_ref)

    kscale = dk ** -0.5
    for c in range(q_ref.shape[0] // C):
        rows = slice(c * C, (c + 1) * C)
        cos = cos_ref[rows, :]
        sin = sin_ref[rows, :]
        for h in range(RET_HEADS):
            q = q_ref[rows, h * dk:(h + 1) * dk].astype(F32)
            k = k_ref[rows, h * dk:(h + 1) * dk].astype(F32)
            v = v_ref[rows, h * dv:(h + 1) * dv].astype(BF16)
            qr = q * cos + pltpu.roll(q, dk // 2, 1) * sin
            kr = (k * cos + pltpu.roll(k, dk // 2, 1) * sin) * kscale
            scores = _dot_nt(qr.astype(BF16), kr.astype(BF16)) * dec_ref[h]
            inner = _dot(scores.astype(BF16), v)
            state = s_ref[h]
            cross = _dot((qr * qdec_ref[h]).astype(BF16), state.astype(BF16))
            kd_t = jnp.transpose(kr * kdec_ref[h]).astype(BF16)
            s_ref[h] = chunk_decay[h] * state + _dot(kd_t, v)
            o = inner + cross
            mu = jnp.mean(o, axis=-1, keepdims=True)
            oc = o - mu
            var = jnp.mean(oc * oc, axis=-1, keepdims=True)
            o = oc * lax.rsqrt(var + EPS)
            g = g_ref[rows, h * dv:(h + 1) * dv].astype(F32)
            o_ref[rows, h * dv:(h + 1) * dv] = (g * jax.nn.sigmoid(g) * o).astype(o_ref.dtype)


def _retention(z, cos2, sin2, tb):
    T = z.shape[0]
    H, C, dk, dv = RET_HEADS, RET_CHUNK, RET_QK_DIM, RET_V_DIM
    log_g = jnp.log(1.0 - 2.0 ** (-5.0 - jnp.arange(H, dtype=F32)))
    i = jnp.arange(C, dtype=F32)
    diff = i[:, None] - i[None, :]
    decay = jnp.where(diff[None] >= 0, jnp.exp(diff[None] * log_g[:, None, None]), 0.0)
    k_decay = jnp.exp((C - 1.0 - i)[None, :] * log_g[:, None])
    q_decay = jnp.exp((i + 1.0)[None, :] * log_g[:, None])
    kdec = jnp.broadcast_to(k_decay[:, :, None], (H, C, dk))
    qdec = jnp.broadcast_to(q_decay[:, :, None], (H, C, dk))
    log_g_np = np.log(1.0 - 2.0 ** (-5.0 - np.arange(H, dtype=np.float32))).astype(np.float32)
    chunk_decay = tuple(float(np.exp(np.float32(C) * lg)) for lg in log_g_np)
    nq = H * dk
    const3 = lambda i: (0, 0, 0)
    return pl.pallas_call(
        functools.partial(_retention_kernel, chunk_decay=chunk_decay),
        grid=(T // tb,),
        in_specs=[pl.BlockSpec((tb, nq), lambda i: (i, 0)),
                  pl.BlockSpec((tb, nq), lambda i: (i, 1)),
                  pl.BlockSpec((tb, H * dv), lambda i: (i, 1)),
                  pl.BlockSpec((tb, H * dv), lambda i: (i, 2)),
                  pl.BlockSpec((tb, dk), lambda i: (i, 0)),
                  pl.BlockSpec((tb, dk), lambda i: (i, 0)),
                  pl.BlockSpec((H, C, C), const3),
                  pl.BlockSpec((H, C, dk), const3),
                  pl.BlockSpec((H, C, dk), const3)],
        out_specs=pl.BlockSpec((tb, H * dv), lambda i: (i, 0)),
        out_shape=jax.ShapeDtypeStruct((T, H * dv), BF16),
        scratch_shapes=[pltpu.VMEM((H, dk, dv), F32)],
        compiler_params=_params("arbitrary"),
        name="retention",
    )(z, z, z, z, cos2, sin2, decay, kdec, qdec)


def _rglru_kernel(x_ref, y_ref, pos_ref, cw_ref, cb_ref, wa_ref, ba_ref, wx_ref, bx_ref, lam_ref,
                  o_ref, xpad_ref, a_ref, b_ref, hc_ref):
    tb = x_ref.shape[0]
    pad = tb // 2
    halo = SUBLANES

    @pl.when(pl.program_id(0) == 0)
    def _():
        xpad_ref[0:halo, :] = jnp.zeros((halo, D_MODEL), F32)
        hc_ref[...] = jnp.zeros_like(hc_ref)
        a_ref[0:pad, :] = jnp.ones((pad, D_MODEL), F32)
        b_ref[0:pad, :] = jnp.zeros((pad, D_MODEL), F32)

    xpad_ref[halo:halo + tb, :] = x_ref[...].astype(F32)
    xc = cb_ref[...] + cw_ref[3:4, :] * xpad_ref[halo:halo + tb, :]
    for w in range(CONV_WIDTH - 1):
        off = halo - (CONV_WIDTH - 1) + w
        xc = xc + cw_ref[w:w + 1, :] * xpad_ref[off:off + tb, :]
    xpad_ref[0:halo, :] = xpad_ref[tb:tb + halo, :]

    reset = pos_ref[...] == 0
    lam = lam_ref[...]
    nl = -lam
    sp = jnp.maximum(nl, 0.0) + jnp.log(1.0 + jnp.exp(-jnp.abs(nl)))
    bd = LRU_BLOCK_DIM
    for n in range(LRU_BLOCKS):
        cols = slice(n * bd, (n + 1) * bd)
        xcn = xc[:, cols]
        xb = xcn.astype(BF16)
        gate_r = jax.nn.sigmoid(_dot(xb, wa_ref[n]) + ba_ref[:, cols])
        gate_i = jax.nn.sigmoid(_dot(xb, wx_ref[n]) + bx_ref[:, cols])
        log_a = (-LRU_C) * gate_r * sp[:, cols]
        a = jnp.exp(log_a)
        mult = jnp.sqrt(jnp.tanh(-log_a) * (a * a + 1.0))
        a = jnp.where(reset, 0.0, a)
        mult = jnp.where(reset, 1.0, mult)
        b = xcn * gate_i * mult
        s = 1
        while s < tb:
            a_ref[pad:pad + tb, cols] = a
            b_ref[pad:pad + tb, cols] = b
            a_sh = a_ref[pad - s:pad - s + tb, cols]
            b_sh = b_ref[pad - s:pad - s + tb, cols]
            b = a * b_sh + b
            a = a * a_sh
            s *= 2
        h = a * hc_ref[0:1, cols] + b
        hc_ref[:, cols] = jnp.broadcast_to(h[tb - 1:tb, :], (SUBLANES, bd))
        o_ref[:, cols] = (h * _gelu(y_ref[:, cols].astype(F32))).astype(o_ref.dtype)


def _rglru(z, pos_col, conv_w, conv_b, wa, ba, wx, bx, lam, tb):
    T = z.shape[0]
    W = D_MODEL
    row = lambda i: (0, 0)
    const3 = lambda i: (0, 0, 0)
    return pl.pallas_call(
        _rglru_kernel,
        grid=(T // tb,),
        in_specs=[pl.BlockSpec((tb, W), lambda i: (i, 3)),
                  pl.BlockSpec((tb, W), lambda i: (i, 4)),
                  pl.BlockSpec((tb, 1), lambda i: (i, 0)),
                  pl.BlockSpec((CONV_WIDTH, W), row),
                  pl.BlockSpec((1, W), row),
                  pl.BlockSpec((LRU_BLOCKS, LRU_BLOCK_DIM, LRU_BLOCK_DIM), const3),
                  pl.BlockSpec((1, W), row),
                  pl.BlockSpec((LRU_BLOCKS, LRU_BLOCK_DIM, LRU_BLOCK_DIM), const3),
                  pl.BlockSpec((1, W), row),
                  pl.BlockSpec((1, W), row)],
        out_specs=pl.BlockSpec((tb, W), lambda i: (i, 0)),
        out_shape=jax.ShapeDtypeStruct((T, W), BF16),
        scratch_shapes=[pltpu.VMEM((tb + 2 * SUBLANES, W), F32),
                        pltpu.VMEM((tb + tb // 2, W), F32),
                        pltpu.VMEM((tb + tb // 2, W), F32),
                        pltpu.VMEM((SUBLANES, W), F32)],
        compiler_params=_params("arbitrary"),
        name="rglru",
    )(z, z, pos_col, conv_w, conv_b, wa, ba, wx, bx, lam)


def _merge_kernel(x_ref, ret_ref, rnn_ref, ga_ref, gb_ref, wr_ref, wn_ref, wo_ref, o_ref):
    p_ret = _dot(ret_ref[...], wr_ref[...])
    p_rnn = _dot(rnn_ref[...], wn_ref[...])
    merged = (jax.nn.sigmoid(ga_ref[...].astype(F32)) * p_ret
              + jax.nn.sigmoid(gb_ref[...].astype(F32)) * p_rnn)
    o_ref[...] = x_ref[...] + _dot(merged.astype(BF16), wo_ref[...])


def _merge(x, ret, rnn, z, w_ret, w_rnn, w_out, tm):
    T, D = x.shape
    tile = lambda i: (i, 0)
    wspec = pl.BlockSpec((D, D), lambda i: (0, 0))
    return pl.pallas_call(
        _merge_kernel,
        grid=(T // tm,),
        in_specs=[pl.BlockSpec((tm, D), tile), pl.BlockSpec((tm, D), tile), pl.BlockSpec((tm, D), tile),
                  pl.BlockSpec((tm, D), lambda i: (i, 5)),
                  pl.BlockSpec((tm, D), lambda i: (i, 6)),
                  wspec, wspec, wspec],
        out_specs=pl.BlockSpec((tm, D), tile),
        out_shape=jax.ShapeDtypeStruct((T, D), F32),
        compiler_params=_params("parallel"),
        name="merge",
    )(x, ret, rnn, z, z, w_ret, w_rnn, w_out)


def _xattn_kernel(x_ref, g_ref, wq_ref, kt_ref, v_ref, wo_ref, o_ref, att_ref):
    x = x_ref[...]
    q = _dot(_rms(x, g_ref[...]).astype(BF16), wq_ref[...])
    dh = XATTN_HEAD_DIM
    for h in range(XATTN_HEADS):
        cols = slice(h * dh, (h + 1) * dh)
        s = _dot(q[:, cols].astype(BF16), kt_ref[cols, :]) * (dh ** -0.5)
        s = s - jnp.max(s, axis=-1, keepdims=True)
        e = jnp.exp(s)
        p = e / jnp.sum(e, axis=-1, keepdims=True)
        att_ref[:, cols] = _dot(p.astype(BF16), v_ref[:, cols]).astype(BF16)
    o_ref[...] = x + _dot(att_ref[...], wo_ref[...])


def _xattn(x, g, wq, kt, v, wo, tm):
    T, D = x.shape
    M = v.shape[0]
    tile = lambda i: (i, 0)
    fixed = lambda i: (0, 0)
    return pl.pallas_call(
        _xattn_kernel,
        grid=(T // tm,),
        in_specs=[pl.BlockSpec((tm, D), tile), pl.BlockSpec((1, D), fixed), pl.BlockSpec((D, D), fixed),
                  pl.BlockSpec((D, M), fixed), pl.BlockSpec((M, D), fixed), pl.BlockSpec((D, D), fixed)],
        out_specs=pl.BlockSpec((tm, D), tile),
        out_shape=jax.ShapeDtypeStruct((T, D), F32),
        scratch_shapes=[pltpu.VMEM((tm, D), BF16)],
        compiler_params=_params("parallel"),
        name="xattn",
    )(x, g, wq, kt, v, wo)


def _peer_scores_kernel(x_ref, g_ref, wq_ref, k1_ref, k2_ref, hft_ref, s1_ref, s2_ref):
    hf = _rms(x_ref[...], g_ref[...])
    hft = jnp.transpose(hf).astype(BF16)
    hft_ref[...] = hft
    qt = _dot(wq_ref[...], hft)
    kd = 2 * PEER_HALF
    for h in range(PEER_HEADS):
        q1 = qt[h * kd:h * kd + PEER_HALF, :].astype(BF16)
        q2 = qt[h * kd + PEER_HALF:(h + 1) * kd, :].astype(BF16)
        s1_ref[h] = _dot(k1_ref[h], q1)
        s2_ref[h] = _dot(k2_ref[h], q2)


def _peer_scores(x, g, wq_t, k1, k2, tm):
    T, D = x.shape
    H, NK = PEER_HEADS, PEER_KEYS
    fixed = lambda i: (0, 0)
    const3 = lambda i: (0, 0, 0)
    return pl.pallas_call(
        _peer_scores_kernel,
        grid=(T // tm,),
        in_specs=[pl.BlockSpec((tm, D), lambda i: (i, 0)), pl.BlockSpec((1, D), fixed),
                  pl.BlockSpec(wq_t.shape, fixed),
                  pl.BlockSpec((H, NK, PEER_HALF), const3), pl.BlockSpec((H, NK, PEER_HALF), const3)],
        out_specs=[pl.BlockSpec((D, tm), lambda i: (0, i)),
                   pl.BlockSpec((H, NK, tm), lambda i: (0, 0, i)),
                   pl.BlockSpec((H, NK, tm), lambda i: (0, 0, i))],
        out_shape=[jax.ShapeDtypeStruct((D, T), BF16),
                   jax.ShapeDtypeStruct((H, NK, T), F32),
                   jax.ShapeDtypeStruct((H, NK, T), F32)],
        compiler_params=_params("parallel"),
        name="peer_scores",
    )(x, g, wq_t, k1, k2)


def _sort_pairs(n):
    pairs = []
    p = 1
    while p < n:
        k = p
        while k >= 1:
            for j in range(k % p, n - k, 2 * k):
                for i in range(min(k, n - j - k)):
                    if (i + j) // (2 * p) == (i + j + k) // (2 * p):
                        pairs.append((i + j, i + j + k))
            k //= 2
        p *= 2
    return pairs


def _sort_desc(vs):
    vs = list(vs)
    for lo, hi in _sort_pairs(len(vs)):
        a, b = vs[lo], vs[hi]
        vs[lo], vs[hi] = jnp.maximum(a, b), jnp.minimum(a, b)
    return vs


def _merge_top(a, b):
    n = len(a)
    c = [jnp.maximum(a[i], b[n - 1 - i]) for i in range(n)]
    k = n // 2
    while k >= 1:
        for i in range(n):
            if i & k == 0:
                x, y = c[i], c[i + k]
                c[i], c[i + k] = jnp.maximum(x, y), jnp.minimum(x, y)
        k //= 2
    return c


def _top_of_keys(groups, roll):
    vs = _sort_desc(groups)
    shift = SUBLANES // 2
    while shift >= 1:
        vs = _merge_top(vs, [roll(v, shift, 0) for v in vs])
        shift //= 2
    return vs


def _peer_select_math(s1, s2, roll):
    K = PEER_TOPK
    shape = s1[0][0].shape
    sub = lax.broadcasted_iota(jnp.int32, shape, 0)
    zero = jnp.zeros(shape, F32)
    tops2 = []
    v1 = [zero] * K
    v2 = [zero] * K
    for h in range(PEER_HEADS):
        t1 = _top_of_keys(s1[h], roll)
        t2 = _top_of_keys(s2[h], roll)
        tops2.append(t2)
        v1 = [jnp.where(sub == h, t, v) for t, v in zip(t1, v1)]
        v2 = [jnp.where(sub == h, t, v) for t, v in zip(t2, v2)]
    cands = [v1[i] + v2[j] for i in range(K) for j in range(K // (i + 1))]
    n = 1
    while n < len(cands):
        n *= 2
    cands = cands + [jnp.full(shape, -jnp.inf, F32)] * (n - len(cands))
    top = _sort_desc(cands)[:K]
    z = zero
    for t in top:
        z = z + jnp.exp(t - top[0])
    tau_all, zinv_all = top[K - 1], 1.0 / z
    out = []
    for h in range(PEER_HEADS):
        bc = lambda v: jnp.broadcast_to(v[h:h + 1, :], shape)
        tau, zinv, m1, m2 = bc(tau_all), bc(zinv_all), bc(v1[0]), bc(v2[0])
        t2 = tops2[h]
        cnt, p1, rk, e2 = [], [], [], []
        for s1g, s2g in zip(s1[h], s2[h]):
            c = zero
            r = zero
            for j in range(K):
                c = jnp.where(s1g + t2[j] >= tau, float(j + 1), c)
                r = jnp.where(t2[j] > s2g, float(j + 1), r)
            cnt.append(c)
            rk.append(r)
            p1.append(jnp.exp(s1g - m1) * zinv)
            e2.append(jnp.exp(s2g - m2))
        out.append((cnt, p1, rk, e2))
    return out


def _bf16_pair(x):
    u = lax.bitcast_convert_type(x.astype(BF16).astype(F32), jnp.uint32)
    return u | (u >> 16)


def _peer_select_kernel(s1_ref, s2_ref, n_ref, p1_ref, r2_ref, e2_ref):
    ng = PEER_KEYS // SUBLANES
    grp = lambda ref, h: [ref[h, i * SUBLANES:(i + 1) * SUBLANES, :] for i in range(ng)]
    s1 = [grp(s1_ref, h) for h in range(PEER_HEADS)]
    s2 = [grp(s2_ref, h) for h in range(PEER_HEADS)]
    res = _peer_select_math(s1, s2, pltpu.roll)
    pk = 2 * SUBLANES
    for h, (cnt, p1, rk, e2) in enumerate(res):
        for i in range(ng):
            rows = slice(i * SUBLANES, (i + 1) * SUBLANES)
            n_ref[h, rows, :] = _bf16_pair(cnt[i])
            p1_ref[h, rows, :] = _bf16_pair(p1[i])
        for i in range(ng // 2):
            rows = slice(i * pk, (i + 1) * pk)
            r2_ref[h, rows, :] = jnp.concatenate([rk[2 * i], rk[2 * i + 1]], axis=0).astype(BF16)
            e2_ref[h, rows, :] = jnp.concatenate([e2[2 * i], e2[2 * i + 1]], axis=0).astype(BF16)


def _peer_select(s1, s2, tl):
    H, NK, T = s1.shape
    spec = pl.BlockSpec((H, NK, tl), lambda i: (0, 0, i))
    return pl.pallas_call(
        _peer_select_kernel,
        grid=(T // tl,),
        in_specs=[spec] * 2,
        out_specs=[spec] * 4,
        out_shape=[jax.ShapeDtypeStruct((H, NK, T), jnp.uint32), jax.ShapeDtypeStruct((H, NK, T), jnp.uint32),
                   jax.ShapeDtypeStruct((H, NK, T), BF16), jax.ShapeDtypeStruct((H, NK, T), BF16)],
        compiler_params=_params("parallel"),
        name="peer_select",
    )(s1, s2)


def _peer_gate_rows(act, a, n_ref, p1_ref, r2_ref, e2_ref, w_ref, row0):
    NK = PEER_KEYS
    tm = act.shape[-1]
    pk = 2 * SUBLANES
    zero = jnp.zeros((pk, tm), BF16)
    bc = lambda ref, h: pltpu.bitcast(jnp.broadcast_to(ref[h, pl.ds(a, 1), :], (SUBLANES, tm)), BF16)
    nrow = [bc(n_ref, h) for h in range(PEER_HEADS)]
    prow = [bc(p1_ref, h) for h in range(PEER_HEADS)]
    for g in range(NK // pk):
        b_rows = slice(g * pk, (g + 1) * pk)
        gate = zero
        for h in range(PEER_HEADS):
            sel = r2_ref[h, b_rows, :] < nrow[h]
            gate = gate + jnp.where(sel, prow[h] * e2_ref[h, b_rows, :], zero)
        w_ref[row0 + g * pk:row0 + (g + 1) * pk, :] = gate * act[b_rows, :]


def _peer_half(a0, hft, keys, u_ref, w_ref, vt_prev_ref, w_prev_ref, acc_ref, gl_ref, na):
    NK = PEER_KEYS
    dr = acc_ref.shape[0] // na
    w_prev = w_prev_ref[...]
    for ai in range(na):
        gl_ref[ai] = _gelu(_dot(u_ref[ai * NK:(ai + 1) * NK, :], hft).astype(BF16))
        rows = slice(ai * dr, (ai + 1) * dr)
        acc_ref[rows, :] += _dot(vt_prev_ref[rows, :], w_prev)
        _peer_gate_rows(gl_ref.at[ai], a0 + ai, *keys, w_ref, ai * NK)


def _peer_dense_kernel(x_ref, hft_ref, n_ref, p1_ref, r2_ref, e2_ref, ua_ref, ub_ref, vta_ref, vtb_ref,
                       *rest, na, norm_out):
    g_ref = rest[0] if norm_out else None
    o_ref, acc_ref, wa_ref, wb_ref, gla_ref, glb_ref = rest[-6:]
    j = pl.program_id(1)
    last = pl.num_programs(1) - 1
    keys = (n_ref, p1_ref, r2_ref, e2_ref)

    @pl.when(j == 0)
    def _():
        acc_ref[...] = jnp.zeros_like(acc_ref)
        wb_ref[...] = jnp.zeros_like(wb_ref)

    @pl.when(j < last)
    def _():
        hft = hft_ref[...]
        _peer_half(2 * j * na, hft, keys, ua_ref, wa_ref, vtb_ref, wb_ref, acc_ref, gla_ref, na)
        _peer_half((2 * j + 1) * na, hft, keys, ub_ref, wb_ref, vta_ref, wa_ref, acc_ref, glb_ref, na)

    @pl.when(j == last)
    def _():
        acc = acc_ref[...] + _dot(vtb_ref[...], wb_ref[...])
        xo = x_ref[...] + jnp.transpose(acc)
        o_ref[...] = _rms(xo, g_ref[...]) if norm_out else xo


def _peer_dense(x, hft, n1, p1, r2, e2, u, vt, tm, na, g_out=None):
    T, D = x.shape
    H, NK = PEER_HEADS, PEER_KEYS
    th = na * NK
    steps = NK // (2 * na)
    kspec = pl.BlockSpec((H, NK, tm), lambda i, j: (0, 0, i))
    cur = lambda j: jnp.minimum(j, steps - 1)
    return pl.pallas_call(
        functools.partial(_peer_dense_kernel, na=na, norm_out=g_out is not None),
        grid=(T // tm, steps + 1),
        in_specs=[pl.BlockSpec((tm, D), lambda i, j: (i, 0)),
                  pl.BlockSpec((D, tm), lambda i, j: (0, i)),
                  kspec, kspec, kspec, kspec,
                  pl.BlockSpec((th, D), lambda i, j: (2 * cur(j), 0)),
                  pl.BlockSpec((th, D), lambda i, j: (2 * cur(j) + 1, 0)),
                  pl.BlockSpec((D, th), lambda i, j: (0, 2 * cur(j))),
                  pl.BlockSpec((D, th), lambda i, j: (0, jnp.maximum(2 * j - 1, 1)))]
                 + ([pl.BlockSpec((1, D), lambda i, j: (0, 0))] if g_out is not None else []),
        out_specs=pl.BlockSpec((tm, D), lambda i, j: (i, 0)),
        out_shape=jax.ShapeDtypeStruct((T, D), F32),
        scratch_shapes=[pltpu.VMEM((D, tm), F32), pltpu.VMEM((th, tm), BF16), pltpu.VMEM((th, tm), BF16),
                        pltpu.VMEM((na, NK, tm), BF16), pltpu.VMEM((na, NK, tm), BF16)],
        compiler_params=_params("parallel", "arbitrary"),
        name="peer_dense",
    )(x, hft, n1, p1, r2, e2, u, u, vt, vt, *(() if g_out is None else (g_out,)))


def kernel(x, mem, positions, g_mix, w_in, w_ret_br, w_rnn_br, w_mix_out, conv_w, conv_b, lru_wa, lru_ba, lru_wx, lru_bx, lru_lam, g_x, g_mem, w_xq, w_xk, w_xv, w_xo, g_ffn, w_pq, sub_k1, sub_k2, peer_u, peer_v, g_final):
    B, T, D = x.shape
    assert B == 1 and D == D_MODEL
    depth = g_mix.shape[0]
    tm = min(TM_PROJ, T)
    xs = x.reshape(T, D)
    mems = mem.reshape(mem.shape[1], D)
    pos_col = positions.reshape(T, 1)
    row = lambda a: a.reshape(1, -1)

    cos2, sin2 = _rope_tables(pos_col, tm)
    for l in range(depth):
        z = _rms_matmul(xs, row(g_mix[l]), w_in[l].astype(BF16), tm=min(TM_IN, T), tn=TN_PROJ, out_dtype=BF16)
        ret = _retention(z, cos2, sin2, min(TB_RET, T))
        rnn = _rglru(z, pos_col, conv_w[l], row(conv_b[l]), lru_wa[l].astype(BF16), row(lru_ba[l]),
                     lru_wx[l].astype(BF16), row(lru_bx[l]), row(lru_lam[l]), min(TB_LRU, T))
        xs = _merge(xs, ret, rnn, z, w_ret_br[l].astype(BF16), w_rnn_br[l].astype(BF16),
                    w_mix_out[l].astype(BF16), tm)
        kmem = _rms_matmul(mems, row(g_mem[l]), w_xk[l].astype(BF16), tm=mems.shape[0], tn=D, out_dtype=BF16)
        vmem = _rms_matmul(mems, row(g_mem[l]), w_xv[l].astype(BF16), tm=mems.shape[0], tn=D, out_dtype=BF16)
        xs = _xattn(xs, row(g_x[l]), w_xq[l].astype(BF16), kmem.T, vmem, w_xo[l].astype(BF16), tm)
        tp = min(TM_PEER, T)
        hft, s1, s2 = _peer_scores(xs, row(g_ffn[l]), w_pq[l].T.astype(BF16), sub_k1[l].astype(BF16),
                                   sub_k2[l].astype(BF16), tp)
        n1, p1, r2, e2 = _peer_select(s1, s2, TL_SORT)
        xs = _peer_dense(xs, hft, n1, p1, r2, e2, peer_u[l].astype(BF16), peer_v[l].T.astype(BF16),
                         tp, NA_PEER, g_out=row(g_final) if l == depth - 1 else None)
    return xs.reshape(B, T, D)
```

```python
import functools
import math

import numpy as np
import jax
import jax.numpy as jnp
from jax import lax
from jax.experimental import pallas as pl
from jax.experimental.pallas import tpu as pltpu

F32 = jnp.float32
BF16 = jnp.bfloat16

D_MODEL = 1024
EPS = 1e-6
RET_HEADS = 4
RET_QK_DIM = 128
RET_V_DIM = 256
RET_CHUNK = 128
ROPE_BASE = 10000.0
LRU_BLOCKS = 8
LRU_BLOCK_DIM = D_MODEL // LRU_BLOCKS
CONV_WIDTH = 4
LRU_C = 8.0
XATTN_HEADS = 4
XATTN_HEAD_DIM = D_MODEL // XATTN_HEADS
PEER_HEADS = 8
PEER_KEYS = 128
PEER_TOPK = 16
PEER_HALF = 128

LANES = 128
SUBLANES = 8
VMEM_LIMIT_BYTES = 56 * 1024 * 1024

TM_PROJ = 512
TM_IN = 2048
TN_PROJ = 1024
TB_RET = 256
TB_LRU = 256
TM_PEER = 512
NA_PEER = 4
TL_SORT = 128


def _params(*sem):
    return pltpu.CompilerParams(dimension_semantics=sem, vmem_limit_bytes=VMEM_LIMIT_BYTES)


def _rms(x, g):
    return x * lax.rsqrt(jnp.mean(x * x, axis=-1, keepdims=True) + EPS) * g


def _gelu(x):
    c = math.sqrt(2.0 / math.pi)
    return 0.5 * x * (1.0 + jnp.tanh(c * (x + 0.044715 * (x * x * x))))


def _dot(a, b):
    return jnp.dot(a, b, preferred_element_type=F32)


def _dot_nt(a, b):
    return lax.dot_general(a, b, (((1,), (1,)), ((), ())), preferred_element_type=F32)


def _rms_matmul_kernel(x_ref, g_ref, w_ref, o_ref, h_ref):
    @pl.when(pl.program_id(1) == 0)
    def _():
        h_ref[...] = _rms(x_ref[...], g_ref[...]).astype(BF16)

    o_ref[...] = _dot(h_ref[...], w_ref[...].astype(BF16)).astype(o_ref.dtype)


def _rms_matmul(x, g, w, layer, *, tm, tn, out_dtype):
    T, K = x.shape
    N = w.shape[2]
    return pl.pallas_call(
        _rms_matmul_kernel,
        grid=(T // tm, N // tn),
        in_specs=[pl.BlockSpec((tm, K), lambda i, j: (i, 0)),
                  pl.BlockSpec((1, K), lambda i, j: (0, 0)),
                  pl.BlockSpec((None, K, tn), lambda i, j: (layer, 0, j))],
        out_specs=pl.BlockSpec((tm, tn), lambda i, j: (i, j)),
        out_shape=jax.ShapeDtypeStruct((T, N), out_dtype),
        scratch_shapes=[pltpu.VMEM((tm, K), BF16)],
        compiler_params=_params("parallel", "arbitrary"),
        name="rms_matmul",
    )(x, g, w)


def _rope_kernel(pos_ref, invf_ref, cos_ref, sin_ref):
    ang = pos_ref[...].astype(F32) * invf_ref[...]
    lane = lax.broadcasted_iota(jnp.int32, ang.shape, 1)
    s = jnp.sin(ang)
    cos_ref[...] = jnp.cos(ang)
    sin_ref[...] = jnp.where(lane < RET_QK_DIM // 2, -s, s)


def _rope_tables(pos_col, tb):
    T = pos_col.shape[0]
    half = RET_QK_DIM // 2
    inv_freq = ROPE_BASE ** (-jnp.arange(0, RET_QK_DIM, 2, dtype=F32) / RET_QK_DIM)
    invf2 = jnp.concatenate([inv_freq, inv_freq]).reshape(1, 2 * half)
    return pl.pallas_call(
        _rope_kernel,
        grid=(T // tb,),
        in_specs=[pl.BlockSpec((tb, 1), lambda i: (i, 0)),
                  pl.BlockSpec((1, RET_QK_DIM), lambda i: (0, 0))],
        out_specs=[pl.BlockSpec((tb, RET_QK_DIM), lambda i: (i, 0))] * 2,
        out_shape=[jax.ShapeDtypeStruct((T, RET_QK_DIM), F32)] * 2,
        compiler_params=_params("parallel"),
        name="rope_tables",
    )(pos_col, invf2)


def _retention_kernel(q_ref, k_ref, v_ref, g_ref, cos_ref, sin_ref, dec_ref, kdec_ref, qdec_ref,
                      o_ref, s_ref, *, chunk_decay):
    C, dk, dv = RET_CHUNK, RET_QK_DIM, RET_V_DIM

    @pl.when(pl.program_id(0) == 0)
    def _():
        s_ref[...] = jnp.zeros_like(s_ref)

    kscale = dk ** -0.5
    for c in range(q_ref.shape[0] // C):
        rows = slice(c * C, (c + 1) * C)
        cos = cos_ref[rows, :]
        sin = sin_ref[rows, :]
        for h in range(RET_HEADS):
            q = q_ref[rows, h * dk:(h + 1) * dk].astype(F32)
            k = k_ref[rows, h * dk:(h + 1) * dk].astype(F32)
            v = v_ref[rows, h * dv:(h + 1) * dv].astype(BF16)
            qr = q * cos + pltpu.roll(q, dk // 2, 1) * sin
            kr = (k * cos + pltpu.roll(k, dk // 2, 1) * sin) * kscale
            scores = _dot_nt(qr.astype(BF16), kr.astype(BF16)) * dec_ref[h]
            inner = _dot(scores.astype(BF16), v)
            state = s_ref[h]
            cross = _dot((qr * qdec_ref[h]).astype(BF16), state.astype(BF16))
            kd_t = jnp.transpose(kr * kdec_ref[h]).astype(BF16)
            s_ref[h] = chunk_decay[h] * state + _dot(kd_t, v)
            o = inner + cross
            mu = jnp.mean(o, axis=-1, keepdims=True)
            oc = o - mu
            var = jnp.mean(oc * oc, axis=-1, keepdims=True)
            o = oc * lax.rsqrt(var + EPS)
            g = g_ref[rows, h * dv:(h + 1) * dv].astype(F32)
            o_ref[rows, h * dv:(h + 1) * dv] = (g * jax.nn.sigmoid(g) * o).astype(o_ref.dtype)


def _retention(z, cos2, sin2, tb):
    T = z.shape[0]
    H, C, dk, dv = RET_HEADS, RET_CHUNK, RET_QK_DIM, RET_V_DIM
    log_g = jnp.log(1.0 - 2.0 ** (-5.0 - jnp.arange(H, dtype=F32)))
    i = jnp.arange(C, dtype=F32)
    diff = i[:, None] - i[None, :]
    decay = jnp.where(diff[None] >= 0, jnp.exp(diff[None] * log_g[:, None, None]), 0.0)
    k_decay = jnp.exp((C - 1.0 - i)[None, :] * log_g[:, None])
    q_decay = jnp.exp((i + 1.0)[None, :] * log_g[:, None])
    kdec = jnp.broadcast_to(k_decay[:, :, None], (H, C, dk))
    qdec = jnp.broadcast_to(q_decay[:, :, None], (H, C, dk))
    log_g_np = np.log(1.0 - 2.0 ** (-5.0 - np.arange(H, dtype=np.float32))).astype(np.float32)
    chunk_decay = tuple(float(np.exp(np.float32(C) * lg)) for lg in log_g_np)
    nq = H * dk
    const3 = lambda i: (0, 0, 0)
    return pl.pallas_call(
        functools.partial(_retention_kernel, chunk_decay=chunk_decay),
        grid=(T // tb,),
        in_specs=[pl.BlockSpec((tb, nq), lambda i: (i, 0)),
                  pl.BlockSpec((tb, nq), lambda i: (i, 1)),
                  pl.BlockSpec((tb, H * dv), lambda i: (i, 1)),
                  pl.BlockSpec((tb, H * dv), lambda i: (i, 2)),
                  pl.BlockSpec((tb, dk), lambda i: (i, 0)),
                  pl.BlockSpec((tb, dk), lambda i: (i, 0)),
                  pl.BlockSpec((H, C, C), const3),
                  pl.BlockSpec((H, C, dk), const3),
                  pl.BlockSpec((H, C, dk), const3)],
        out_specs=pl.BlockSpec((tb, H * dv), lambda i: (i, 0)),
        out_shape=jax.ShapeDtypeStruct((T, H * dv), BF16),
        scratch_shapes=[pltpu.VMEM((H, dk, dv), F32)],
        compiler_params=_params("arbitrary"),
        name="retention",
    )(z, z, z, z, cos2, sin2, decay, kdec, qdec)


def _rglru_kernel(x_ref, y_ref, pos_ref, cw_ref, cb_ref, wa_ref, ba_ref, wx_ref, bx_ref, lam_ref,
                  o_ref, xpad_ref, a_ref, b_ref, hc_ref):
    tb = x_ref.shape[0]
    pad = tb // 2
    halo = SUBLANES

    @pl.when(pl.program_id(0) == 0)
    def _():
        xpad_ref[0:halo, :] = jnp.zeros((halo, D_MODEL), F32)
        hc_ref[...] = jnp.zeros_like(hc_ref)
        a_ref[0:pad, :] = jnp.ones((pad, D_MODEL), F32)
        b_ref[0:pad, :] = jnp.zeros((pad, D_MODEL), F32)

    xpad_ref[halo:halo + tb, :] = x_ref[...].astype(F32)
    xc = cb_ref[...] + cw_ref[3:4, :] * xpad_ref[halo:halo + tb, :]
    for w in range(CONV_WIDTH - 1):
        off = halo - (CONV_WIDTH - 1) + w
        xc = xc + cw_ref[w:w + 1, :] * xpad_ref[off:off + tb, :]
    xpad_ref[0:halo, :] = xpad_ref[tb:tb + halo, :]

    reset = pos_ref[...] == 0
    lam = lam_ref[...]
    nl = -lam
    sp = jnp.maximum(nl, 0.0) + jnp.log(1.0 + jnp.exp(-jnp.abs(nl)))
    bd = LRU_BLOCK_DIM
    for n in range(LRU_BLOCKS):
        cols = slice(n * bd, (n + 1) * bd)
        xcn = xc[:, cols]
        xb = xcn.astype(BF16)
        gate_r = jax.nn.sigmoid(_dot(xb, wa_ref[n]) + ba_ref[:, cols])
        gate_i = jax.nn.sigmoid(_dot(xb, wx_ref[n]) + bx_ref[:, cols])
        log_a = (-LRU_C) * gate_r * sp[:, cols]
        a = jnp.exp(log_a)
        mult = jnp.sqrt(jnp.tanh(-log_a) * (a * a + 1.0))
        a = jnp.where(reset, 0.0, a)
        mult = jnp.where(reset, 1.0, mult)
        b = xcn * gate_i * mult
        s = 1
        while s < tb:
            a_ref[pad:pad + tb, cols] = a
            b_ref[pad:pad + tb, cols] = b
            a_sh = a_ref[pad - s:pad - s + tb, cols]
            b_sh = b_ref[pad - s:pad - s + tb, cols]
            b = a * b_sh + b
            a = a * a_sh
            s *= 2
        h = a * hc_ref[0:1, cols] + b
        hc_ref[:, cols] = jnp.broadcast_to(h[tb - 1:tb, :], (SUBLANES, bd))
        o_ref[:, cols] = (h * _gelu(y_ref[:, cols].astype(F32))).astype(o_ref.dtype)


def _rglru(z, pos_col, conv_w, conv_b, wa, ba, wx, bx, lam, tb):
    T = z.shape[0]
    W = D_MODEL
    row = lambda i: (0, 0)
    const3 = lambda i: (0, 0, 0)
    return pl.pallas_call(
        _rglru_kernel,
        grid=(T // tb,),
        in_specs=[pl.BlockSpec((tb, W), lambda i: (i, 3)),
                  pl.BlockSpec((tb, W), lambda i: (i, 4)),
                  pl.BlockSpec((tb, 1), lambda i: (i, 0)),
                  pl.BlockSpec((CONV_WIDTH, W), row),
                  pl.BlockSpec((1, W), row),
                  pl.BlockSpec((LRU_BLOCKS, LRU_BLOCK_DIM, LRU_BLOCK_DIM), const3),
                  pl.BlockSpec((1, W), row),
                  pl.BlockSpec((LRU_BLOCKS, LRU_BLOCK_DIM, LRU_BLOCK_DIM), const3),
                  pl.BlockSpec((1, W), row),
                  pl.BlockSpec((1, W), row)],
        out_specs=pl.BlockSpec((tb, W), lambda i: (i, 0)),
        out_shape=jax.ShapeDtypeStruct((T, W), BF16),
        scratch_shapes=[pltpu.VMEM((tb + 2 * SUBLANES, W), F32),
                        pltpu.VMEM((tb + tb // 2, W), F32),
                        pltpu.VMEM((tb + tb // 2, W), F32),
                        pltpu.VMEM((SUBLANES, W), F32)],
        compiler_params=_params("arbitrary"),
        name="rglru",
    )(z, z, pos_col, conv_w, conv_b, wa, ba, wx, bx, lam)


def _merge_kernel(x_ref, ret_ref, rnn_ref, ga_ref, gb_ref, wr_ref, wn_ref, wo_ref, o_ref, w_ref):
    @pl.when(pl.program_id(0) == 0)
    def _():
        for k, src in enumerate((wr_ref, wn_ref, wo_ref)):
            w_ref[k] = src[...].astype(BF16)

    p_ret = _dot(ret_ref[...], w_ref[0])
    p_rnn = _dot(rnn_ref[...], w_ref[1])
    merged = (jax.nn.sigmoid(ga_ref[...].astype(F32)) * p_ret
              + jax.nn.sigmoid(gb_ref[...].astype(F32)) * p_rnn)
    o_ref[...] = x_ref[...] + _dot(merged.astype(BF16), w_ref[2])


def _merge(x, ret, rnn, z, w_ret, w_rnn, w_out, layer, tm):
    T, D = x.shape
    tile = lambda i: (i, 0)
    wspec = pl.BlockSpec((None, D, D), lambda i: (layer, 0, 0))
    return pl.pallas_call(
        _merge_kernel,
        grid=(T // tm,),
        in_specs=[pl.BlockSpec((tm, D), tile), pl.BlockSpec((tm, D), tile), pl.BlockSpec((tm, D), tile),
                  pl.BlockSpec((tm, D), lambda i: (i, 5)),
                  pl.BlockSpec((tm, D), lambda i: (i, 6)),
                  wspec, wspec, wspec],
        out_specs=pl.BlockSpec((tm, D), tile),
        out_shape=jax.ShapeDtypeStruct((T, D), F32),
        scratch_shapes=[pltpu.VMEM((3, D, D), BF16)],
        compiler_params=_params("arbitrary"),
        name="merge",
    )(x, ret, rnn, z, z, w_ret, w_rnn, w_out)


def _xattn_kernel(x_ref, g_ref, wq_ref, kt_ref, v_ref, wo_ref, o_ref, att_ref, w_ref):
    @pl.when(pl.program_id(0) == 0)
    def _():
        w_ref[0] = wq_ref[...].astype(BF16)
        w_ref[1] = wo_ref[...].astype(BF16)

    x = x_ref[...]
    q = _dot(_rms(x, g_ref[...]).astype(BF16), w_ref[0])
    dh = XATTN_HEAD_DIM
    for h in range(XATTN_HEADS):
        cols = slice(h * dh, (h + 1) * dh)
        s = _dot(q[:, cols].astype(BF16), kt_ref[cols, :]) * (dh ** -0.5)
        s = s - jnp.max(s, axis=-1, keepdims=True)
        e = jnp.exp(s)
        p = e / jnp.sum(e, axis=-1, keepdims=True)
        att_ref[:, cols] = _dot(p.astype(BF16), v_ref[:, cols]).astype(BF16)
    o_ref[...] = x + _dot(att_ref[...], w_ref[1])


def _xattn(x, g, wq, kt, v, wo, layer, tm):
    T, D = x.shape
    M = v.shape[0]
    tile = lambda i: (i, 0)
    fixed = lambda i: (0, 0)
    wspec = pl.BlockSpec((None, D, D), lambda i: (layer, 0, 0))
    return pl.pallas_call(
        _xattn_kernel,
        grid=(T // tm,),
        in_specs=[pl.BlockSpec((tm, D), tile), pl.BlockSpec((1, D), fixed), wspec,
                  pl.BlockSpec((D, M), fixed), pl.BlockSpec((M, D), fixed), wspec],
        out_specs=pl.BlockSpec((tm, D), tile),
        out_shape=jax.ShapeDtypeStruct((T, D), F32),
        scratch_shapes=[pltpu.VMEM((tm, D), BF16), pltpu.VMEM((2, D, D), BF16)],
        compiler_params=_params("arbitrary"),
        name="xattn",
    )(x, g, wq, kt, v, wo)


def _peer_scores_kernel(x_ref, g_ref, wq_ref, k1_ref, k2_ref, hft_ref, s1_ref, s2_ref):
    hf = _rms(x_ref[...], g_ref[...])
    hft = jnp.transpose(hf).astype(BF16)
    hft_ref[...] = hft
    qt = _dot(wq_ref[...], hft)
    kd = 2 * PEER_HALF
    for h in range(PEER_HEADS):
        q1 = qt[h * kd:h * kd + PEER_HALF, :].astype(BF16)
        q2 = qt[h * kd + PEER_HALF:(h + 1) * kd, :].astype(BF16)
        s1_ref[h] = _dot(k1_ref[h], q1)
        s2_ref[h] = _dot(k2_ref[h], q2)


def _peer_scores(x, g, wq_t, k1, k2, tm):
    T, D = x.shape
    H, NK = PEER_HEADS, PEER_KEYS
    fixed = lambda i: (0, 0)
    const3 = lambda i: (0, 0, 0)
    return pl.pallas_call(
        _peer_scores_kernel,
        grid=(T // tm,),
        in_specs=[pl.BlockSpec((tm, D), lambda i: (i, 0)), pl.BlockSpec((1, D), fixed),
                  pl.BlockSpec(wq_t.shape, fixed),
                  pl.BlockSpec((H, NK, PEER_HALF), const3), pl.BlockSpec((H, NK, PEER_HALF), const3)],
        out_specs=[pl.BlockSpec((D, tm), lambda i: (0, i)),
                   pl.BlockSpec((H, NK, tm), lambda i: (0, 0, i)),
                   pl.BlockSpec((H, NK, tm), lambda i: (0, 0, i))],
        out_shape=[jax.ShapeDtypeStruct((D, T), BF16),
                   jax.ShapeDtypeStruct((H, NK, T), F32),
                   jax.ShapeDtypeStruct((H, NK, T), F32)],
        compiler_params=_params("parallel"),
        name="peer_scores",
    )(x, g, wq_t, k1, k2)


def _sort_pairs(n):
    pairs = []
    p = 1
    while p < n:
        k = p
        while k >= 1:
            for j in range(k % p, n - k, 2 * k):
                for i in range(min(k, n - j - k)):
                    if (i + j) // (2 * p) == (i + j + k) // (2 * p):
                        pairs.append((i + j, i + j + k))
            k //= 2
        p *= 2
    return pairs


def _sort_desc(vs):
    vs = list(vs)
    for lo, hi in _sort_pairs(len(vs)):
        a, b = vs[lo], vs[hi]
        vs[lo], vs[hi] = jnp.maximum(a, b), jnp.minimum(a, b)
    return vs


def _merge_top(a, b):
    n = len(a)
    c = [jnp.maximum(a[i], b[n - 1 - i]) for i in range(n)]
    k = n // 2
    while k >= 1:
        for i in range(n):
            if i & k == 0:
                x, y = c[i], c[i + k]
                c[i], c[i + k] = jnp.maximum(x, y), jnp.minimum(x, y)
        k //= 2
    return c


def _top_of_keys(groups, roll):
    vs = _sort_desc(groups)
    shift = SUBLANES // 2
    while shift >= 1:
        vs = _merge_top(vs, [roll(v, shift, 0) for v in vs])
        shift //= 2
    return vs


def _peer_select_math(s1, s2, roll):
    K = PEER_TOPK
    shape = s1[0][0].shape
    sub = lax.broadcasted_iota(jnp.int32, shape, 0)
    zero = jnp.zeros(shape, F32)
    tops2 = []
    v1 = [zero] * K
    v2 = [zero] * K
    for h in range(PEER_HEADS):
        t1 = _top_of_keys(s1[h], roll)
        t2 = _top_of_keys(s2[h], roll)
        tops2.append(t2)
        v1 = [jnp.where(sub == h, t, v) for t, v in zip(t1, v1)]
        v2 = [jnp.where(sub == h, t, v) for t, v in zip(t2, v2)]
    cands = [v1[i] + v2[j] for i in range(K) for j in range(K // (i + 1))]
    n = 1
    while n < len(cands):
        n *= 2
    cands = cands + [jnp.full(shape, -jnp.inf, F32)] * (n - len(cands))
    top = _sort_desc(cands)[:K]
    z = zero
    for t in top:
        z = z + jnp.exp(t - top[0])
    tau_all, zinv_all = top[K - 1], 1.0 / z
    out = []
    for h in range(PEER_HEADS):
        bc = lambda v: jnp.broadcast_to(v[h:h + 1, :], shape)
        tau, zinv, m1, m2 = bc(tau_all), bc(zinv_all), bc(v1[0]), bc(v2[0])
        t2 = tops2[h]
        cnt, p1, rk, e2 = [], [], [], []
        for s1g, s2g in zip(s1[h], s2[h]):
            c = zero
            r = zero
            for j in range(K):
                c = jnp.where(s1g + t2[j] >= tau, float(j + 1), c)
                r = jnp.where(t2[j] > s2g, float(j + 1), r)
            cnt.append(c)
            rk.append(r)
            p1.append(jnp.exp(s1g - m1) * zinv)
            e2.append(jnp.exp(s2g - m2))
        out.append((cnt, p1, rk, e2))
    return out


def _bf16_pair(x):
    u = lax.bitcast_convert_type(x.astype(BF16).astype(F32), jnp.uint32)
    return u | (u >> 16)


def _peer_select_kernel(s1_ref, s2_ref, n_ref, p1_ref, r2_ref, e2_ref):
    ng = PEER_KEYS // SUBLANES
    grp = lambda ref, h: [ref[h, i * SUBLANES:(i + 1) * SUBLANES, :] for i in range(ng)]
    s1 = [grp(s1_ref, h) for h in range(PEER_HEADS)]
    s2 = [grp(s2_ref, h) for h in range(PEER_HEADS)]
    res = _peer_select_math(s1, s2, pltpu.roll)
    pk = 2 * SUBLANES
    for h, (cnt, p1, rk, e2) in enumerate(res):
        for i in range(ng):
            rows = slice(i * SUBLANES, (i + 1) * SUBLANES)
            n_ref[h, rows, :] = _bf16_pair(cnt[i])
            p1_ref[h, rows, :] = _bf16_pair(p1[i])
        for i in range(ng // 2):
            rows = slice(i * pk, (i + 1) * pk)
            r2_ref[h, rows, :] = jnp.concatenate([rk[2 * i], rk[2 * i + 1]], axis=0).astype(BF16)
            e2_ref[h, rows, :] = jnp.concatenate([e2[2 * i], e2[2 * i + 1]], axis=0).astype(BF16)


def _peer_select(s1, s2, tl):
    H, NK, T = s1.shape
    spec = pl.BlockSpec((H, NK, tl), lambda i: (0, 0, i))
    return pl.pallas_call(
        _peer_select_kernel,
        grid=(T // tl,),
        in_specs=[spec] * 2,
        out_specs=[spec] * 4,
        out_shape=[jax.ShapeDtypeStruct((H, NK, T), jnp.uint32), jax.ShapeDtypeStruct((H, NK, T), jnp.uint32),
                   jax.ShapeDtypeStruct((H, NK, T), BF16), jax.ShapeDtypeStruct((H, NK, T), BF16)],
        compiler_params=_params("parallel"),
        name="peer_select",
    )(s1, s2)


def _peer_gate_rows(act, a, n_ref, p1_ref, r2_ref, e2_ref, w_ref, row0):
    NK = PEER_KEYS
    tm = act.shape[-1]
    pk = 2 * SUBLANES
    zero = jnp.zeros((pk, tm), BF16)
    bc = lambda ref, h: pltpu.bitcast(jnp.broadcast_to(ref[h, pl.ds(a, 1), :], (SUBLANES, tm)), BF16)
    nrow = [bc(n_ref, h) for h in range(PEER_HEADS)]
    prow = [bc(p1_ref, h) for h in range(PEER_HEADS)]
    for g in range(NK // pk):
        b_rows = slice(g * pk, (g + 1) * pk)
        gate = zero
        for h in range(PEER_HEADS):
            sel = r2_ref[h, b_rows, :] < nrow[h]
            gate = gate + jnp.where(sel, prow[h] * e2_ref[h, b_rows, :], zero)
        w_ref[row0 + g * pk:row0 + (g + 1) * pk, :] = gate * act[b_rows, :]


def _peer_half(a0, hft, keys, u_ref, w_ref, vt_prev_ref, w_prev_ref, acc_ref, gl_ref, na):
    NK = PEER_KEYS
    dr = acc_ref.shape[0] // na
    w_prev = w_prev_ref[...]
    for ai in range(na):
        gl_ref[ai] = _gelu(_dot(u_ref[ai * NK:(ai + 1) * NK, :], hft).astype(BF16))
        if ai % 2 == 0:
            rows = slice(ai * dr, (ai + 2) * dr)
            acc_ref[rows, :] += _dot(vt_prev_ref[rows, :], w_prev)
        _peer_gate_rows(gl_ref.at[ai], a0 + ai, *keys, w_ref, ai * NK)


def _peer_dense_kernel(x_ref, hft_ref, n_ref, p1_ref, r2_ref, e2_ref, ua_ref, ub_ref, vta_ref, vtb_ref,
                       *rest, na, norm_out):
    g_ref = rest[0] if norm_out else None
    o_ref, acc_ref, wa_ref, wb_ref, gla_ref, glb_ref = rest[-6:]
    j = pl.program_id(1)
    last = pl.num_programs(1) - 1
    keys = (n_ref, p1_ref, r2_ref, e2_ref)

    @pl.when(j == 0)
    def _():
        acc_ref[...] = jnp.zeros_like(acc_ref)
        wb_ref[...] = jnp.zeros_like(wb_ref)

    @pl.when(j < last)
    def _():
        hft = hft_ref[...]
        _peer_half(2 * j * na, hft, keys, ua_ref, wa_ref, vtb_ref, wb_ref, acc_ref, gla_ref, na)
        _peer_half((2 * j + 1) * na, hft, keys, ub_ref, wb_ref, vta_ref, wa_ref, acc_ref, glb_ref, na)

    @pl.when(j == last)
    def _():
        acc = acc_ref[...] + _dot(vtb_ref[...], wb_ref[...])
        xo = x_ref[...] + jnp.transpose(acc)
        o_ref[...] = _rms(xo, g_ref[...]) if norm_out else xo


def _peer_dense(x, hft, n1, p1, r2, e2, u, vt, tm, na, g_out=None):
    T, D = x.shape
    H, NK = PEER_HEADS, PEER_KEYS
    th = na * NK
    steps = NK // (2 * na)
    kspec = pl.BlockSpec((H, NK, tm), lambda i, j: (0, 0, i))
    cur = lambda j: jnp.minimum(j, steps - 1)
    return pl.pallas_call(
        functools.partial(_peer_dense_kernel, na=na, norm_out=g_out is not None),
        grid=(T // tm, steps + 1),
        in_specs=[pl.BlockSpec((tm, D), lambda i, j: (i, 0)),
                  pl.BlockSpec((D, tm), lambda i, j: (0, i)),
                  kspec, kspec, kspec, kspec,
                  pl.BlockSpec((th, D), lambda i, j: (2 * cur(j), 0)),
                  pl.BlockSpec((th, D), lambda i, j: (2 * cur(j) + 1, 0)),
                  pl.BlockSpec((D, th), lambda i, j: (0, 2 * cur(j))),
                  pl.BlockSpec((D, th), lambda i, j: (0, jnp.maximum(2 * j - 1, 1)))]
                 + ([pl.BlockSpec((1, D), lambda i, j: (0, 0))] if g_out is not None else []),
        out_specs=pl.BlockSpec((tm, D), lambda i, j: (i, 0)),
        out_shape=jax.ShapeDtypeStruct((T, D), F32),
        scratch_shapes=[pltpu.VMEM((D, tm), F32), pltpu.VMEM((th, tm), BF16), pltpu.VMEM((th, tm), BF16),
                        pltpu.VMEM((na, NK, tm), BF16), pltpu.VMEM((na, NK, tm), BF16)],
        compiler_params=_params("parallel", "arbitrary"),
        name="peer_dense",
    )(x, hft, n1, p1, r2, e2, u, u, vt, vt, *(() if g_out is None else (g_out,)))


def kernel(x, mem, positions, g_mix, w_in, w_ret_br, w_rnn_br, w_mix_out, conv_w, conv_b, lru_wa, lru_ba, lru_wx, lru_bx, lru_lam, g_x, g_mem, w_xq, w_xk, w_xv, w_xo, g_ffn, w_pq, sub_k1, sub_k2, peer_u, peer_v, g_final):
    B, T, D = x.shape
    assert B == 1 and D == D_MODEL
    depth = g_mix.shape[0]
    tm = min(TM_PROJ, T)
    xs = x.reshape(T, D)
    mems = mem.reshape(mem.shape[1], D)
    pos_col = positions.reshape(T, 1)
    row = lambda a: a.reshape(1, -1)

    cos2, sin2 = _rope_tables(pos_col, tm)
    for l in range(depth):
        z = _rms_matmul(xs, row(g_mix[l]), w_in, l, tm=min(TM_IN, T), tn=TN_PROJ, out_dtype=BF16)
        ret = _retention(z, cos2, sin2, min(TB_RET, T))
        rnn = _rglru(z, pos_col, conv_w[l], row(conv_b[l]), lru_wa[l].astype(BF16), row(lru_ba[l]),
                     lru_wx[l].astype(BF16), row(lru_bx[l]), row(lru_lam[l]), min(TB_LRU, T))
        xs = _merge(xs, ret, rnn, z, w_ret_br, w_rnn_br, w_mix_out, l, tm)
        kmem = _rms_matmul(mems, row(g_mem[l]), w_xk, l, tm=mems.shape[0], tn=D, out_dtype=BF16)
        vmem = _rms_matmul(mems, row(g_mem[l]), w_xv, l, tm=mems.shape[0], tn=D, out_dtype=BF16)
        xs = _xattn(xs, row(g_x[l]), w_xq, kmem.T, vmem, w_xo, l, tm)
        tp = min(TM_PEER, T)
        hft, s1, s2 = _peer_scores(xs, row(g_ffn[l]), w_pq[l].T.astype(BF16), sub_k1[l].astype(BF16),
                                   sub_k2[l].astype(BF16), tp)
        n1, p1, r2, e2 = _peer_select(s1, s2, TL_SORT)
        xs = _peer_dense(xs, hft, n1, p1, r2, e2, peer_u[l].astype(BF16), peer_v[l].T.astype(BF16),
                         tp, NA_PEER, g_out=row(g_final) if l == depth - 1 else None)
    return xs.reshape(B, T, D)
```

```python
import functools
import math

import numpy as np
import jax
import jax.numpy as jnp
from jax import lax
from jax.experimental import pallas as pl
from jax.experimental.pallas import tpu as pltpu

F32 = jnp.float32
BF16 = jnp.bfloat16

D_MODEL = 1024
EPS = 1e-6
RET_HEADS = 4
RET_QK_DIM = 128
RET_V_DIM = 256
RET_CHUNK = 128
ROPE_BASE = 10000.0
LRU_BLOCKS = 8
LRU_BLOCK_DIM = D_MODEL // LRU_BLOCKS
CONV_WIDTH = 4
LRU_C = 8.0
XATTN_HEADS = 4
XATTN_HEAD_DIM = D_MODEL // XATTN_HEADS
PEER_HEADS = 8
PEER_KEYS = 128
PEER_TOPK = 16
PEER_HALF = 128

LANES = 128
SUBLANES = 8
VMEM_LIMIT_BYTES = 56 * 1024 * 1024

TM_PROJ = 512
TM_IN = 2048
TN_PROJ = 1024
TB_RET = 256
TB_LRU = 256
TM_PEER = 512
NA_PEER = 4
TL_SORT = 256


def _params(*sem):
    return pltpu.CompilerParams(dimension_semantics=sem, vmem_limit_bytes=VMEM_LIMIT_BYTES)


def _rms(x, g):
    return x * lax.rsqrt(jnp.mean(x * x, axis=-1, keepdims=True) + EPS) * g


def _gelu(x):
    c = math.sqrt(2.0 / math.pi)
    return 0.5 * x * (1.0 + jnp.tanh(c * (x + 0.044715 * (x * x * x))))


def _dot(a, b):
    return jnp.dot(a, b, preferred_element_type=F32)


def _dot_nt(a, b):
    return lax.dot_general(a, b, (((1,), (1,)), ((), ())), preferred_element_type=F32)


def _rms_matmul_kernel(x_ref, g_ref, w_ref, o_ref, h_ref):
    @pl.when(pl.program_id(1) == 0)
    def _():
        h_ref[...] = _rms(x_ref[...], g_ref[...]).astype(BF16)

    o_ref[...] = _dot(h_ref[...], w_ref[...].astype(BF16)).astype(o_ref.dtype)


def _rms_matmul(x, g, w, layer, *, tm, tn, out_dtype):
    T, K = x.shape
    N = w.shape[2]
    return pl.pallas_call(
        _rms_matmul_kernel,
        grid=(T // tm, N // tn),
        in_specs=[pl.BlockSpec((tm, K), lambda i, j: (i, 0)),
                  pl.BlockSpec((1, K), lambda i, j: (0, 0)),
                  pl.BlockSpec((None, K, tn), lambda i, j: (layer, 0, j))],
        out_specs=pl.BlockSpec((tm, tn), lambda i, j: (i, j)),
        out_shape=jax.ShapeDtypeStruct((T, N), out_dtype),
        scratch_shapes=[pltpu.VMEM((tm, K), BF16)],
        compiler_params=_params("parallel", "arbitrary"),
        name="rms_matmul",
    )(x, g, w)


def _rope_kernel(pos_ref, invf_ref, cos_ref, sin_ref):
    ang = pos_ref[...].astype(F32) * invf_ref[...]
    lane = lax.broadcasted_iota(jnp.int32, ang.shape, 1)
    s = jnp.sin(ang)
    cos_ref[...] = jnp.cos(ang)
    sin_ref[...] = jnp.where(lane < RET_QK_DIM // 2, -s, s)


def _rope_tables(pos_col, tb):
    T = pos_col.shape[0]
    half = RET_QK_DIM // 2
    inv_freq = ROPE_BASE ** (-jnp.arange(0, RET_QK_DIM, 2, dtype=F32) / RET_QK_DIM)
    invf2 = jnp.concatenate([inv_freq, inv_freq]).reshape(1, 2 * half)
    return pl.pallas_call(
        _rope_kernel,
        grid=(T // tb,),
        in_specs=[pl.BlockSpec((tb, 1), lambda i: (i, 0)),
                  pl.BlockSpec((1, RET_QK_DIM), lambda i: (0, 0))],
        out_specs=[pl.BlockSpec((tb, RET_QK_DIM), lambda i: (i, 0))] * 2,
        out_shape=[jax.ShapeDtypeStruct((T, RET_QK_DIM), F32)] * 2,
        compiler_params=_params("parallel"),
        name="rope_tables",
    )(pos_col, invf2)


def _retention_kernel(q_ref, k_ref, v_ref, g_ref, cos_ref, sin_ref, dec_ref, kdec_ref, qdec_ref,
                      o_ref, s_ref, *, chunk_decay):
    C, dk, dv = RET_CHUNK, RET_QK_DIM, RET_V_DIM

    @pl.when(pl.program_id(0) == 0)
    def _():
        s_ref[...] = jnp.zeros_like(s_ref)

    kscale = dk ** -0.5
    for c in range(q_ref.shape[0] // C):
        rows = slice(c * C, (c + 1) * C)
        cos = cos_ref[rows, :]
        sin = sin_ref[rows, :]
        for h in range(RET_HEADS):
            q = q_ref[rows, h * dk:(h + 1) * dk].astype(F32)
            k = k_ref[rows, h * dk:(h + 1) * dk].astype(F32)
            v = v_ref[rows, h * dv:(h + 1) * dv].astype(BF16)
            qr = q * cos + pltpu.roll(q, dk // 2, 1) * sin
            kr = (k * cos + pltpu.roll(k, dk // 2, 1) * sin) * kscale
            scores = _dot_nt(qr.astype(BF16), kr.astype(BF16)) * dec_ref[h]
            inner = _dot(scores.astype(BF16), v)
            state = s_ref[h]
            cross = _dot((qr * qdec_ref[h]).astype(BF16), state.astype(BF16))
            kd_t = jnp.transpose(kr * kdec_ref[h]).astype(BF16)
            s_ref[h] = chunk_decay[h] * state + _dot(kd_t, v)
            o = inner + cross
            mu = jnp.mean(o, axis=-1, keepdims=True)
            oc = o - mu
            var = jnp.mean(oc * oc, axis=-1, keepdims=True)
            o = oc * lax.rsqrt(var + EPS)
            g = g_ref[rows, h * dv:(h + 1) * dv].astype(F32)
            o_ref[rows, h * dv:(h + 1) * dv] = (g * jax.nn.sigmoid(g) * o).astype(o_ref.dtype)


def _retention(z, cos2, sin2, tb):
    T = z.shape[0]
    H, C, dk, dv = RET_HEADS, RET_CHUNK, RET_QK_DIM, RET_V_DIM
    log_g = jnp.log(1.0 - 2.0 ** (-5.0 - jnp.arange(H, dtype=F32)))
    i = jnp.arange(C, dtype=F32)
    diff = i[:, None] - i[None, :]
    decay = jnp.where(diff[None] >= 0, jnp.exp(diff[None] * log_g[:, None, None]), 0.0)
    k_decay = jnp.exp((C - 1.0 - i)[None, :] * log_g[:, None])
    q_decay = jnp.exp((i + 1.0)[None, :] * log_g[:, None])
    kdec = jnp.broadcast_to(k_decay[:, :, None], (H, C, dk))
    qdec = jnp.broadcast_to(q_decay[:, :, None], (H, C, dk))
    log_g_np = np.log(1.0 - 2.0 ** (-5.0 - np.arange(H, dtype=np.float32))).astype(np.float32)
    chunk_decay = tuple(float(np.exp(np.float32(C) * lg)) for lg in log_g_np)
    nq = H * dk
    const3 = lambda i: (0, 0, 0)
    return pl.pallas_call(
        functools.partial(_retention_kernel, chunk_decay=chunk_decay),
        grid=(T // tb,),
        in_specs=[pl.BlockSpec((tb, nq), lambda i: (i, 0)),
                  pl.BlockSpec((tb, nq), lambda i: (i, 1)),
                  pl.BlockSpec((tb, H * dv), lambda i: (i, 1)),
                  pl.BlockSpec((tb, H * dv), lambda i: (i, 2)),
                  pl.BlockSpec((tb, dk), lambda i: (i, 0)),
                  pl.BlockSpec((tb, dk), lambda i: (i, 0)),
                  pl.BlockSpec((H, C, C), const3),
                  pl.BlockSpec((H, C, dk), const3),
                  pl.BlockSpec((H, C, dk), const3)],
        out_specs=pl.BlockSpec((tb, H * dv), lambda i: (i, 0)),
        out_shape=jax.ShapeDtypeStruct((T, H * dv), BF16),
        scratch_shapes=[pltpu.VMEM((H, dk, dv), F32)],
        compiler_params=_params("arbitrary"),
        name="retention",
    )(z, z, z, z, cos2, sin2, decay, kdec, qdec)


def _rglru_kernel(x_ref, y_ref, pos_ref, cw_ref, cb_ref, wa_ref, ba_ref, wx_ref, bx_ref, lam_ref,
                  o_ref, xpad_ref, a_ref, b_ref, hc_ref):
    tb = x_ref.shape[0]
    pad = tb // 2
    halo = SUBLANES

    @pl.when(pl.program_id(0) == 0)
    def _():
        xpad_ref[0:halo, :] = jnp.zeros((halo, D_MODEL), F32)
        hc_ref[...] = jnp.zeros_like(hc_ref)
        a_ref[0:pad, :] = jnp.ones((pad, D_MODEL), F32)
        b_ref[0:pad, :] = jnp.zeros((pad, D_MODEL), F32)

    xpad_ref[halo:halo + tb, :] = x_ref[...].astype(F32)
    xc = cb_ref[...] + cw_ref[3:4, :] * xpad_ref[halo:halo + tb, :]
    for w in range(CONV_WIDTH - 1):
        off = halo - (CONV_WIDTH - 1) + w
        xc = xc + cw_ref[w:w + 1, :] * xpad_ref[off:off + tb, :]
    xpad_ref[0:halo, :] = xpad_ref[tb:tb + halo, :]

    reset = pos_ref[...] == 0
    lam = lam_ref[...]
    nl = -lam
    sp = jnp.maximum(nl, 0.0) + jnp.log(1.0 + jnp.exp(-jnp.abs(nl)))
    bd = LRU_BLOCK_DIM
    for n in range(LRU_BLOCKS):
        cols = slice(n * bd, (n + 1) * bd)
        xcn = xc[:, cols]
        xb = xcn.astype(BF16)
        gate_r = jax.nn.sigmoid(_dot(xb, wa_ref[n]) + ba_ref[:, cols])
        gate_i = jax.nn.sigmoid(_dot(xb, wx_ref[n]) + bx_ref[:, cols])
        log_a = (-LRU_C) * gate_r * sp[:, cols]
        a = jnp.exp(log_a)
        mult = jnp.sqrt(jnp.tanh(-log_a) * (a * a + 1.0))
        a = jnp.where(reset, 0.0, a)
        mult = jnp.where(reset, 1.0, mult)
        b = xcn * gate_i * mult
        s = 1
        while s < tb:
            a_ref[pad:pad + tb, cols] = a
            b_ref[pad:pad + tb, cols] = b
            a_sh = a_ref[pad - s:pad - s + tb, cols]
            b_sh = b_ref[pad - s:pad - s + tb, cols]
            b = a * b_sh + b
            a = a * a_sh
            s *= 2
        h = a * hc_ref[0:1, cols] + b
        hc_ref[:, cols] = jnp.broadcast_to(h[tb - 1:tb, :], (SUBLANES, bd))
        o_ref[:, cols] = (h * _gelu(y_ref[:, cols].astype(F32))).astype(o_ref.dtype)


def _rglru(z, pos_col, conv_w, conv_b, wa, ba, wx, bx, lam, tb):
    T = z.shape[0]
    W = D_MODEL
    row = lambda i: (0, 0)
    const3 = lambda i: (0, 0, 0)
    return pl.pallas_call(
        _rglru_kernel,
        grid=(T // tb,),
        in_specs=[pl.BlockSpec((tb, W), lambda i: (i, 3)),
                  pl.BlockSpec((tb, W), lambda i: (i, 4)),
                  pl.BlockSpec((tb, 1), lambda i: (i, 0)),
                  pl.BlockSpec((CONV_WIDTH, W), row),
                  pl.BlockSpec((1, W), row),
                  pl.BlockSpec((LRU_BLOCKS, LRU_BLOCK_DIM, LRU_BLOCK_DIM), const3),
                  pl.BlockSpec((1, W), row),
                  pl.BlockSpec((LRU_BLOCKS, LRU_BLOCK_DIM, LRU_BLOCK_DIM), const3),
                  pl.BlockSpec((1, W), row),
                  pl.BlockSpec((1, W), row)],
        out_specs=pl.BlockSpec((tb, W), lambda i: (i, 0)),
        out_shape=jax.ShapeDtypeStruct((T, W), BF16),
        scratch_shapes=[pltpu.VMEM((tb + 2 * SUBLANES, W), F32),
                        pltpu.VMEM((tb + tb // 2, W), F32),
                        pltpu.VMEM((tb + tb // 2, W), F32),
                        pltpu.VMEM((SUBLANES, W), F32)],
        compiler_params=_params("arbitrary"),
        name="rglru",
    )(z, z, pos_col, conv_w, conv_b, wa, ba, wx, bx, lam)


def _merge_kernel(x_ref, ret_ref, rnn_ref, ga_ref, gb_ref, wr_ref, wn_ref, wo_ref, o_ref, w_ref):
    @pl.when(pl.program_id(0) == 0)
    def _():
        for k, src in enumerate((wr_ref, wn_ref, wo_ref)):
            w_ref[k] = src[...].astype(BF16)

    p_ret = _dot(ret_ref[...], w_ref[0])
    p_rnn = _dot(rnn_ref[...], w_ref[1])
    merged = (jax.nn.sigmoid(ga_ref[...].astype(F32)) * p_ret
              + jax.nn.sigmoid(gb_ref[...].astype(F32)) * p_rnn)
    o_ref[...] = x_ref[...] + _dot(merged.astype(BF16), w_ref[2])


def _merge(x, ret, rnn, z, w_ret, w_rnn, w_out, layer, tm):
    T, D = x.shape
    tile = lambda i: (i, 0)
    wspec = pl.BlockSpec((None, D, D), lambda i: (layer, 0, 0))
    return pl.pallas_call(
        _merge_kernel,
        grid=(T // tm,),
        in_specs=[pl.BlockSpec((tm, D), tile), pl.BlockSpec((tm, D), tile), pl.BlockSpec((tm, D), tile),
                  pl.BlockSpec((tm, D), lambda i: (i, 5)),
                  pl.BlockSpec((tm, D), lambda i: (i, 6)),
                  wspec, wspec, wspec],
        out_specs=pl.BlockSpec((tm, D), tile),
        out_shape=jax.ShapeDtypeStruct((T, D), F32),
        scratch_shapes=[pltpu.VMEM((3, D, D), BF16)],
        compiler_params=_params("arbitrary"),
        name="merge",
    )(x, ret, rnn, z, z, w_ret, w_rnn, w_out)


def _xattn_kernel(x_ref, g_ref, wq_ref, kt_ref, v_ref, wo_ref, o_ref, att_ref, w_ref):
    @pl.when(pl.program_id(0) == 0)
    def _():
        w_ref[0] = wq_ref[...].astype(BF16)
        w_ref[1] = wo_ref[...].astype(BF16)

    x = x_ref[...]
    q = _dot(_rms(x, g_ref[...]).astype(BF16), w_ref[0])
    dh = XATTN_HEAD_DIM
    for h in range(XATTN_HEADS):
        cols = slice(h * dh, (h + 1) * dh)
        s = _dot(q[:, cols].astype(BF16), kt_ref[cols, :]) * (dh ** -0.5)
        s = s - jnp.max(s, axis=-1, keepdims=True)
        e = jnp.exp(s)
        p = e / jnp.sum(e, axis=-1, keepdims=True)
        att_ref[:, cols] = _dot(p.astype(BF16), v_ref[:, cols]).astype(BF16)
    o_ref[...] = x + _dot(att_ref[...], w_ref[1])


def _xattn(x, g, wq, kt, v, wo, layer, tm):
    T, D = x.shape
    M = v.shape[0]
    tile = lambda i: (i, 0)
    fixed = lambda i: (0, 0)
    wspec = pl.BlockSpec((None, D, D), lambda i: (layer, 0, 0))
    return pl.pallas_call(
        _xattn_kernel,
        grid=(T // tm,),
        in_specs=[pl.BlockSpec((tm, D), tile), pl.BlockSpec((1, D), fixed), wspec,
                  pl.BlockSpec((D, M), fixed), pl.BlockSpec((M, D), fixed), wspec],
        out_specs=pl.BlockSpec((tm, D), tile),
        out_shape=jax.ShapeDtypeStruct((T, D), F32),
        scratch_shapes=[pltpu.VMEM((tm, D), BF16), pltpu.VMEM((2, D, D), BF16)],
        compiler_params=_params("arbitrary"),
        name="xattn",
    )(x, g, wq, kt, v, wo)


def _peer_scores_kernel(x_ref, g_ref, wq_ref, k1_ref, k2_ref, hft_ref, s1_ref, s2_ref):
    hf = _rms(x_ref[...], g_ref[...])
    hft = jnp.transpose(hf).astype(BF16)
    hft_ref[...] = hft
    qt = _dot(wq_ref[...], hft)
    kd = 2 * PEER_HALF
    for h in range(PEER_HEADS):
        q1 = qt[h * kd:h * kd + PEER_HALF, :].astype(BF16)
        q2 = qt[h * kd + PEER_HALF:(h + 1) * kd, :].astype(BF16)
        s1_ref[h] = _dot(k1_ref[h], q1)
        s2_ref[h] = _dot(k2_ref[h], q2)


def _peer_scores(x, g, wq_t, k1, k2, tm):
    T, D = x.shape
    H, NK = PEER_HEADS, PEER_KEYS
    fixed = lambda i: (0, 0)
    const3 = lambda i: (0, 0, 0)
    return pl.pallas_call(
        _peer_scores_kernel,
        grid=(T // tm,),
        in_specs=[pl.BlockSpec((tm, D), lambda i: (i, 0)), pl.BlockSpec((1, D), fixed),
                  pl.BlockSpec(wq_t.shape, fixed),
                  pl.BlockSpec((H, NK, PEER_HALF), const3), pl.BlockSpec((H, NK, PEER_HALF), const3)],
        out_specs=[pl.BlockSpec((D, tm), lambda i: (0, i)),
                   pl.BlockSpec((H, NK, tm), lambda i: (0, 0, i)),
                   pl.BlockSpec((H, NK, tm), lambda i: (0, 0, i))],
        out_shape=[jax.ShapeDtypeStruct((D, T), BF16),
                   jax.ShapeDtypeStruct((H, NK, T), F32),
                   jax.ShapeDtypeStruct((H, NK, T), F32)],
        compiler_params=_params("parallel"),
        name="peer_scores",
    )(x, g, wq_t, k1, k2)


def _sort_pairs(n):
    pairs = []
    p = 1
    while p < n:
        k = p
        while k >= 1:
            for j in range(k % p, n - k, 2 * k):
                for i in range(min(k, n - j - k)):
                    if (i + j) // (2 * p) == (i + j + k) // (2 * p):
                        pairs.append((i + j, i + j + k))
            k //= 2
        p *= 2
    return pairs


def _sort_desc(vs):
    vs = list(vs)
    for lo, hi in _sort_pairs(len(vs)):
        a, b = vs[lo], vs[hi]
        vs[lo], vs[hi] = jnp.maximum(a, b), jnp.minimum(a, b)
    return vs


def _merge_top(a, b):
    n = len(a)
    c = [jnp.maximum(a[i], b[n - 1 - i]) for i in range(n)]
    k = n // 2
    while k >= 1:
        for i in range(n):
            if i & k == 0:
                x, y = c[i], c[i + k]
                c[i], c[i + k] = jnp.maximum(x, y), jnp.minimum(x, y)
        k //= 2
    return c


def _top_of_keys(groups, roll):
    vs = _sort_desc(groups)
    shift = SUBLANES // 2
    while shift >= 1:
        vs = _merge_top(vs, [roll(v, shift, 0) for v in vs])
        shift //= 2
    return vs


def _peer_select_math(s1, s2, roll):
    K = PEER_TOPK
    shape = s1[0][0].shape
    sub = lax.broadcasted_iota(jnp.int32, shape, 0)
    zero = jnp.zeros(shape, F32)
    tops2 = []
    v1 = [zero] * K
    v2 = [zero] * K
    for h in range(PEER_HEADS):
        t1 = _top_of_keys(s1[h], roll)
        t2 = _top_of_keys(s2[h], roll)
        tops2.append(t2)
        v1 = [jnp.where(sub == h, t, v) for t, v in zip(t1, v1)]
        v2 = [jnp.where(sub == h, t, v) for t, v in zip(t2, v2)]
    cands = [v1[i] + v2[j] for i in range(K) for j in range(K // (i + 1))]
    n = 1
    while n < len(cands):
        n *= 2
    cands = cands + [jnp.full(shape, -jnp.inf, F32)] * (n - len(cands))
    top = _sort_desc(cands)[:K]
    z = zero
    for t in top:
        z = z + jnp.exp(t - top[0])
    tau_all, zinv_all = top[K - 1], 1.0 / z
    out = []
    for h in range(PEER_HEADS):
        bc = lambda v: jnp.broadcast_to(v[h:h + 1, :], shape)
        tau, zinv, m1, m2 = bc(tau_all), bc(zinv_all), bc(v1[0]), bc(v2[0])
        t2 = tops2[h]
        cnt, p1, rk, e2 = [], [], [], []
        for s1g, s2g in zip(s1[h], s2[h]):
            c = zero
            r = zero
            for j in range(K):
                c = jnp.where(s1g + t2[j] >= tau, float(j + 1), c)
                r = jnp.where(t2[j] > s2g, float(j + 1), r)
            cnt.append(c)
            rk.append(r)
            p1.append(jnp.exp(s1g - m1) * zinv)
            e2.append(jnp.exp(s2g - m2))
        out.append((cnt, p1, rk, e2))
    return out


def _bf16_pair(x):
    u = lax.bitcast_convert_type(x.astype(BF16).astype(F32), jnp.uint32)
    return u | (u >> 16)


def _peer_select_kernel(s1_ref, s2_ref, n_ref, p1_ref, r2_ref, e2_ref):
    ng = PEER_KEYS // SUBLANES
    grp = lambda ref, h: [ref[h, i * SUBLANES:(i + 1) * SUBLANES, :] for i in range(ng)]
    s1 = [grp(s1_ref, h) for h in range(PEER_HEADS)]
    s2 = [grp(s2_ref, h) for h in range(PEER_HEADS)]
    res = _peer_select_math(s1, s2, pltpu.roll)
    pk = 2 * SUBLANES
    for h, (cnt, p1, rk, e2) in enumerate(res):
        for i in range(ng):
            rows = slice(i * SUBLANES, (i + 1) * SUBLANES)
            n_ref[h, rows, :] = _bf16_pair(cnt[i])
            p1_ref[h, rows, :] = _bf16_pair(p1[i])
        for i in range(ng // 2):
            rows = slice(i * pk, (i + 1) * pk)
            r2_ref[h, rows, :] = jnp.concatenate([rk[2 * i], rk[2 * i + 1]], axis=0).astype(BF16)
            e2_ref[h, rows, :] = jnp.concatenate([e2[2 * i], e2[2 * i + 1]], axis=0).astype(BF16)


def _peer_select(s1, s2, tl):
    H, NK, T = s1.shape
    spec = pl.BlockSpec((H, NK, tl), lambda i: (0, 0, i))
    return pl.pallas_call(
        _peer_select_kernel,
        grid=(T // tl,),
        in_specs=[spec] * 2,
        out_specs=[spec] * 4,
        out_shape=[jax.ShapeDtypeStruct((H, NK, T), jnp.uint32), jax.ShapeDtypeStruct((H, NK, T), jnp.uint32),
                   jax.ShapeDtypeStruct((H, NK, T), BF16), jax.ShapeDtypeStruct((H, NK, T), BF16)],
        compiler_params=_params("parallel"),
        name="peer_select",
    )(s1, s2)


def _peer_gate_rows(act, a, n_ref, p1_ref, r2_ref, e2_ref, w_ref, row0):
    NK = PEER_KEYS
    tm = act.shape[-1]
    pk = 2 * SUBLANES
    zero = jnp.zeros((pk, tm), BF16)
    bc = lambda ref, h: pltpu.bitcast(jnp.broadcast_to(ref[h, pl.ds(a, 1), :], (SUBLANES, tm)), BF16)
    nrow = [bc(n_ref, h) for h in range(PEER_HEADS)]
    prow = [bc(p1_ref, h) for h in range(PEER_HEADS)]
    for g in range(NK // pk):
        b_rows = slice(g * pk, (g + 1) * pk)
        gate = zero
        for h in range(PEER_HEADS):
            sel = r2_ref[h, b_rows, :] < nrow[h]
            gate = gate + jnp.where(sel, prow[h] * e2_ref[h, b_rows, :], zero)
        w_ref[row0 + g * pk:row0 + (g + 1) * pk, :] = gate * act[b_rows, :]


def _peer_half(a0, hft, keys, u_ref, w_ref, vt_prev_ref, w_prev_ref, acc_ref, gl_ref, na):
    NK = PEER_KEYS
    dr = acc_ref.shape[0] // na
    w_prev = w_prev_ref[...]
    for ai in range(na):
        gl_ref[ai] = _gelu(_dot(u_ref[ai * NK:(ai + 1) * NK, :], hft).astype(BF16))
        if ai % 2 == 0:
            rows = slice(ai * dr, (ai + 2) * dr)
            acc_ref[rows, :] += _dot(vt_prev_ref[rows, :], w_prev)
        _peer_gate_rows(gl_ref.at[ai], a0 + ai, *keys, w_ref, ai * NK)


def _peer_dense_kernel(x_ref, hft_ref, n_ref, p1_ref, r2_ref, e2_ref, ua_ref, ub_ref, vta_ref, vtb_ref,
                       *rest, na, norm_out):
    g_ref = rest[0] if norm_out else None
    o_ref, acc_ref, wa_ref, wb_ref, gla_ref, glb_ref = rest[-6:]
    j = pl.program_id(1)
    last = pl.num_programs(1) - 1
    keys = (n_ref, p1_ref, r2_ref, e2_ref)

    @pl.when(j == 0)
    def _():
        acc_ref[...] = jnp.zeros_like(acc_ref)
        wb_ref[...] = jnp.zeros_like(wb_ref)

    @pl.when(j < last)
    def _():
        hft = hft_ref[...]
        _peer_half(2 * j * na, hft, keys, ua_ref, wa_ref, vtb_ref, wb_ref, acc_ref, gla_ref, na)
        _peer_half((2 * j + 1) * na, hft, keys, ub_ref, wb_ref, vta_ref, wa_ref, acc_ref, glb_ref, na)

    @pl.when(j == last)
    def _():
        acc = acc_ref[...] + _dot(vtb_ref[...], wb_ref[...])
        xo = x_ref[...] + jnp.transpose(acc)
        o_ref[...] = _rms(xo, g_ref[...]) if norm_out else xo


def _peer_dense(x, hft, n1, p1, r2, e2, u, vt, layer, tm, na, g_out=None):
    T, D = x.shape
    H, NK = PEER_HEADS, PEER_KEYS
    th = na * NK
    steps = NK // (2 * na)
    kspec = pl.BlockSpec((H, NK, tm), lambda i, j: (0, 0, i))
    cur = lambda j: jnp.minimum(j, steps - 1)
    return pl.pallas_call(
        functools.partial(_peer_dense_kernel, na=na, norm_out=g_out is not None),
        grid=(T // tm, steps + 1),
        in_specs=[pl.BlockSpec((tm, D), lambda i, j: (i, 0)),
                  pl.BlockSpec((D, tm), lambda i, j: (0, i)),
                  kspec, kspec, kspec, kspec,
                  pl.BlockSpec((None, th, D), lambda i, j: (layer, 2 * cur(j), 0)),
                  pl.BlockSpec((None, th, D), lambda i, j: (layer, 2 * cur(j) + 1, 0)),
                  pl.BlockSpec((None, D, th), lambda i, j: (layer, 0, 2 * cur(j))),
                  pl.BlockSpec((None, D, th), lambda i, j: (layer, 0, jnp.maximum(2 * j - 1, 1)))]
                 + ([pl.BlockSpec((1, D), lambda i, j: (0, 0))] if g_out is not None else []),
        out_specs=pl.BlockSpec((tm, D), lambda i, j: (i, 0)),
        out_shape=jax.ShapeDtypeStruct((T, D), F32),
        scratch_shapes=[pltpu.VMEM((D, tm), F32), pltpu.VMEM((th, tm), BF16), pltpu.VMEM((th, tm), BF16),
                        pltpu.VMEM((na, NK, tm), BF16), pltpu.VMEM((na, NK, tm), BF16)],
        compiler_params=_params("parallel", "arbitrary"),
        name="peer_dense",
    )(x, hft, n1, p1, r2, e2, u, u, vt, vt, *(() if g_out is None else (g_out,)))


def kernel(x, mem, positions, g_mix, w_in, w_ret_br, w_rnn_br, w_mix_out, conv_w, conv_b, lru_wa, lru_ba, lru_wx, lru_bx, lru_lam, g_x, g_mem, w_xq, w_xk, w_xv, w_xo, g_ffn, w_pq, sub_k1, sub_k2, peer_u, peer_v, g_final):
    B, T, D = x.shape
    assert B == 1 and D == D_MODEL
    depth = g_mix.shape[0]
    tm = min(TM_PROJ, T)
    xs = x.reshape(T, D)
    mems = mem.reshape(mem.shape[1], D)
    pos_col = positions.reshape(T, 1)
    row = lambda a: a.reshape(1, -1)

    cos2, sin2 = _rope_tables(pos_col, tm)
    u_all = peer_u.astype(BF16)
    vt_all = jnp.swapaxes(peer_v, 1, 2).astype(BF16)
    for l in range(depth):
        z = _rms_matmul(xs, row(g_mix[l]), w_in, l, tm=min(TM_IN, T), tn=TN_PROJ, out_dtype=BF16)
        ret = _retention(z, cos2, sin2, min(TB_RET, T))
        rnn = _rglru(z, pos_col, conv_w[l], row(conv_b[l]), lru_wa[l].astype(BF16), row(lru_ba[l]),
                     lru_wx[l].astype(BF16), row(lru_bx[l]), row(lru_lam[l]), min(TB_LRU, T))
        xs = _merge(xs, ret, rnn, z, w_ret_br, w_rnn_br, w_mix_out, l, tm)
        kmem = _rms_matmul(mems, row(g_mem[l]), w_xk, l, tm=mems.shape[0], tn=D, out_dtype=BF16)
        vmem = _rms_matmul(mems, row(g_mem[l]), w_xv, l, tm=mems.shape[0], tn=D, out_dtype=BF16)
        xs = _xattn(xs, row(g_x[l]), w_xq, kmem.T, vmem, w_xo, l, tm)
        tp = min(TM_PEER, T)
        hft, s1, s2 = _peer_scores(xs, row(g_ffn[l]), w_pq[l].T.astype(BF16), sub_k1[l].astype(BF16),
                                   sub_k2[l].astype(BF16), tp)
        n1, p1, r2, e2 = _peer_select(s1, s2, TL_SORT)
        xs = _peer_dense(xs, hft, n1, p1, r2, e2, u_all, vt_all, l, tp, NA_PEER,
                         g_out=row(g_final) if l == depth - 1 else None)
    return xs.reshape(B, T, D)
```

```python
import functools
import math

import numpy as np
import jax
import jax.numpy as jnp
from jax import lax
from jax.experimental import pallas as pl
from jax.experimental.pallas import tpu as pltpu

F32 = jnp.float32
BF16 = jnp.bfloat16

D_MODEL = 1024
EPS = 1e-6
RET_HEADS = 4
RET_QK_DIM = 128
RET_V_DIM = 256
RET_CHUNK = 128
ROPE_BASE = 10000.0
LRU_BLOCKS = 8
LRU_BLOCK_DIM = D_MODEL // LRU_BLOCKS
CONV_WIDTH = 4
LRU_C = 8.0
XATTN_HEADS = 4
XATTN_HEAD_DIM = D_MODEL // XATTN_HEADS
PEER_HEADS = 8
PEER_KEYS = 128
PEER_TOPK = 16
PEER_HALF = 128

LANES = 128
SUBLANES = 8
VMEM_LIMIT_BYTES = 56 * 1024 * 1024

TM_PROJ = 512
TM_IN = 2048
TN_PROJ = 1024
TB_RET = 256
TB_LRU = 256
TM_PEER = 512
NA_PEER = 4
TL_SORT = 256


def _params(*sem):
    return pltpu.CompilerParams(dimension_semantics=sem, vmem_limit_bytes=VMEM_LIMIT_BYTES)


def _rms(x, g):
    return x * lax.rsqrt(jnp.mean(x * x, axis=-1, keepdims=True) + EPS) * g


def _gelu(x):
    c = math.sqrt(2.0 / math.pi)
    return 0.5 * x * (1.0 + jnp.tanh(c * (x + 0.044715 * (x * x * x))))


def _dot(a, b):
    return jnp.dot(a, b, preferred_element_type=F32)


def _dot_nt(a, b):
    return lax.dot_general(a, b, (((1,), (1,)), ((), ())), preferred_element_type=F32)


def _rms_matmul_kernel(x_ref, g_ref, w_ref, o_ref, h_ref):
    @pl.when(pl.program_id(1) == 0)
    def _():
        h_ref[...] = _rms(x_ref[...], g_ref[...]).astype(BF16)

    o_ref[...] = _dot(h_ref[...], w_ref[...].astype(BF16)).astype(o_ref.dtype)


def _rms_matmul(x, g, w, layer, *, tm, tn, out_dtype):
    T, K = x.shape
    N = w.shape[2]
    return pl.pallas_call(
        _rms_matmul_kernel,
        grid=(T // tm, N // tn),
        in_specs=[pl.BlockSpec((tm, K), lambda i, j: (i, 0)),
                  pl.BlockSpec((1, K), lambda i, j: (0, 0)),
                  pl.BlockSpec((None, K, tn), lambda i, j: (layer, 0, j))],
        out_specs=pl.BlockSpec((tm, tn), lambda i, j: (i, j)),
        out_shape=jax.ShapeDtypeStruct((T, N), out_dtype),
        scratch_shapes=[pltpu.VMEM((tm, K), BF16)],
        compiler_params=_params("parallel", "arbitrary"),
        name="rms_matmul",
    )(x, g, w)


def _rope_kernel(pos_ref, invf_ref, cos_ref, sin_ref):
    ang = pos_ref[...].astype(F32) * invf_ref[...]
    lane = lax.broadcasted_iota(jnp.int32, ang.shape, 1)
    s = jnp.sin(ang)
    cos_ref[...] = jnp.cos(ang)
    sin_ref[...] = jnp.where(lane < RET_QK_DIM // 2, -s, s)


def _rope_tables(pos_col, tb):
    T = pos_col.shape[0]
    half = RET_QK_DIM // 2
    inv_freq = ROPE_BASE ** (-jnp.arange(0, RET_QK_DIM, 2, dtype=F32) / RET_QK_DIM)
    invf2 = jnp.concatenate([inv_freq, inv_freq]).reshape(1, 2 * half)
    return pl.pallas_call(
        _rope_kernel,
        grid=(T // tb,),
        in_specs=[pl.BlockSpec((tb, 1), lambda i: (i, 0)),
                  pl.BlockSpec((1, RET_QK_DIM), lambda i: (0, 0))],
        out_specs=[pl.BlockSpec((tb, RET_QK_DIM), lambda i: (i, 0))] * 2,
        out_shape=[jax.ShapeDtypeStruct((T, RET_QK_DIM), F32)] * 2,
        compiler_params=_params("parallel"),
        name="rope_tables",
    )(pos_col, invf2)


def _retention_kernel(q_ref, k_ref, v_ref, g_ref, cos_ref, sin_ref, dec_ref, kdec_ref, qdec_ref,
                      o_ref, s_ref, *, chunk_decay):
    C, dk, dv = RET_CHUNK, RET_QK_DIM, RET_V_DIM

    @pl.when(pl.program_id(0) == 0)
    def _():
        s_ref[...] = jnp.zeros_like(s_ref)

    kscale = dk ** -0.5
    for c in range(q_ref.shape[0] // C):
        rows = slice(c * C, (c + 1) * C)
        cos = cos_ref[rows, :]
        sin = sin_ref[rows, :]
        for h in range(RET_HEADS):
            q = q_ref[rows, h * dk:(h + 1) * dk].astype(F32)
            k = k_ref[rows, h * dk:(h + 1) * dk].astype(F32)
            v = v_ref[rows, h * dv:(h + 1) * dv].astype(BF16)
            qr = q * cos + pltpu.roll(q, dk // 2, 1) * sin
            kr = (k * cos + pltpu.roll(k, dk // 2, 1) * sin) * kscale
            scores = _dot_nt(qr.astype(BF16), kr.astype(BF16)) * dec_ref[h]
            inner = _dot(scores.astype(BF16), v)
            state = s_ref[h]
            cross = _dot((qr * qdec_ref[h]).astype(BF16), state.astype(BF16))
            kd_t = jnp.transpose(kr * kdec_ref[h]).astype(BF16)
            s_ref[h] = chunk_decay[h] * state + _dot(kd_t, v)
            o = inner + cross
            mu = jnp.mean(o, axis=-1, keepdims=True)
            oc = o - mu
            var = jnp.mean(oc * oc, axis=-1, keepdims=True)
            o = oc * lax.rsqrt(var + EPS)
            g = g_ref[rows, h * dv:(h + 1) * dv].astype(F32)
            o_ref[rows, h * dv:(h + 1) * dv] = (g * jax.nn.sigmoid(g) * o).astype(o_ref.dtype)


def _retention(z, cos2, sin2, tb):
    T = z.shape[0]
    H, C, dk, dv = RET_HEADS, RET_CHUNK, RET_QK_DIM, RET_V_DIM
    log_g = jnp.log(1.0 - 2.0 ** (-5.0 - jnp.arange(H, dtype=F32)))
    i = jnp.arange(C, dtype=F32)
    diff = i[:, None] - i[None, :]
    decay = jnp.where(diff[None] >= 0, jnp.exp(diff[None] * log_g[:, None, None]), 0.0)
    k_decay = jnp.exp((C - 1.0 - i)[None, :] * log_g[:, None])
    q_decay = jnp.exp((i + 1.0)[None, :] * log_g[:, None])
    kdec = jnp.broadcast_to(k_decay[:, :, None], (H, C, dk))
    qdec = jnp.broadcast_to(q_decay[:, :, None], (H, C, dk))
    log_g_np = np.log(1.0 - 2.0 ** (-5.0 - np.arange(H, dtype=np.float32))).astype(np.float32)
    chunk_decay = tuple(float(np.exp(np.float32(C) * lg)) for lg in log_g_np)
    nq = H * dk
    const3 = lambda i: (0, 0, 0)
    return pl.pallas_call(
        functools.partial(_retention_kernel, chunk_decay=chunk_decay),
        grid=(T // tb,),
        in_specs=[pl.BlockSpec((tb, nq), lambda i: (i, 0)),
                  pl.BlockSpec((tb, nq), lambda i: (i, 1)),
                  pl.BlockSpec((tb, H * dv), lambda i: (i, 1)),
                  pl.BlockSpec((tb, H * dv), lambda i: (i, 2)),
                  pl.BlockSpec((tb, dk), lambda i: (i, 0)),
                  pl.BlockSpec((tb, dk), lambda i: (i, 0)),
                  pl.BlockSpec((H, C, C), const3),
                  pl.BlockSpec((H, C, dk), const3),
                  pl.BlockSpec((H, C, dk), const3)],
        out_specs=pl.BlockSpec((tb, H * dv), lambda i: (i, 0)),
        out_shape=jax.ShapeDtypeStruct((T, H * dv), BF16),
        scratch_shapes=[pltpu.VMEM((H, dk, dv), F32)],
        compiler_params=_params("arbitrary"),
        name="retention",
    )(z, z, z, z, cos2, sin2, decay, kdec, qdec)


def _rglru_kernel(x_ref, y_ref, pos_ref, cw_ref, cb_ref, wa_ref, ba_ref, wx_ref, bx_ref, lam_ref,
                  o_ref, xpad_ref, a_ref, b_ref, hc_ref):
    tb = x_ref.shape[0]
    pad = tb // 2
    halo = SUBLANES

    @pl.when(pl.program_id(0) == 0)
    def _():
        xpad_ref[0:halo, :] = jnp.zeros((halo, D_MODEL), F32)
        hc_ref[...] = jnp.zeros_like(hc_ref)
        a_ref[0:pad, :] = jnp.ones((pad, D_MODEL), F32)
        b_ref[0:pad, :] = jnp.zeros((pad, D_MODEL), F32)

    xpad_ref[halo:halo + tb, :] = x_ref[...].astype(F32)
    xc = cb_ref[...] + cw_ref[3:4, :] * xpad_ref[halo:halo + tb, :]
    for w in range(CONV_WIDTH - 1):
        off = halo - (CONV_WIDTH - 1) + w
        xc = xc + cw_ref[w:w + 1, :] * xpad_ref[off:off + tb, :]
    xpad_ref[0:halo, :] = xpad_ref[tb:tb + halo, :]

    reset = pos_ref[...] == 0
    lam = lam_ref[...]
    nl = -lam
    sp = jnp.maximum(nl, 0.0) + jnp.log(1.0 + jnp.exp(-jnp.abs(nl)))
    bd = LRU_BLOCK_DIM
    for n in range(LRU_BLOCKS):
        cols = slice(n * bd, (n + 1) * bd)
        xcn = xc[:, cols]
        xb = xcn.astype(BF16)
        gate_r = jax.nn.sigmoid(_dot(xb, wa_ref[n]) + ba_ref[:, cols])
        gate_i = jax.nn.sigmoid(_dot(xb, wx_ref[n]) + bx_ref[:, cols])
        log_a = (-LRU_C) * gate_r * sp[:, cols]
        a = jnp.exp(log_a)
        mult = jnp.sqrt(jnp.tanh(-log_a) * (a * a + 1.0))
        a = jnp.where(reset, 0.0, a)
        mult = jnp.where(reset, 1.0, mult)
        b = xcn * gate_i * mult
        s = 1
        while s < tb:
            a_ref[pad:pad + tb, cols] = a
            b_ref[pad:pad + tb, cols] = b
            a_sh = a_ref[pad - s:pad - s + tb, cols]
            b_sh = b_ref[pad - s:pad - s + tb, cols]
            b = a * b_sh + b
            a = a * a_sh
            s *= 2
        h = a * hc_ref[0:1, cols] + b
        hc_ref[:, cols] = jnp.broadcast_to(h[tb - 1:tb, :], (SUBLANES, bd))
        o_ref[:, cols] = (h * _gelu(y_ref[:, cols].astype(F32))).astype(o_ref.dtype)


def _rglru(z, pos_col, conv_w, conv_b, wa, ba, wx, bx, lam, tb):
    T = z.shape[0]
    W = D_MODEL
    row = lambda i: (0, 0)
    const3 = lambda i: (0, 0, 0)
    return pl.pallas_call(
        _rglru_kernel,
        grid=(T // tb,),
        in_specs=[pl.BlockSpec((tb, W), lambda i: (i, 3)),
                  pl.BlockSpec((tb, W), lambda i: (i, 4)),
                  pl.BlockSpec((tb, 1), lambda i: (i, 0)),
                  pl.BlockSpec((CONV_WIDTH, W), row),
                  pl.BlockSpec((1, W), row),
                  pl.BlockSpec((LRU_BLOCKS, LRU_BLOCK_DIM, LRU_BLOCK_DIM), const3),
                  pl.BlockSpec((1, W), row),
                  pl.BlockSpec((LRU_BLOCKS, LRU_BLOCK_DIM, LRU_BLOCK_DIM), const3),
                  pl.BlockSpec((1, W), row),
                  pl.BlockSpec((1, W), row)],
        out_specs=pl.BlockSpec((tb, W), lambda i: (i, 0)),
        out_shape=jax.ShapeDtypeStruct((T, W), BF16),
        scratch_shapes=[pltpu.VMEM((tb + 2 * SUBLANES, W), F32),
                        pltpu.VMEM((tb + tb // 2, W), F32),
                        pltpu.VMEM((tb + tb // 2, W), F32),
                        pltpu.VMEM((SUBLANES, W), F32)],
        compiler_params=_params("arbitrary"),
        name="rglru",
    )(z, z, pos_col, conv_w, conv_b, wa, ba, wx, bx, lam)


def _merge_kernel(x_ref, ret_ref, rnn_ref, ga_ref, gb_ref, wr_ref, wn_ref, wo_ref, o_ref, w_ref):
    @pl.when(pl.program_id(0) == 0)
    def _():
        for k, src in enumerate((wr_ref, wn_ref, wo_ref)):
            w_ref[k] = src[...].astype(BF16)

    p_ret = _dot(ret_ref[...], w_ref[0])
    p_rnn = _dot(rnn_ref[...], w_ref[1])
    merged = (jax.nn.sigmoid(ga_ref[...].astype(F32)) * p_ret
              + jax.nn.sigmoid(gb_ref[...].astype(F32)) * p_rnn)
    o_ref[...] = x_ref[...] + _dot(merged.astype(BF16), w_ref[2])


def _merge(x, ret, rnn, z, w_ret, w_rnn, w_out, layer, tm):
    T, D = x.shape
    tile = lambda i: (i, 0)
    wspec = pl.BlockSpec((None, D, D), lambda i: (layer, 0, 0))
    return pl.pallas_call(
        _merge_kernel,
        grid=(T // tm,),
        in_specs=[pl.BlockSpec((tm, D), tile), pl.BlockSpec((tm, D), tile), pl.BlockSpec((tm, D), tile),
                  pl.BlockSpec((tm, D), lambda i: (i, 5)),
                  pl.BlockSpec((tm, D), lambda i: (i, 6)),
                  wspec, wspec, wspec],
        out_specs=pl.BlockSpec((tm, D), tile),
        out_shape=jax.ShapeDtypeStruct((T, D), F32),
        scratch_shapes=[pltpu.VMEM((3, D, D), BF16)],
        compiler_params=_params("arbitrary"),
        name="merge",
    )(x, ret, rnn, z, z, w_ret, w_rnn, w_out)


def _xattn_kernel(x_ref, g_ref, wq_ref, kt_ref, v_ref, wo_ref, o_ref, att_ref, w_ref):
    @pl.when(pl.program_id(0) == 0)
    def _():
        w_ref[0] = wq_ref[...].astype(BF16)
        w_ref[1] = wo_ref[...].astype(BF16)

    x = x_ref[...]
    q = _dot(_rms(x, g_ref[...]).astype(BF16), w_ref[0])
    dh = XATTN_HEAD_DIM
    for h in range(XATTN_HEADS):
        cols = slice(h * dh, (h + 1) * dh)
        s = _dot(q[:, cols].astype(BF16), kt_ref[cols, :]) * (dh ** -0.5)
        s = s - jnp.max(s, axis=-1, keepdims=True)
        e = jnp.exp(s)
        p = e / jnp.sum(e, axis=-1, keepdims=True)
        att_ref[:, cols] = _dot(p.astype(BF16), v_ref[:, cols]).astype(BF16)
    o_ref[...] = x + _dot(att_ref[...], w_ref[1])


def _xattn(x, g, wq, kt, v, wo, layer, tm):
    T, D = x.shape
    M = v.shape[0]
    tile = lambda i: (i, 0)
    fixed = lambda i: (0, 0)
    wspec = pl.BlockSpec((None, D, D), lambda i: (layer, 0, 0))
    return pl.pallas_call(
        _xattn_kernel,
        grid=(T // tm,),
        in_specs=[pl.BlockSpec((tm, D), tile), pl.BlockSpec((1, D), fixed), wspec,
                  pl.BlockSpec((D, M), fixed), pl.BlockSpec((M, D), fixed), wspec],
        out_specs=pl.BlockSpec((tm, D), tile),
        out_shape=jax.ShapeDtypeStruct((T, D), F32),
        scratch_shapes=[pltpu.VMEM((tm, D), BF16), pltpu.VMEM((2, D, D), BF16)],
        compiler_params=_params("arbitrary"),
        name="xattn",
    )(x, g, wq, kt, v, wo)


def _peer_scores_kernel(x_ref, g_ref, wq_ref, k1_ref, k2_ref, hft_ref, s1_ref, s2_ref):
    hf = _rms(x_ref[...], g_ref[...])
    hft = jnp.transpose(hf).astype(BF16)
    hft_ref[...] = hft
    qt = _dot(wq_ref[...], hft)
    kd = 2 * PEER_HALF
    for h in range(PEER_HEADS):
        q1 = qt[h * kd:h * kd + PEER_HALF, :].astype(BF16)
        q2 = qt[h * kd + PEER_HALF:(h + 1) * kd, :].astype(BF16)
        s1_ref[h] = _dot(k1_ref[h], q1)
        s2_ref[h] = _dot(k2_ref[h], q2)


def _peer_scores(x, g, wq_t, k1, k2, tm):
    T, D = x.shape
    H, NK = PEER_HEADS, PEER_KEYS
    fixed = lambda i: (0, 0)
    const3 = lambda i: (0, 0, 0)
    return pl.pallas_call(
        _peer_scores_kernel,
        grid=(T // tm,),
        in_specs=[pl.BlockSpec((tm, D), lambda i: (i, 0)), pl.BlockSpec((1, D), fixed),
                  pl.BlockSpec(wq_t.shape, fixed),
                  pl.BlockSpec((H, NK, PEER_HALF), const3), pl.BlockSpec((H, NK, PEER_HALF), const3)],
        out_specs=[pl.BlockSpec((D, tm), lambda i: (0, i)),
                   pl.BlockSpec((H, NK, tm), lambda i: (0, 0, i)),
                   pl.BlockSpec((H, NK, tm), lambda i: (0, 0, i))],
        out_shape=[jax.ShapeDtypeStruct((D, T), BF16),
                   jax.ShapeDtypeStruct((H, NK, T), F32),
                   jax.ShapeDtypeStruct((H, NK, T), F32)],
        compiler_params=_params("parallel"),
        name="peer_scores",
    )(x, g, wq_t, k1, k2)


def _sort_pairs(n):
    pairs = []
    p = 1
    while p < n:
        k = p
        while k >= 1:
            for j in range(k % p, n - k, 2 * k):
                for i in range(min(k, n - j - k)):
                    if (i + j) // (2 * p) == (i + j + k) // (2 * p):
                        pairs.append((i + j, i + j + k))
            k //= 2
        p *= 2
    return pairs


def _sort_desc(vs):
    vs = list(vs)
    for lo, hi in _sort_pairs(len(vs)):
        a, b = vs[lo], vs[hi]
        vs[lo], vs[hi] = jnp.maximum(a, b), jnp.minimum(a, b)
    return vs


def _merge_top(a, b):
    n = len(a)
    c = [jnp.maximum(a[i], b[n - 1 - i]) for i in range(n)]
    k = n // 2
    while k >= 1:
        for i in range(n):
            if i & k == 0:
                x, y = c[i], c[i + k]
                c[i], c[i + k] = jnp.maximum(x, y), jnp.minimum(x, y)
        k //= 2
    return c


def _top_of_keys(groups, roll):
    vs = _sort_desc(groups)
    shift = SUBLANES // 2
    while shift >= 1:
        vs = _merge_top(vs, [roll(v, shift, 0) for v in vs])
        shift //= 2
    return vs


def _peer_select_math(s1, s2, roll):
    K = PEER_TOPK
    shape = s1[0][0].shape
    sub = lax.broadcasted_iota(jnp.int32, shape, 0)
    zero = jnp.zeros(shape, F32)
    tops2 = []
    v1 = [zero] * K
    v2 = [zero] * K
    for h in range(PEER_HEADS):
        t1 = _top_of_keys(s1[h], roll)
        t2 = _top_of_keys(s2[h], roll)
        tops2.append(t2)
        v1 = [jnp.where(sub == h, t, v) for t, v in zip(t1, v1)]
        v2 = [jnp.where(sub == h, t, v) for t, v in zip(t2, v2)]
    cands = [v1[i] + v2[j] for i in range(K) for j in range(K // (i + 1))]
    n = 1
    while n < len(cands):
        n *= 2
    cands = cands + [jnp.full(shape, -jnp.inf, F32)] * (n - len(cands))
    top = _sort_desc(cands)[:K]
    z = zero
    for t in top:
        z = z + jnp.exp(t - top[0])
    tau_all, zinv_all = top[K - 1], 1.0 / z
    out = []
    for h in range(PEER_HEADS):
        bc = lambda v: jnp.broadcast_to(v[h:h + 1, :], shape)
        tau, zinv, m1, m2 = bc(tau_all), bc(zinv_all), bc(v1[0]), bc(v2[0])
        t2 = tops2[h]
        cnt, p1, rk, e2 = [], [], [], []
        c_top = zero
        for j in range(K // 2, K):
            c_top = jnp.where(m1 + t2[j] >= tau, float(j + 1), c_top)
        for s1g, s2g in zip(s1[h], s2[h]):
            c = zero
            r = zero
            for j in range(K // 2):
                c = jnp.where(s1g + t2[j] >= tau, float(j + 1), c)
            c = jnp.where(s1g == m1, jnp.maximum(c, c_top), c)
            for j in range(K):
                r = jnp.where(t2[j] > s2g, float(j + 1), r)
            cnt.append(c)
            rk.append(r)
            p1.append(jnp.exp(s1g - m1) * zinv)
            e2.append(jnp.exp(s2g - m2))
        out.append((cnt, p1, rk, e2))
    return out


def _bf16_pair(x):
    u = lax.bitcast_convert_type(x.astype(BF16).astype(F32), jnp.uint32)
    return u | (u >> 16)


def _peer_select_kernel(s1_ref, s2_ref, n_ref, p1_ref, r2_ref, e2_ref):
    ng = PEER_KEYS // SUBLANES
    grp = lambda ref, h: [ref[h, i * SUBLANES:(i + 1) * SUBLANES, :] for i in range(ng)]
    s1 = [grp(s1_ref, h) for h in range(PEER_HEADS)]
    s2 = [grp(s2_ref, h) for h in range(PEER_HEADS)]
    res = _peer_select_math(s1, s2, pltpu.roll)
    pk = 2 * SUBLANES
    for h, (cnt, p1, rk, e2) in enumerate(res):
        for i in range(ng):
            rows = slice(i * SUBLANES, (i + 1) * SUBLANES)
            n_ref[h, rows, :] = _bf16_pair(cnt[i])
            p1_ref[h, rows, :] = _bf16_pair(p1[i])
        for i in range(ng // 2):
            rows = slice(i * pk, (i + 1) * pk)
            r2_ref[h, rows, :] = jnp.concatenate([rk[2 * i], rk[2 * i + 1]], axis=0).astype(BF16)
            e2_ref[h, rows, :] = jnp.concatenate([e2[2 * i], e2[2 * i + 1]], axis=0).astype(BF16)


def _peer_select(s1, s2, tl):
    H, NK, T = s1.shape
    spec = pl.BlockSpec((H, NK, tl), lambda i: (0, 0, i))
    return pl.pallas_call(
        _peer_select_kernel,
        grid=(T // tl,),
        in_specs=[spec] * 2,
        out_specs=[spec] * 4,
        out_shape=[jax.ShapeDtypeStruct((H, NK, T), jnp.uint32), jax.ShapeDtypeStruct((H, NK, T), jnp.uint32),
                   jax.ShapeDtypeStruct((H, NK, T), BF16), jax.ShapeDtypeStruct((H, NK, T), BF16)],
        compiler_params=_params("parallel"),
        name="peer_select",
    )(s1, s2)


def _peer_gate_rows(act, a, n_ref, p1_ref, r2_ref, e2_ref, w_ref, row0):
    NK = PEER_KEYS
    tm = act.shape[-1]
    pk = 2 * SUBLANES
    zero = jnp.zeros((pk, tm), BF16)
    bc = lambda ref, h: pltpu.bitcast(jnp.broadcast_to(ref[h, pl.ds(a, 1), :], (SUBLANES, tm)), BF16)
    nrow = [bc(n_ref, h) for h in range(PEER_HEADS)]
    prow = [bc(p1_ref, h) for h in range(PEER_HEADS)]
    for g in range(NK // pk):
        b_rows = slice(g * pk, (g + 1) * pk)
        gate = zero
        for h in range(PEER_HEADS):
            sel = r2_ref[h, b_rows, :] < nrow[h]
            gate = gate + jnp.where(sel, prow[h] * e2_ref[h, b_rows, :], zero)
        w_ref[row0 + g * pk:row0 + (g + 1) * pk, :] = gate * act[b_rows, :]


def _peer_half(a0, hft, keys, u_ref, w_ref, vt_prev_ref, w_prev_ref, acc_ref, gl_ref, na):
    NK = PEER_KEYS
    dr = acc_ref.shape[0] // na
    w_prev = w_prev_ref[...]
    for ai in range(na):
        gl_ref[ai] = _gelu(_dot(u_ref[ai * NK:(ai + 1) * NK, :], hft).astype(BF16))
        if ai % 2 == 0:
            rows = slice(ai * dr, (ai + 2) * dr)
            acc_ref[rows, :] += _dot(vt_prev_ref[rows, :], w_prev)
        _peer_gate_rows(gl_ref.at[ai], a0 + ai, *keys, w_ref, ai * NK)


def _peer_dense_kernel(x_ref, hft_ref, n_ref, p1_ref, r2_ref, e2_ref, ua_ref, ub_ref, vta_ref, vtb_ref,
                       vtl_ref, *rest, na, norm_out):
    g_ref = rest[0] if norm_out else None
    o_ref, acc_ref, wa_ref, wb_ref, gla_ref, glb_ref = rest[-6:]
    j = pl.program_id(1)
    last = pl.num_programs(1) - 1
    keys = (n_ref, p1_ref, r2_ref, e2_ref)

    @pl.when(j == 0)
    def _():
        acc_ref[...] = jnp.zeros_like(acc_ref)
        wb_ref[...] = jnp.zeros_like(wb_ref)

    hft = hft_ref[...]
    _peer_half(2 * j * na, hft, keys, ua_ref, wa_ref, vtb_ref, wb_ref, acc_ref, gla_ref, na)
    _peer_half((2 * j + 1) * na, hft, keys, ub_ref, wb_ref, vta_ref, wa_ref, acc_ref, glb_ref, na)

    @pl.when(j == last)
    def _():
        acc = acc_ref[...] + _dot(vtl_ref[...], wb_ref[...])
        xo = x_ref[...] + jnp.transpose(acc)
        o_ref[...] = _rms(xo, g_ref[...]) if norm_out else xo


def _peer_dense(x, hft, n1, p1, r2, e2, u, vt, layer, tm, na, g_out=None):
    T, D = x.shape
    H, NK = PEER_HEADS, PEER_KEYS
    th = na * NK
    steps = NK // (2 * na)
    kspec = pl.BlockSpec((H, NK, tm), lambda i, j: (0, 0, i))
    return pl.pallas_call(
        functools.partial(_peer_dense_kernel, na=na, norm_out=g_out is not None),
        grid=(T // tm, steps),
        in_specs=[pl.BlockSpec((tm, D), lambda i, j: (i, 0)),
                  pl.BlockSpec((D, tm), lambda i, j: (0, i)),
                  kspec, kspec, kspec, kspec,
                  pl.BlockSpec((None, th, D), lambda i, j: (layer, 2 * j, 0)),
                  pl.BlockSpec((None, th, D), lambda i, j: (layer, 2 * j + 1, 0)),
                  pl.BlockSpec((None, D, th), lambda i, j: (layer, 0, 2 * j)),
                  pl.BlockSpec((None, D, th), lambda i, j: (layer, 0, jnp.maximum(2 * j - 1, 1))),
                  pl.BlockSpec((None, D, th), lambda i, j: (layer, 0, 2 * steps - 1))]
                 + ([pl.BlockSpec((1, D), lambda i, j: (0, 0))] if g_out is not None else []),
        out_specs=pl.BlockSpec((tm, D), lambda i, j: (i, 0)),
        out_shape=jax.ShapeDtypeStruct((T, D), F32),
        scratch_shapes=[pltpu.VMEM((D, tm), F32), pltpu.VMEM((th, tm), BF16), pltpu.VMEM((th, tm), BF16),
                        pltpu.VMEM((na, NK, tm), BF16), pltpu.VMEM((na, NK, tm), BF16)],
        compiler_params=_params("parallel", "arbitrary"),
        name="peer_dense",
    )(x, hft, n1, p1, r2, e2, u, u, vt, vt, vt, *(() if g_out is None else (g_out,)))


def kernel(x, mem, positions, g_mix, w_in, w_ret_br, w_rnn_br, w_mix_out, conv_w, conv_b, lru_wa, lru_ba, lru_wx, lru_bx, lru_lam, g_x, g_mem, w_xq, w_xk, w_xv, w_xo, g_ffn, w_pq, sub_k1, sub_k2, peer_u, peer_v, g_final):
    B, T, D = x.shape
    assert B == 1 and D == D_MODEL
    depth = g_mix.shape[0]
    tm = min(TM_PROJ, T)
    xs = x.reshape(T, D)
    mems = mem.reshape(mem.shape[1], D)
    pos_col = positions.reshape(T, 1)
    row = lambda a: a.reshape(1, -1)

    cos2, sin2 = _rope_tables(pos_col, tm)
    u_all = peer_u.astype(BF16)
    vt_all = jnp.swapaxes(peer_v, 1, 2).astype(BF16)
    for l in range(depth):
        z = _rms_matmul(xs, row(g_mix[l]), w_in, l, tm=min(TM_IN, T), tn=TN_PROJ, out_dtype=BF16)
        ret = _retention(z, cos2, sin2, min(TB_RET, T))
        rnn = _rglru(z, pos_col, conv_w[l], row(conv_b[l]), lru_wa[l].astype(BF16), row(lru_ba[l]),
                     lru_wx[l].astype(BF16), row(lru_bx[l]), row(lru_lam[l]), min(TB_LRU, T))
        xs = _merge(xs, ret, rnn, z, w_ret_br, w_rnn_br, w_mix_out, l, tm)
        kmem = _rms_matmul(mems, row(g_mem[l]), w_xk, l, tm=mems.shape[0], tn=D, out_dtype=BF16)
        vmem = _rms_matmul(mems, row(g_mem[l]), w_xv, l, tm=mems.shape[0], tn=D, out_dtype=BF16)
        xs = _xattn(xs, row(g_x[l]), w_xq, kmem.T, vmem, w_xo, l, tm)
        tp = min(TM_PEER, T)
        hft, s1, s2 = _peer_scores(xs, row(g_ffn[l]), w_pq[l].T.astype(BF16), sub_k1[l].astype(BF16),
                                   sub_k2[l].astype(BF16), tp)
        n1, p1, r2, e2 = _peer_select(s1, s2, TL_SORT)
        xs = _peer_dense(xs, hft, n1, p1, r2, e2, u_all, vt_all, l, tp, NA_PEER,
                         g_out=row(g_final) if l == depth - 1 else None)
    return xs.reshape(B, T, D)
```

```python
import functools
import math

import numpy as np
import jax
import jax.numpy as jnp
from jax import lax
from jax.experimental import pallas as pl
from jax.experimental.pallas import tpu as pltpu

F32 = jnp.float32
BF16 = jnp.bfloat16

D_MODEL = 1024
EPS = 1e-6
RET_HEADS = 4
RET_QK_DIM = 128
RET_V_DIM = 256
RET_CHUNK = 128
ROPE_BASE = 10000.0
LRU_BLOCKS = 8
LRU_BLOCK_DIM = D_MODEL // LRU_BLOCKS
CONV_WIDTH = 4
LRU_C = 8.0
XATTN_HEADS = 4
XATTN_HEAD_DIM = D_MODEL // XATTN_HEADS
PEER_HEADS = 8
PEER_KEYS = 128
PEER_TOPK = 16
PEER_HALF = 128

LANES = 128
SUBLANES = 8
VMEM_LIMIT_BYTES = 56 * 1024 * 1024

TM_PROJ = 512
TM_IN = 2048
TN_PROJ = 1024
TB_RET = 256
TB_LRU = 512
TM_PEER = 512
NA_PEER = 4
TL_SORT = 256


def _params(*sem):
    return pltpu.CompilerParams(dimension_semantics=sem, vmem_limit_bytes=VMEM_LIMIT_BYTES)


def _rms(x, g):
    return x * lax.rsqrt(jnp.mean(x * x, axis=-1, keepdims=True) + EPS) * g


def _gelu(x):
    c = math.sqrt(2.0 / math.pi)
    return 0.5 * x * (1.0 + jnp.tanh(c * (x + 0.044715 * (x * x * x))))


def _dot(a, b):
    return jnp.dot(a, b, preferred_element_type=F32)


def _dot_nt(a, b):
    return lax.dot_general(a, b, (((1,), (1,)), ((), ())), preferred_element_type=F32)


def _rms_matmul_kernel(x_ref, g_ref, w_ref, o_ref, h_ref):
    @pl.when(pl.program_id(1) == 0)
    def _():
        h_ref[...] = _rms(x_ref[...], g_ref[...]).astype(BF16)

    o_ref[...] = _dot(h_ref[...], w_ref[...].astype(BF16)).astype(o_ref.dtype)


def _rms_matmul(x, g, w, layer, *, tm, tn, out_dtype):
    T, K = x.shape
    N = w.shape[2]
    return pl.pallas_call(
        _rms_matmul_kernel,
        grid=(T // tm, N // tn),
        in_specs=[pl.BlockSpec((tm, K), lambda i, j: (i, 0)),
                  pl.BlockSpec((1, K), lambda i, j: (0, 0)),
                  pl.BlockSpec((None, K, tn), lambda i, j: (layer, 0, j))],
        out_specs=pl.BlockSpec((tm, tn), lambda i, j: (i, j)),
        out_shape=jax.ShapeDtypeStruct((T, N), out_dtype),
        scratch_shapes=[pltpu.VMEM((tm, K), BF16)],
        compiler_params=_params("parallel", "arbitrary"),
        name="rms_matmul",
    )(x, g, w)


def _rope_kernel(pos_ref, invf_ref, cos_ref, sin_ref):
    ang = pos_ref[...].astype(F32) * invf_ref[...]
    lane = lax.broadcasted_iota(jnp.int32, ang.shape, 1)
    s = jnp.sin(ang)
    cos_ref[...] = jnp.cos(ang)
    sin_ref[...] = jnp.where(lane < RET_QK_DIM // 2, -s, s)


def _rope_tables(pos_col, tb):
    T = pos_col.shape[0]
    half = RET_QK_DIM // 2
    inv_freq = ROPE_BASE ** (-jnp.arange(0, RET_QK_DIM, 2, dtype=F32) / RET_QK_DIM)
    invf2 = jnp.concatenate([inv_freq, inv_freq]).reshape(1, 2 * half)
    return pl.pallas_call(
        _rope_kernel,
        grid=(T // tb,),
        in_specs=[pl.BlockSpec((tb, 1), lambda i: (i, 0)),
                  pl.BlockSpec((1, RET_QK_DIM), lambda i: (0, 0))],
        out_specs=[pl.BlockSpec((tb, RET_QK_DIM), lambda i: (i, 0))] * 2,
        out_shape=[jax.ShapeDtypeStruct((T, RET_QK_DIM), F32)] * 2,
        compiler_params=_params("parallel"),
        name="rope_tables",
    )(pos_col, invf2)


def _retention_kernel(q_ref, k_ref, v_ref, g_ref, cos_ref, sin_ref, dec_ref, kdec_ref, qdec_ref,
                      o_ref, s_ref, *, chunk_decay):
    C, dk, dv = RET_CHUNK, RET_QK_DIM, RET_V_DIM

    @pl.when(pl.program_id(0) == 0)
    def _():
        s_ref[...] = jnp.zeros_like(s_ref)

    kscale = dk ** -0.5
    for c in range(q_ref.shape[0] // C):
        rows = slice(c * C, (c + 1) * C)
        cos = cos_ref[rows, :]
        sin = sin_ref[rows, :]
        for h in range(RET_HEADS):
            q = q_ref[rows, h * dk:(h + 1) * dk].astype(F32)
            k = k_ref[rows, h * dk:(h + 1) * dk].astype(F32)
            v = v_ref[rows, h * dv:(h + 1) * dv].astype(BF16)
            qr = q * cos + pltpu.roll(q, dk // 2, 1) * sin
            kr = (k * cos + pltpu.roll(k, dk // 2, 1) * sin) * kscale
            scores = _dot_nt(qr.astype(BF16), kr.astype(BF16)) * dec_ref[h]
            inner = _dot(scores.astype(BF16), v)
            state = s_ref[h]
            cross = _dot((qr * qdec_ref[h]).astype(BF16), state.astype(BF16))
            kd_t = jnp.transpose(kr * kdec_ref[h]).astype(BF16)
            s_ref[h] = chunk_decay[h] * state + _dot(kd_t, v)
            o = inner + cross
            mu = jnp.mean(o, axis=-1, keepdims=True)
            oc = o - mu
            var = jnp.mean(oc * oc, axis=-1, keepdims=True)
            o = oc * lax.rsqrt(var + EPS)
            g = g_ref[rows, h * dv:(h + 1) * dv].astype(F32)
            o_ref[rows, h * dv:(h + 1) * dv] = (g * jax.nn.sigmoid(g) * o).astype(o_ref.dtype)


def _retention(z, cos2, sin2, tb):
    T = z.shape[0]
    H, C, dk, dv = RET_HEADS, RET_CHUNK, RET_QK_DIM, RET_V_DIM
    log_g = jnp.log(1.0 - 2.0 ** (-5.0 - jnp.arange(H, dtype=F32)))
    i = jnp.arange(C, dtype=F32)
    diff = i[:, None] - i[None, :]
    decay = jnp.where(diff[None] >= 0, jnp.exp(diff[None] * log_g[:, None, None]), 0.0)
    k_decay = jnp.exp((C - 1.0 - i)[None, :] * log_g[:, None])
    q_decay = jnp.exp((i + 1.0)[None, :] * log_g[:, None])
    kdec = jnp.broadcast_to(k_decay[:, :, None], (H, C, dk))
    qdec = jnp.broadcast_to(q_decay[:, :, None], (H, C, dk))
    log_g_np = np.log(1.0 - 2.0 ** (-5.0 - np.arange(H, dtype=np.float32))).astype(np.float32)
    chunk_decay = tuple(float(np.exp(np.float32(C) * lg)) for lg in log_g_np)
    nq = H * dk
    const3 = lambda i: (0, 0, 0)
    return pl.pallas_call(
        functools.partial(_retention_kernel, chunk_decay=chunk_decay),
        grid=(T // tb,),
        in_specs=[pl.BlockSpec((tb, nq), lambda i: (i, 0)),
                  pl.BlockSpec((tb, nq), lambda i: (i, 1)),
                  pl.BlockSpec((tb, H * dv), lambda i: (i, 1)),
                  pl.BlockSpec((tb, H * dv), lambda i: (i, 2)),
                  pl.BlockSpec((tb, dk), lambda i: (i, 0)),
                  pl.BlockSpec((tb, dk), lambda i: (i, 0)),
                  pl.BlockSpec((H, C, C), const3),
                  pl.BlockSpec((H, C, dk), const3),
                  pl.BlockSpec((H, C, dk), const3)],
        out_specs=pl.BlockSpec((tb, H * dv), lambda i: (i, 0)),
        out_shape=jax.ShapeDtypeStruct((T, H * dv), BF16),
        scratch_shapes=[pltpu.VMEM((H, dk, dv), F32)],
        compiler_params=_params("arbitrary"),
        name="retention",
    )(z, z, z, z, cos2, sin2, decay, kdec, qdec)


def _rglru_kernel(x_ref, y_ref, pos_ref, cw_ref, cb_ref, wa_ref, ba_ref, wx_ref, bx_ref, lam_ref,
                  o_ref, xpad_ref, a_ref, b_ref, hc_ref):
    tb = x_ref.shape[0]
    pad = SUBLANES
    halo = SUBLANES

    @pl.when(pl.program_id(0) == 0)
    def _():
        xpad_ref[0:halo, :] = jnp.zeros((halo, D_MODEL), F32)
        hc_ref[...] = jnp.zeros_like(hc_ref)
        a_ref[0:pad, :] = jnp.ones((pad, D_MODEL), F32)
        b_ref[0:pad, :] = jnp.zeros((pad, D_MODEL), F32)

    xpad_ref[halo:halo + tb, :] = x_ref[...].astype(F32)
    xc = cb_ref[...] + cw_ref[3:4, :] * xpad_ref[halo:halo + tb, :]
    for w in range(CONV_WIDTH - 1):
        off = halo - (CONV_WIDTH - 1) + w
        xc = xc + cw_ref[w:w + 1, :] * xpad_ref[off:off + tb, :]
    xpad_ref[0:halo, :] = xpad_ref[tb:tb + halo, :]

    reset = pos_ref[...] == 0
    lam = lam_ref[...]
    nl = -lam
    sp = jnp.maximum(nl, 0.0) + jnp.log(1.0 + jnp.exp(-jnp.abs(nl)))
    bd = LRU_BLOCK_DIM
    for n in range(LRU_BLOCKS):
        cols = slice(n * bd, (n + 1) * bd)
        xcn = xc[:, cols]
        xb = xcn.astype(BF16)
        gate_r = jax.nn.sigmoid(_dot(xb, wa_ref[n]) + ba_ref[:, cols])
        gate_i = jax.nn.sigmoid(_dot(xb, wx_ref[n]) + bx_ref[:, cols])
        log_a = (-LRU_C) * gate_r * sp[:, cols]
        a = jnp.exp(log_a)
        mult = jnp.sqrt(jnp.tanh(-log_a) * (a * a + 1.0))
        a = jnp.where(reset, 0.0, a)
        mult = jnp.where(reset, 1.0, mult)
        b = xcn * gate_i * mult
        s = 1
        while s < SUBLANES:
            a_ref[pad:pad + tb, cols] = a
            b_ref[pad:pad + tb, cols] = b
            a_sh = a_ref[pad - s:pad - s + tb, cols]
            b_sh = b_ref[pad - s:pad - s + tb, cols]
            b = a * b_sh + b
            a = a * a_sh
            s *= 2
        hg = hc_ref[:, cols]
        groups = []
        for g in range(tb // SUBLANES):
            rows = slice(g * SUBLANES, (g + 1) * SUBLANES)
            hg = a[rows, :] * hg + b[rows, :]
            groups.append(hg)
        h = jnp.concatenate(groups, axis=0)
        hc_ref[:, cols] = jnp.broadcast_to(hg[SUBLANES - 1:SUBLANES, :], (SUBLANES, bd))
        o_ref[:, cols] = (h * _gelu(y_ref[:, cols].astype(F32))).astype(o_ref.dtype)


def _rglru(z, pos_col, conv_w, conv_b, wa, ba, wx, bx, lam, tb):
    T = z.shape[0]
    W = D_MODEL
    row = lambda i: (0, 0)
    const3 = lambda i: (0, 0, 0)
    return pl.pallas_call(
        _rglru_kernel,
        grid=(T // tb,),
        in_specs=[pl.BlockSpec((tb, W), lambda i: (i, 3)),
                  pl.BlockSpec((tb, W), lambda i: (i, 4)),
                  pl.BlockSpec((tb, 1), lambda i: (i, 0)),
                  pl.BlockSpec((CONV_WIDTH, W), row),
                  pl.BlockSpec((1, W), row),
                  pl.BlockSpec((LRU_BLOCKS, LRU_BLOCK_DIM, LRU_BLOCK_DIM), const3),
                  pl.BlockSpec((1, W), row),
                  pl.BlockSpec((LRU_BLOCKS, LRU_BLOCK_DIM, LRU_BLOCK_DIM), const3),
                  pl.BlockSpec((1, W), row),
                  pl.BlockSpec((1, W), row)],
        out_specs=pl.BlockSpec((tb, W), lambda i: (i, 0)),
        out_shape=jax.ShapeDtypeStruct((T, W), BF16),
        scratch_shapes=[pltpu.VMEM((tb + 2 * SUBLANES, W), F32),
                        pltpu.VMEM((tb + SUBLANES, W), F32),
                        pltpu.VMEM((tb + SUBLANES, W), F32),
                        pltpu.VMEM((SUBLANES, W), F32)],
        compiler_params=_params("arbitrary"),
        name="rglru",
    )(z, z, pos_col, conv_w, conv_b, wa, ba, wx, bx, lam)


def _merge_kernel(x_ref, ret_ref, rnn_ref, ga_ref, gb_ref, wr_ref, wn_ref, wo_ref, o_ref, w_ref):
    @pl.when(pl.program_id(0) == 0)
    def _():
        for k, src in enumerate((wr_ref, wn_ref, wo_ref)):
            w_ref[k] = src[...].astype(BF16)

    p_ret = _dot(ret_ref[...], w_ref[0])
    p_rnn = _dot(rnn_ref[...], w_ref[1])
    merged = (jax.nn.sigmoid(ga_ref[...].astype(F32)) * p_ret
              + jax.nn.sigmoid(gb_ref[...].astype(F32)) * p_rnn)
    o_ref[...] = x_ref[...] + _dot(merged.astype(BF16), w_ref[2])


def _merge(x, ret, rnn, z, w_ret, w_rnn, w_out, layer, tm):
    T, D = x.shape
    tile = lambda i: (i, 0)
    wspec = pl.BlockSpec((None, D, D), lambda i: (layer, 0, 0))
    return pl.pallas_call(
        _merge_kernel,
        grid=(T // tm,),
        in_specs=[pl.BlockSpec((tm, D), tile), pl.BlockSpec((tm, D), tile), pl.BlockSpec((tm, D), tile),
                  pl.BlockSpec((tm, D), lambda i: (i, 5)),
                  pl.BlockSpec((tm, D), lambda i: (i, 6)),
                  wspec, wspec, wspec],
        out_specs=pl.BlockSpec((tm, D), tile),
        out_shape=jax.ShapeDtypeStruct((T, D), F32),
        scratch_shapes=[pltpu.VMEM((3, D, D), BF16)],
        compiler_params=_params("arbitrary"),
        name="merge",
    )(x, ret, rnn, z, z, w_ret, w_rnn, w_out)


def _xattn_kernel(x_ref, g_ref, wq_ref, kt_ref, v_ref, wo_ref, o_ref, att_ref, w_ref):
    @pl.when(pl.program_id(0) == 0)
    def _():
        w_ref[0] = wq_ref[...].astype(BF16)
        w_ref[1] = wo_ref[...].astype(BF16)

    x = x_ref[...]
    q = _dot(_rms(x, g_ref[...]).astype(BF16), w_ref[0])
    dh = XATTN_HEAD_DIM
    for h in range(XATTN_HEADS):
        cols = slice(h * dh, (h + 1) * dh)
        s = _dot(q[:, cols].astype(BF16), kt_ref[cols, :]) * (dh ** -0.5)
        s = s - jnp.max(s, axis=-1, keepdims=True)
        e = jnp.exp(s)
        p = e / jnp.sum(e, axis=-1, keepdims=True)
        att_ref[:, cols] = _dot(p.astype(BF16), v_ref[:, cols]).astype(BF16)
    o_ref[...] = x + _dot(att_ref[...], w_ref[1])


def _xattn(x, g, wq, kt, v, wo, layer, tm):
    T, D = x.shape
    M = v.shape[0]
    tile = lambda i: (i, 0)
    fixed = lambda i: (0, 0)
    wspec = pl.BlockSpec((None, D, D), lambda i: (layer, 0, 0))
    return pl.pallas_call(
        _xattn_kernel,
        grid=(T // tm,),
        in_specs=[pl.BlockSpec((tm, D), tile), pl.BlockSpec((1, D), fixed), wspec,
                  pl.BlockSpec((D, M), fixed), pl.BlockSpec((M, D), fixed), wspec],
        out_specs=pl.BlockSpec((tm, D), tile),
        out_shape=jax.ShapeDtypeStruct((T, D), F32),
        scratch_shapes=[pltpu.VMEM((tm, D), BF16), pltpu.VMEM((2, D, D), BF16)],
        compiler_params=_params("arbitrary"),
        name="xattn",
    )(x, g, wq, kt, v, wo)


def _peer_scores_kernel(x_ref, g_ref, wq_ref, k1_ref, k2_ref, hft_ref, s1_ref, s2_ref):
    hf = _rms(x_ref[...], g_ref[...])
    hft = jnp.transpose(hf).astype(BF16)
    hft_ref[...] = hft
    qt = _dot(wq_ref[...], hft)
    kd = 2 * PEER_HALF
    for h in range(PEER_HEADS):
        q1 = qt[h * kd:h * kd + PEER_HALF, :].astype(BF16)
        q2 = qt[h * kd + PEER_HALF:(h + 1) * kd, :].astype(BF16)
        s1_ref[h] = _dot(k1_ref[h], q1)
        s2_ref[h] = _dot(k2_ref[h], q2)


def _peer_scores(x, g, wq_t, k1, k2, tm):
    T, D = x.shape
    H, NK = PEER_HEADS, PEER_KEYS
    fixed = lambda i: (0, 0)
    const3 = lambda i: (0, 0, 0)
    return pl.pallas_call(
        _peer_scores_kernel,
        grid=(T // tm,),
        in_specs=[pl.BlockSpec((tm, D), lambda i: (i, 0)), pl.BlockSpec((1, D), fixed),
                  pl.BlockSpec(wq_t.shape, fixed),
                  pl.BlockSpec((H, NK, PEER_HALF), const3), pl.BlockSpec((H, NK, PEER_HALF), const3)],
        out_specs=[pl.BlockSpec((D, tm), lambda i: (0, i)),
                   pl.BlockSpec((H, NK, tm), lambda i: (0, 0, i)),
                   pl.BlockSpec((H, NK, tm), lambda i: (0, 0, i))],
        out_shape=[jax.ShapeDtypeStruct((D, T), BF16),
                   jax.ShapeDtypeStruct((H, NK, T), F32),
                   jax.ShapeDtypeStruct((H, NK, T), F32)],
        compiler_params=_params("parallel"),
        name="peer_scores",
    )(x, g, wq_t, k1, k2)


def _sort_pairs(n):
    pairs = []
    p = 1
    while p < n:
        k = p
        while k >= 1:
            for j in range(k % p, n - k, 2 * k):
                for i in range(min(k, n - j - k)):
                    if (i + j) // (2 * p) == (i + j + k) // (2 * p):
                        pairs.append((i + j, i + j + k))
            k //= 2
        p *= 2
    return pairs


def _sort_desc(vs):
    vs = list(vs)
    for lo, hi in _sort_pairs(len(vs)):
        a, b = vs[lo], vs[hi]
        vs[lo], vs[hi] = jnp.maximum(a, b), jnp.minimum(a, b)
    return vs


def _merge_top(a, b):
    n = len(a)
    c = [jnp.maximum(a[i], b[n - 1 - i]) for i in range(n)]
    k = n // 2
    while k >= 1:
        for i in range(n):
            if i & k == 0:
                x, y = c[i], c[i + k]
                c[i], c[i + k] = jnp.maximum(x, y), jnp.minimum(x, y)
        k //= 2
    return c


def _top_of_keys(groups, roll):
    vs = _sort_desc(groups)
    shift = SUBLANES // 2
    while shift >= 1:
        vs = _merge_top(vs, [roll(v, shift, 0) for v in vs])
        shift //= 2
    return vs


def _peer_select_math(s1, s2, roll):
    K = PEER_TOPK
    shape = s1[0][0].shape
    sub = lax.broadcasted_iota(jnp.int32, shape, 0)
    zero = jnp.zeros(shape, F32)
    tops2 = []
    v1 = [zero] * K
    v2 = [zero] * K
    for h in range(PEER_HEADS):
        t1 = _top_of_keys(s1[h], roll)
        t2 = _top_of_keys(s2[h], roll)
        tops2.append(t2)
        v1 = [jnp.where(sub == h, t, v) for t, v in zip(t1, v1)]
        v2 = [jnp.where(sub == h, t, v) for t, v in zip(t2, v2)]
    cands = [v1[i] + v2[j] for i in range(K) for j in range(K // (i + 1))]
    n = 1
    while n < len(cands):
        n *= 2
    cands = cands + [jnp.full(shape, -jnp.inf, F32)] * (n - len(cands))
    top = _sort_desc(cands)[:K]
    z = zero
    for t in top:
        z = z + jnp.exp(t - top[0])
    tau_all, zinv_all = top[K - 1], 1.0 / z
    out = []
    for h in range(PEER_HEADS):
        bc = lambda v: jnp.broadcast_to(v[h:h + 1, :], shape)
        tau, zinv, m1, m2 = bc(tau_all), bc(zinv_all), bc(v1[0]), bc(v2[0])
        t2 = tops2[h]
        cnt, p1, rk, e2 = [], [], [], []
        c_top = zero
        for j in range(K // 2, K):
            c_top = jnp.where(m1 + t2[j] >= tau, float(j + 1), c_top)
        for s1g, s2g in zip(s1[h], s2[h]):
            c = zero
            r = zero
            for j in range(K // 2):
                c = jnp.where(s1g + t2[j] >= tau, float(j + 1), c)
            c = jnp.where(s1g == m1, jnp.maximum(c, c_top), c)
            for j in range(K):
                r = jnp.where(t2[j] > s2g, float(j + 1), r)
            cnt.append(c)
            rk.append(r)
            p1.append(jnp.exp(s1g - m1) * zinv)
            e2.append(jnp.exp(s2g - m2))
        out.append((cnt, p1, rk, e2))
    return out


def _bf16_pair(x):
    u = lax.bitcast_convert_type(x.astype(BF16).astype(F32), jnp.uint32)
    return u | (u >> 16)


def _peer_select_kernel(s1_ref, s2_ref, n_ref, p1_ref, r2_ref, e2_ref):
    ng = PEER_KEYS // SUBLANES
    grp = lambda ref, h: [ref[h, i * SUBLANES:(i + 1) * SUBLANES, :] for i in range(ng)]
    s1 = [grp(s1_ref, h) for h in range(PEER_HEADS)]
    s2 = [grp(s2_ref, h) for h in range(PEER_HEADS)]
    res = _peer_select_math(s1, s2, pltpu.roll)
    pk = 2 * SUBLANES
    for h, (cnt, p1, rk, e2) in enumerate(res):
        for i in range(ng):
            rows = slice(i * SUBLANES, (i + 1) * SUBLANES)
            n_ref[h, rows, :] = _bf16_pair(cnt[i])
            p1_ref[h, rows, :] = _bf16_pair(p1[i])
        for i in range(ng // 2):
            rows = slice(i * pk, (i + 1) * pk)
            r2_ref[h, rows, :] = jnp.concatenate([rk[2 * i], rk[2 * i + 1]], axis=0).astype(BF16)
            e2_ref[h, rows, :] = jnp.concatenate([e2[2 * i], e2[2 * i + 1]], axis=0).astype(BF16)


def _peer_select(s1, s2, tl):
    H, NK, T = s1.shape
    spec = pl.BlockSpec((H, NK, tl), lambda i: (0, 0, i))
    return pl.pallas_call(
        _peer_select_kernel,
        grid=(T // tl,),
        in_specs=[spec] * 2,
        out_specs=[spec] * 4,
        out_shape=[jax.ShapeDtypeStruct((H, NK, T), jnp.uint32), jax.ShapeDtypeStruct((H, NK, T), jnp.uint32),
                   jax.ShapeDtypeStruct((H, NK, T), BF16), jax.ShapeDtypeStruct((H, NK, T), BF16)],
        compiler_params=_params("parallel"),
        name="peer_select",
    )(s1, s2)


def _peer_gate_rows(act, a, n_ref, p1_ref, r2_ref, e2_ref, w_ref, row0):
    NK = PEER_KEYS
    tm = act.shape[-1]
    pk = 2 * SUBLANES
    zero = jnp.zeros((pk, tm), BF16)
    bc = lambda ref, h: pltpu.bitcast(jnp.broadcast_to(ref[h, pl.ds(a, 1), :], (SUBLANES, tm)), BF16)
    nrow = [bc(n_ref, h) for h in range(PEER_HEADS)]
    prow = [bc(p1_ref, h) for h in range(PEER_HEADS)]
    for g in range(NK // pk):
        b_rows = slice(g * pk, (g + 1) * pk)
        gate = zero
        for h in range(PEER_HEADS):
            sel = r2_ref[h, b_rows, :] < nrow[h]
            gate = gate + jnp.where(sel, prow[h] * e2_ref[h, b_rows, :], zero)
        w_ref[row0 + g * pk:row0 + (g + 1) * pk, :] = gate * act[b_rows, :]


def _peer_half(a0, hft, keys, u_ref, w_ref, vt_prev_ref, w_prev_ref, acc_ref, gl_ref, na):
    NK = PEER_KEYS
    dr = acc_ref.shape[0] // na
    w_prev = w_prev_ref[...]
    for ai in range(na):
        gl_ref[ai] = _gelu(_dot(u_ref[ai * NK:(ai + 1) * NK, :], hft).astype(BF16))
        if ai % 2 == 0:
            rows = slice(ai * dr, (ai + 2) * dr)
            acc_ref[rows, :] += _dot(vt_prev_ref[rows, :], w_prev)
        _peer_gate_rows(gl_ref.at[ai], a0 + ai, *keys, w_ref, ai * NK)


def _peer_dense_kernel(x_ref, hft_ref, n_ref, p1_ref, r2_ref, e2_ref, ua_ref, ub_ref, vta_ref, vtb_ref,
                       vtl_ref, *rest, na, norm_out):
    g_ref = rest[0] if norm_out else None
    o_ref, acc_ref, wa_ref, wb_ref, gla_ref, glb_ref = rest[-6:]
    j = pl.program_id(1)
    last = pl.num_programs(1) - 1
    keys = (n_ref, p1_ref, r2_ref, e2_ref)

    @pl.when(j == 0)
    def _():
        acc_ref[...] = jnp.zeros_like(acc_ref)
        wb_ref[...] = jnp.zeros_like(wb_ref)

    hft = hft_ref[...]
    _peer_half(2 * j * na, hft, keys, ua_ref, wa_ref, vtb_ref, wb_ref, acc_ref, gla_ref, na)
    _peer_half((2 * j + 1) * na, hft, keys, ub_ref, wb_ref, vta_ref, wa_ref, acc_ref, glb_ref, na)

    @pl.when(j == last)
    def _():
        acc = acc_ref[...] + _dot(vtl_ref[...], wb_ref[...])
        xo = x_ref[...] + jnp.transpose(acc)
        o_ref[...] = _rms(xo, g_ref[...]) if norm_out else xo


def _peer_dense(x, hft, n1, p1, r2, e2, u, vt, layer, tm, na, g_out=None):
    T, D = x.shape
    H, NK = PEER_HEADS, PEER_KEYS
    th = na * NK
    steps = NK // (2 * na)
    kspec = pl.BlockSpec((H, NK, tm), lambda i, j: (0, 0, i))
    return pl.pallas_call(
        functools.partial(_peer_dense_kernel, na=na, norm_out=g_out is not None),
        grid=(T // tm, steps),
        in_specs=[pl.BlockSpec((tm, D), lambda i, j: (i, 0)),
                  pl.BlockSpec((D, tm), lambda i, j: (0, i)),
                  kspec, kspec, kspec, kspec,
                  pl.BlockSpec((None, th, D), lambda i, j: (layer, 2 * j, 0)),
                  pl.BlockSpec((None, th, D), lambda i, j: (layer, 2 * j + 1, 0)),
                  pl.BlockSpec((None, None, D, th), lambda i, j: (layer, 2 * j, 0, 0)),
                  pl.BlockSpec((None, None, D, th), lambda i, j: (layer, jnp.maximum(2 * j - 1, 1), 0, 0)),
                  pl.BlockSpec((None, None, D, th), lambda i, j: (layer, 2 * steps - 1, 0, 0))]
                 + ([pl.BlockSpec((1, D), lambda i, j: (0, 0))] if g_out is not None else []),
        out_specs=pl.BlockSpec((tm, D), lambda i, j: (i, 0)),
        out_shape=jax.ShapeDtypeStruct((T, D), F32),
        scratch_shapes=[pltpu.VMEM((D, tm), F32), pltpu.VMEM((th, tm), BF16), pltpu.VMEM((th, tm), BF16),
                        pltpu.VMEM((na, NK, tm), BF16), pltpu.VMEM((na, NK, tm), BF16)],
        compiler_params=_params("parallel", "arbitrary"),
        name="peer_dense",
    )(x, hft, n1, p1, r2, e2, u, u, vt, vt, vt, *(() if g_out is None else (g_out,)))


def kernel(x, mem, positions, g_mix, w_in, w_ret_br, w_rnn_br, w_mix_out, conv_w, conv_b, lru_wa, lru_ba, lru_wx, lru_bx, lru_lam, g_x, g_mem, w_xq, w_xk, w_xv, w_xo, g_ffn, w_pq, sub_k1, sub_k2, peer_u, peer_v, g_final):
    B, T, D = x.shape
    assert B == 1 and D == D_MODEL
    depth = g_mix.shape[0]
    tm = min(TM_PROJ, T)
    xs = x.reshape(T, D)
    mems = mem.reshape(mem.shape[1], D)
    pos_col = positions.reshape(T, 1)
    row = lambda a: a.reshape(1, -1)

    cos2, sin2 = _rope_tables(pos_col, tm)
    u_all = peer_u.astype(BF16)
    th = NA_PEER * PEER_KEYS
    vt_all = jnp.swapaxes(peer_v.reshape(depth, -1, th, D), 2, 3).astype(BF16)
    for l in range(depth):
        z = _rms_matmul(xs, row(g_mix[l]), w_in, l, tm=min(TM_IN, T), tn=TN_PROJ, out_dtype=BF16)
        ret = _retention(z, cos2, sin2, min(TB_RET, T))
        rnn = _rglru(z, pos_col, conv_w[l], row(conv_b[l]), lru_wa[l].astype(BF16), row(lru_ba[l]),
                     lru_wx[l].astype(BF16), row(lru_bx[l]), row(lru_lam[l]), min(TB_LRU, T))
        xs = _merge(xs, ret, rnn, z, w_ret_br, w_rnn_br, w_mix_out, l, tm)
        kmem = _rms_matmul(mems, row(g_mem[l]), w_xk, l, tm=mems.shape[0], tn=D, out_dtype=BF16)
        vmem = _rms_matmul(mems, row(g_mem[l]), w_xv, l, tm=mems.shape[0], tn=D, out_dtype=BF16)
        xs = _xattn(xs, row(g_x[l]), w_xq, kmem.T, vmem, w_xo, l, tm)
        tp = min(TM_PEER, T)
        hft, s1, s2 = _peer_scores(xs, row(g_ffn[l]), w_pq[l].T.astype(BF16), sub_k1[l].astype(BF16),
                                   sub_k2[l].astype(BF16), tp)
        n1, p1, r2, e2 = _peer_select(s1, s2, TL_SORT)
        xs = _peer_dense(xs, hft, n1, p1, r2, e2, u_all, vt_all, l, tp, NA_PEER,
                         g_out=row(g_final) if l == depth - 1 else None)
    return xs.reshape(B, T, D)
```

```python
import functools
import math

import numpy as np
import jax
import jax.numpy as jnp
from jax import lax
from jax.experimental import pallas as pl
from jax.experimental.pallas import tpu as pltpu

F32 = jnp.float32
BF16 = jnp.bfloat16

D_MODEL = 1024
EPS = 1e-6
RET_HEADS = 4
RET_QK_DIM = 128
RET_V_DIM = 256
RET_CHUNK = 128
ROPE_BASE = 10000.0
LRU_BLOCKS = 8
LRU_BLOCK_DIM = D_MODEL // LRU_BLOCKS
CONV_WIDTH = 4
LRU_C = 8.0
XATTN_HEADS = 4
XATTN_HEAD_DIM = D_MODEL // XATTN_HEADS
PEER_HEADS = 8
PEER_KEYS = 128
PEER_TOPK = 16
PEER_HALF = 128

LANES = 128
SUBLANES = 8
VMEM_LIMIT_BYTES = 56 * 1024 * 1024

TM_PROJ = 512
TM_IN = 2048
TN_PROJ = 1024
TB_RET = 256
TB_LRU = 512
TM_PEER = 512
NA_PEER = 4
TL_SORT = 256


def _params(*sem):
    return pltpu.CompilerParams(dimension_semantics=sem, vmem_limit_bytes=VMEM_LIMIT_BYTES)


def _rms(x, g):
    return x * lax.rsqrt(jnp.mean(x * x, axis=-1, keepdims=True) + EPS) * g


def _gelu(x):
    c = math.sqrt(2.0 / math.pi)
    return 0.5 * x * (1.0 + jnp.tanh(c * (x + 0.044715 * (x * x * x))))


def _dot(a, b):
    return jnp.dot(a, b, preferred_element_type=F32)


def _dot_nt(a, b):
    return lax.dot_general(a, b, (((1,), (1,)), ((), ())), preferred_element_type=F32)


def _rms_matmul_kernel(x_ref, g_ref, w_ref, o_ref, h_ref):
    @pl.when(pl.program_id(1) == 0)
    def _():
        h_ref[...] = _rms(x_ref[...], g_ref[...]).astype(BF16)

    o_ref[...] = _dot(h_ref[...], w_ref[...].astype(BF16)).astype(o_ref.dtype)


def _rms_matmul(x, g, w, layer, *, tm, tn, out_dtype):
    T, K = x.shape
    N = w.shape[2]
    return pl.pallas_call(
        _rms_matmul_kernel,
        grid=(T // tm, N // tn),
        in_specs=[pl.BlockSpec((tm, K), lambda i, j: (i, 0)),
                  pl.BlockSpec((1, K), lambda i, j: (0, 0)),
                  pl.BlockSpec((None, K, tn), lambda i, j: (layer, 0, j))],
        out_specs=pl.BlockSpec((tm, tn), lambda i, j: (i, j)),
        out_shape=jax.ShapeDtypeStruct((T, N), out_dtype),
        scratch_shapes=[pltpu.VMEM((tm, K), BF16)],
        compiler_params=_params("parallel", "arbitrary"),
        name="rms_matmul",
    )(x, g, w)


def _rope_kernel(pos_ref, invf_ref, cos_ref, sin_ref):
    ang = pos_ref[...].astype(F32) * invf_ref[...]
    lane = lax.broadcasted_iota(jnp.int32, ang.shape, 1)
    s = jnp.sin(ang)
    cos_ref[...] = jnp.cos(ang)
    sin_ref[...] = jnp.where(lane < RET_QK_DIM // 2, -s, s)


def _rope_tables(pos_col, tb):
    T = pos_col.shape[0]
    half = RET_QK_DIM // 2
    inv_freq = ROPE_BASE ** (-jnp.arange(0, RET_QK_DIM, 2, dtype=F32) / RET_QK_DIM)
    invf2 = jnp.concatenate([inv_freq, inv_freq]).reshape(1, 2 * half)
    return pl.pallas_call(
        _rope_kernel,
        grid=(T // tb,),
        in_specs=[pl.BlockSpec((tb, 1), lambda i: (i, 0)),
                  pl.BlockSpec((1, RET_QK_DIM), lambda i: (0, 0))],
        out_specs=[pl.BlockSpec((tb, RET_QK_DIM), lambda i: (i, 0))] * 2,
        out_shape=[jax.ShapeDtypeStruct((T, RET_QK_DIM), F32)] * 2,
        compiler_params=_params("parallel"),
        name="rope_tables",
    )(pos_col, invf2)


def _retention_kernel(q_ref, k_ref, v_ref, g_ref, cos_ref, sin_ref, dec_ref, kdec_ref, qdec_ref,
                      o_ref, s_ref, *, chunk_decay):
    C, dk, dv = RET_CHUNK, RET_QK_DIM, RET_V_DIM

    @pl.when(pl.program_id(0) == 0)
    def _():
        s_ref[...] = jnp.zeros_like(s_ref)

    kscale = dk ** -0.5
    for c in range(q_ref.shape[0] // C):
        rows = slice(c * C, (c + 1) * C)
        cos = cos_ref[rows, :]
        sin = sin_ref[rows, :]
        for h in range(RET_HEADS):
            q = q_ref[rows, h * dk:(h + 1) * dk].astype(F32)
            k = k_ref[rows, h * dk:(h + 1) * dk].astype(F32)
            v = v_ref[rows, h * dv:(h + 1) * dv].astype(BF16)
            qr = q * cos + pltpu.roll(q, dk // 2, 1) * sin
            kr = (k * cos + pltpu.roll(k, dk // 2, 1) * sin) * kscale
            scores = _dot_nt(qr.astype(BF16), kr.astype(BF16)) * dec_ref[h]
            inner = _dot(scores.astype(BF16), v)
            state = s_ref[h]
            cross = _dot((qr * qdec_ref[h]).astype(BF16), state.astype(BF16))
            kd_t = jnp.transpose(kr * kdec_ref[h]).astype(BF16)
            s_ref[h] = chunk_decay[h] * state + _dot(kd_t, v)
            o = inner + cross
            mu = jnp.mean(o, axis=-1, keepdims=True)
            oc = o - mu
            var = jnp.mean(oc * oc, axis=-1, keepdims=True)
            o = oc * lax.rsqrt(var + EPS)
            g = g_ref[rows, h * dv:(h + 1) * dv].astype(F32)
            o_ref[rows, h * dv:(h + 1) * dv] = (g * jax.nn.sigmoid(g) * o).astype(o_ref.dtype)


def _retention(z, cos2, sin2, tb):
    T = z.shape[0]
    H, C, dk, dv = RET_HEADS, RET_CHUNK, RET_QK_DIM, RET_V_DIM
    log_g = jnp.log(1.0 - 2.0 ** (-5.0 - jnp.arange(H, dtype=F32)))
    i = jnp.arange(C, dtype=F32)
    diff = i[:, None] - i[None, :]
    decay = jnp.where(diff[None] >= 0, jnp.exp(diff[None] * log_g[:, None, None]), 0.0)
    k_decay = jnp.exp((C - 1.0 - i)[None, :] * log_g[:, None])
    q_decay = jnp.exp((i + 1.0)[None, :] * log_g[:, None])
    kdec = jnp.broadcast_to(k_decay[:, :, None], (H, C, dk))
    qdec = jnp.broadcast_to(q_decay[:, :, None], (H, C, dk))
    log_g_np = np.log(1.0 - 2.0 ** (-5.0 - np.arange(H, dtype=np.float32))).astype(np.float32)
    chunk_decay = tuple(float(np.exp(np.float32(C) * lg)) for lg in log_g_np)
    nq = H * dk
    const3 = lambda i: (0, 0, 0)
    return pl.pallas_call(
        functools.partial(_retention_kernel, chunk_decay=chunk_decay),
        grid=(T // tb,),
        in_specs=[pl.BlockSpec((tb, nq), lambda i: (i, 0)),
                  pl.BlockSpec((tb, nq), lambda i: (i, 1)),
                  pl.BlockSpec((tb, H * dv), lambda i: (i, 1)),
                  pl.BlockSpec((tb, H * dv), lambda i: (i, 2)),
                  pl.BlockSpec((tb, dk), lambda i: (i, 0)),
                  pl.BlockSpec((tb, dk), lambda i: (i, 0)),
                  pl.BlockSpec((H, C, C), const3),
                  pl.BlockSpec((H, C, dk), const3),
                  pl.BlockSpec((H, C, dk), const3)],
        out_specs=pl.BlockSpec((tb, H * dv), lambda i: (i, 0)),
        out_shape=jax.ShapeDtypeStruct((T, H * dv), BF16),
        scratch_shapes=[pltpu.VMEM((H, dk, dv), F32)],
        compiler_params=_params("arbitrary"),
        name="retention",
    )(z, z, z, z, cos2, sin2, decay, kdec, qdec)


def _rglru_kernel(x_ref, y_ref, pos_ref, cw_ref, cb_ref, wa_ref, ba_ref, wx_ref, bx_ref, lam_ref,
                  o_ref, xpad_ref, a_ref, b_ref, hc_ref):
    tb = x_ref.shape[0]
    pad = SUBLANES
    halo = SUBLANES

    @pl.when(pl.program_id(0) == 0)
    def _():
        xpad_ref[0:halo, :] = jnp.zeros((halo, D_MODEL), F32)
        hc_ref[...] = jnp.zeros_like(hc_ref)
        a_ref[0:pad, :] = jnp.ones((pad, D_MODEL), F32)
        b_ref[0:pad, :] = jnp.zeros((pad, D_MODEL), F32)

    xpad_ref[halo:halo + tb, :] = x_ref[...].astype(F32)
    xc = cb_ref[...] + cw_ref[3:4, :] * xpad_ref[halo:halo + tb, :]
    for w in range(CONV_WIDTH - 1):
        off = halo - (CONV_WIDTH - 1) + w
        xc = xc + cw_ref[w:w + 1, :] * xpad_ref[off:off + tb, :]
    xpad_ref[0:halo, :] = xpad_ref[tb:tb + halo, :]

    reset = pos_ref[...] == 0
    lam = lam_ref[...]
    nl = -lam
    sp = jnp.maximum(nl, 0.0) + jnp.log(1.0 + jnp.exp(-jnp.abs(nl)))
    bd = LRU_BLOCK_DIM
    for n in range(LRU_BLOCKS):
        cols = slice(n * bd, (n + 1) * bd)
        xcn = xc[:, cols]
        xb = xcn.astype(BF16)
        gate_r = jax.nn.sigmoid(_dot(xb, wa_ref[n]) + ba_ref[:, cols])
        gate_i = jax.nn.sigmoid(_dot(xb, wx_ref[n]) + bx_ref[:, cols])
        log_a = (-LRU_C) * gate_r * sp[:, cols]
        a = jnp.exp(log_a)
        mult = jnp.sqrt(jnp.tanh(-log_a) * (a * a + 1.0))
        a = jnp.where(reset, 0.0, a)
        mult = jnp.where(reset, 1.0, mult)
        b = xcn * gate_i * mult
        s = 1
        while s < SUBLANES:
            a_ref[pad:pad + tb, cols] = a
            b_ref[pad:pad + tb, cols] = b
            a_sh = a_ref[pad - s:pad - s + tb, cols]
            b_sh = b_ref[pad - s:pad - s + tb, cols]
            b = a * b_sh + b
            a = a * a_sh
            s *= 2
        hg = hc_ref[:, cols]
        groups = []
        for g in range(tb // SUBLANES):
            rows = slice(g * SUBLANES, (g + 1) * SUBLANES)
            hg = a[rows, :] * hg + b[rows, :]
            groups.append(hg)
        h = jnp.concatenate(groups, axis=0)
        hc_ref[:, cols] = jnp.broadcast_to(hg[SUBLANES - 1:SUBLANES, :], (SUBLANES, bd))
        o_ref[:, cols] = (h * _gelu(y_ref[:, cols].astype(F32))).astype(o_ref.dtype)


def _rglru(z, pos_col, conv_w, conv_b, wa, ba, wx, bx, lam, tb):
    T = z.shape[0]
    W = D_MODEL
    row = lambda i: (0, 0)
    const3 = lambda i: (0, 0, 0)
    return pl.pallas_call(
        _rglru_kernel,
        grid=(T // tb,),
        in_specs=[pl.BlockSpec((tb, W), lambda i: (i, 3)),
                  pl.BlockSpec((tb, W), lambda i: (i, 4)),
                  pl.BlockSpec((tb, 1), lambda i: (i, 0)),
                  pl.BlockSpec((CONV_WIDTH, W), row),
                  pl.BlockSpec((1, W), row),
                  pl.BlockSpec((LRU_BLOCKS, LRU_BLOCK_DIM, LRU_BLOCK_DIM), const3),
                  pl.BlockSpec((1, W), row),
                  pl.BlockSpec((LRU_BLOCKS, LRU_BLOCK_DIM, LRU_BLOCK_DIM), const3),
                  pl.BlockSpec((1, W), row),
                  pl.BlockSpec((1, W), row)],
        out_specs=pl.BlockSpec((tb, W), lambda i: (i, 0)),
        out_shape=jax.ShapeDtypeStruct((T, W), BF16),
        scratch_shapes=[pltpu.VMEM((tb + 2 * SUBLANES, W), F32),
                        pltpu.VMEM((tb + SUBLANES, W), F32),
                        pltpu.VMEM((tb + SUBLANES, W), F32),
                        pltpu.VMEM((SUBLANES, W), F32)],
        compiler_params=_params("arbitrary"),
        name="rglru",
    )(z, z, pos_col, conv_w, conv_b, wa, ba, wx, bx, lam)


def _merge_kernel(x_ref, ret_ref, rnn_ref, ga_ref, gb_ref, wr_ref, wn_ref, wo_ref, o_ref, w_ref):
    @pl.when(pl.program_id(0) == 0)
    def _():
        for k, src in enumerate((wr_ref, wn_ref, wo_ref)):
            w_ref[k] = src[...].astype(BF16)

    p_ret = _dot(ret_ref[...], w_ref[0])
    p_rnn = _dot(rnn_ref[...], w_ref[1])
    merged = (jax.nn.sigmoid(ga_ref[...].astype(F32)) * p_ret
              + jax.nn.sigmoid(gb_ref[...].astype(F32)) * p_rnn)
    o_ref[...] = x_ref[...] + _dot(merged.astype(BF16), w_ref[2])


def _merge(x, ret, rnn, z, w_ret, w_rnn, w_out, layer, tm):
    T, D = x.shape
    tile = lambda i: (i, 0)
    wspec = pl.BlockSpec((None, D, D), lambda i: (layer, 0, 0))
    return pl.pallas_call(
        _merge_kernel,
        grid=(T // tm,),
        in_specs=[pl.BlockSpec((tm, D), tile), pl.BlockSpec((tm, D), tile), pl.BlockSpec((tm, D), tile),
                  pl.BlockSpec((tm, D), lambda i: (i, 5)),
                  pl.BlockSpec((tm, D), lambda i: (i, 6)),
                  wspec, wspec, wspec],
        out_specs=pl.BlockSpec((tm, D), tile),
        out_shape=jax.ShapeDtypeStruct((T, D), F32),
        scratch_shapes=[pltpu.VMEM((3, D, D), BF16)],
        compiler_params=_params("arbitrary"),
        name="merge",
    )(x, ret, rnn, z, z, w_ret, w_rnn, w_out)


def _xattn_kernel(x_ref, g_ref, wq_ref, kt_ref, v_ref, wo_ref, o_ref, att_ref, w_ref):
    @pl.when(pl.program_id(0) == 0)
    def _():
        w_ref[0] = wq_ref[...].astype(BF16)
        w_ref[1] = wo_ref[...].astype(BF16)

    x = x_ref[...]
    q = _dot(_rms(x, g_ref[...]).astype(BF16), w_ref[0])
    dh = XATTN_HEAD_DIM
    for h in range(XATTN_HEADS):
        cols = slice(h * dh, (h + 1) * dh)
        s = _dot(q[:, cols].astype(BF16), kt_ref[cols, :]) * (dh ** -0.5)
        s = s - jnp.max(s, axis=-1, keepdims=True)
        e = jnp.exp(s)
        p = e / jnp.sum(e, axis=-1, keepdims=True)
        att_ref[:, cols] = _dot(p.astype(BF16), v_ref[:, cols]).astype(BF16)
    o_ref[...] = x + _dot(att_ref[...], w_ref[1])


def _xattn(x, g, wq, kt, v, wo, layer, tm):
    T, D = x.shape
    M = v.shape[0]
    tile = lambda i: (i, 0)
    fixed = lambda i: (0, 0)
    wspec = pl.BlockSpec((None, D, D), lambda i: (layer, 0, 0))
    return pl.pallas_call(
        _xattn_kernel,
        grid=(T // tm,),
        in_specs=[pl.BlockSpec((tm, D), tile), pl.BlockSpec((1, D), fixed), wspec,
                  pl.BlockSpec((D, M), fixed), pl.BlockSpec((M, D), fixed), wspec],
        out_specs=pl.BlockSpec((tm, D), tile),
        out_shape=jax.ShapeDtypeStruct((T, D), F32),
        scratch_shapes=[pltpu.VMEM((tm, D), BF16), pltpu.VMEM((2, D, D), BF16)],
        compiler_params=_params("arbitrary"),
        name="xattn",
    )(x, g, wq, kt, v, wo)


def _peer_scores_kernel(x_ref, g_ref, wq_ref, k1_ref, k2_ref, hft_ref, s1_ref, s2_ref):
    hf = _rms(x_ref[...], g_ref[...])
    hft = jnp.transpose(hf).astype(BF16)
    hft_ref[...] = hft
    qt = _dot(wq_ref[...], hft)
    kd = 2 * PEER_HALF
    for h in range(PEER_HEADS):
        q1 = qt[h * kd:h * kd + PEER_HALF, :].astype(BF16)
        q2 = qt[h * kd + PEER_HALF:(h + 1) * kd, :].astype(BF16)
        s1_ref[h] = _dot(k1_ref[h], q1)
        s2_ref[h] = _dot(k2_ref[h], q2)


def _peer_scores(x, g, wq_t, k1, k2, tm):
    T, D = x.shape
    H, NK = PEER_HEADS, PEER_KEYS
    fixed = lambda i: (0, 0)
    const3 = lambda i: (0, 0, 0)
    return pl.pallas_call(
        _peer_scores_kernel,
        grid=(T // tm,),
        in_specs=[pl.BlockSpec((tm, D), lambda i: (i, 0)), pl.BlockSpec((1, D), fixed),
                  pl.BlockSpec(wq_t.shape, fixed),
                  pl.BlockSpec((H, NK, PEER_HALF), const3), pl.BlockSpec((H, NK, PEER_HALF), const3)],
        out_specs=[pl.BlockSpec((D, tm), lambda i: (0, i)),
                   pl.BlockSpec((H, NK, tm), lambda i: (0, 0, i)),
                   pl.BlockSpec((H, NK, tm), lambda i: (0, 0, i))],
        out_shape=[jax.ShapeDtypeStruct((D, T), BF16),
                   jax.ShapeDtypeStruct((H, NK, T), F32),
                   jax.ShapeDtypeStruct((H, NK, T), F32)],
        compiler_params=_params("parallel"),
        name="peer_scores",
    )(x, g, wq_t, k1, k2)


def _sort_pairs(n):
    pairs = []
    p = 1
    while p < n:
        k = p
        while k >= 1:
            for j in range(k % p, n - k, 2 * k):
                for i in range(min(k, n - j - k)):
                    if (i + j) // (2 * p) == (i + j + k) // (2 * p):
                        pairs.append((i + j, i + j + k))
            k //= 2
        p *= 2
    return pairs


def _sort_desc(vs):
    vs = list(vs)
    for lo, hi in _sort_pairs(len(vs)):
        a, b = vs[lo], vs[hi]
        vs[lo], vs[hi] = jnp.maximum(a, b), jnp.minimum(a, b)
    return vs


def _merge_top(a, b):
    n = len(a)
    c = [jnp.maximum(a[i], b[n - 1 - i]) for i in range(n)]
    k = n // 2
    while k >= 1:
        for i in range(n):
            if i & k == 0:
                x, y = c[i], c[i + k]
                c[i], c[i + k] = jnp.maximum(x, y), jnp.minimum(x, y)
        k //= 2
    return c


def _top_of_keys(groups, roll):
    vs = _sort_desc(groups)
    shift = SUBLANES // 2
    while shift >= 1:
        vs = _merge_top(vs, [roll(v, shift, 0) for v in vs])
        shift //= 2
    return vs


def _peer_select_math(s1, s2, roll):
    K = PEER_TOPK
    shape = s1[0][0].shape
    sub = lax.broadcasted_iota(jnp.int32, shape, 0)
    zero = jnp.zeros(shape, F32)
    tops2 = []
    v1 = [zero] * K
    v2 = [zero] * K
    for h in range(PEER_HEADS):
        t1 = _top_of_keys(s1[h], roll)
        t2 = _top_of_keys(s2[h], roll)
        tops2.append(t2)
        v1 = [jnp.where(sub == h, t, v) for t, v in zip(t1, v1)]
        v2 = [jnp.where(sub == h, t, v) for t, v in zip(t2, v2)]
    cands = [v1[i] + v2[j] for i in range(K) for j in range(K // (i + 1))]
    n = 1
    while n < len(cands):
        n *= 2
    cands = cands + [jnp.full(shape, -jnp.inf, F32)] * (n - len(cands))
    top = _sort_desc(cands)[:K]
    z = zero
    for t in top:
        z = z + jnp.exp(t - top[0])
    tau_all, zinv_all = top[K - 1], 1.0 / z
    out = []
    for h in range(PEER_HEADS):
        bc = lambda v: jnp.broadcast_to(v[h:h + 1, :], shape)
        tau, zinv, m1, m2 = bc(tau_all), bc(zinv_all), bc(v1[0]), bc(v2[0])
        t2 = tops2[h]
        cnt, p1, rk, e2 = [], [], [], []
        c_top = zero
        for j in range(K // 2, K):
            c_top = jnp.where(m1 + t2[j] >= tau, float(j + 1), c_top)
        for s1g, s2g in zip(s1[h], s2[h]):
            c = zero
            r = zero
            for j in range(K // 2):
                c = jnp.where(s1g + t2[j] >= tau, float(j + 1), c)
            c = jnp.where(s1g == m1, jnp.maximum(c, c_top), c)
            for j in range(K):
                r = jnp.where(t2[j] > s2g, float(j + 1), r)
            cnt.append(c)
            rk.append(r)
            p1.append(jnp.exp(s1g - m1) * zinv)
            e2.append(jnp.exp(s2g - m2))
        out.append((cnt, p1, rk, e2))
    return out


def _bf16_pair(x):
    u = lax.bitcast_convert_type(x.astype(BF16).astype(F32), jnp.uint32)
    return u | (u >> 16)


def _peer_select_kernel(s1_ref, s2_ref, n_ref, p1_ref, r2_ref, e2_ref):
    ng = PEER_KEYS // SUBLANES
    grp = lambda ref, h: [ref[h, i * SUBLANES:(i + 1) * SUBLANES, :] for i in range(ng)]
    s1 = [grp(s1_ref, h) for h in range(PEER_HEADS)]
    s2 = [grp(s2_ref, h) for h in range(PEER_HEADS)]
    res = _peer_select_math(s1, s2, pltpu.roll)
    pk = 2 * SUBLANES
    for h, (cnt, p1, rk, e2) in enumerate(res):
        for i in range(ng):
            rows = slice(i * SUBLANES, (i + 1) * SUBLANES)
            n_ref[h, rows, :] = _bf16_pair(cnt[i])
            p1_ref[h, rows, :] = _bf16_pair(p1[i])
        for i in range(ng // 2):
            rows = slice(i * pk, (i + 1) * pk)
            r2_ref[h, rows, :] = jnp.concatenate([rk[2 * i], rk[2 * i + 1]], axis=0).astype(BF16)
            e2_ref[h, rows, :] = jnp.concatenate([e2[2 * i], e2[2 * i + 1]], axis=0).astype(BF16)


def _peer_select(s1, s2, tl):
    H, NK, T = s1.shape
    spec = pl.BlockSpec((H, NK, tl), lambda i: (0, 0, i))
    return pl.pallas_call(
        _peer_select_kernel,
        grid=(T // tl,),
        in_specs=[spec] * 2,
        out_specs=[spec] * 4,
        out_shape=[jax.ShapeDtypeStruct((H, NK, T), jnp.uint32), jax.ShapeDtypeStruct((H, NK, T), jnp.uint32),
                   jax.ShapeDtypeStruct((H, NK, T), BF16), jax.ShapeDtypeStruct((H, NK, T), BF16)],
        compiler_params=_params("parallel"),
        name="peer_select",
    )(s1, s2)


def _peer_gate_rows(act, a, n_ref, p1_ref, r2_ref, e2_ref, w_ref, row0):
    NK = PEER_KEYS
    tm = act.shape[-1]
    pk = 2 * SUBLANES
    zero = jnp.zeros((pk, tm), BF16)
    bc = lambda ref, h: pltpu.bitcast(jnp.broadcast_to(ref[h, pl.ds(a, 1), :], (SUBLANES, tm)), BF16)
    nrow = [bc(n_ref, h) for h in range(PEER_HEADS)]
    prow = [bc(p1_ref, h) for h in range(PEER_HEADS)]
    for g in range(NK // pk):
        b_rows = slice(g * pk, (g + 1) * pk)
        gate = zero
        for h in range(PEER_HEADS):
            sel = r2_ref[h, b_rows, :] < nrow[h]
            gate = gate + jnp.where(sel, prow[h] * e2_ref[h, b_rows, :], zero)
        w_ref[row0 + g * pk:row0 + (g + 1) * pk, :] = gate * act[b_rows, :]


def _peer_half(a0, hft, keys, u_ref, w_ref, vt_prev_ref, w_prev_ref, acc_ref, gl_ref, na):
    NK = PEER_KEYS
    dr = acc_ref.shape[0] // na
    w_prev = w_prev_ref[...]
    for ai in range(na):
        if ai % 2 == 0:
            gl2 = _gelu(_dot(u_ref[ai * NK:(ai + 2) * NK, :], hft).astype(BF16))
            gl_ref[ai] = gl2[:NK]
            gl_ref[ai + 1] = gl2[NK:]
        if ai % 2 == 0:
            rows = slice(ai * dr, (ai + 2) * dr)
            acc_ref[rows, :] += _dot(vt_prev_ref[rows, :], w_prev)
        _peer_gate_rows(gl_ref.at[ai], a0 + ai, *keys, w_ref, ai * NK)


def _peer_dense_kernel(x_ref, hft_ref, n_ref, p1_ref, r2_ref, e2_ref, ua_ref, ub_ref, vta_ref, vtb_ref,
                       vtl_ref, *rest, na, norm_out):
    g_ref = rest[0] if norm_out else None
    o_ref, acc_ref, wa_ref, wb_ref, gla_ref, glb_ref = rest[-6:]
    j = pl.program_id(1)
    last = pl.num_programs(1) - 1
    keys = (n_ref, p1_ref, r2_ref, e2_ref)

    @pl.when(j == 0)
    def _():
        acc_ref[...] = jnp.zeros_like(acc_ref)
        wb_ref[...] = jnp.zeros_like(wb_ref)

    hft = hft_ref[...]
    _peer_half(2 * j * na, hft, keys, ua_ref, wa_ref, vtb_ref, wb_ref, acc_ref, gla_ref, na)
    _peer_half((2 * j + 1) * na, hft, keys, ub_ref, wb_ref, vta_ref, wa_ref, acc_ref, glb_ref, na)

    @pl.when(j == last)
    def _():
        acc = acc_ref[...] + _dot(vtl_ref[...], wb_ref[...])
        xo = x_ref[...] + jnp.transpose(acc)
        o_ref[...] = _rms(xo, g_ref[...]) if norm_out else xo


def _peer_dense(x, hft, n1, p1, r2, e2, u, vt, layer, tm, na, g_out=None):
    T, D = x.shape
    H, NK = PEER_HEADS, PEER_KEYS
    th = na * NK
    steps = NK // (2 * na)
    kspec = pl.BlockSpec((H, NK, tm), lambda i, j: (0, 0, i))
    return pl.pallas_call(
        functools.partial(_peer_dense_kernel, na=na, norm_out=g_out is not None),
        grid=(T // tm, steps),
        in_specs=[pl.BlockSpec((tm, D), lambda i, j: (i, 0)),
                  pl.BlockSpec((D, tm), lambda i, j: (0, i)),
                  kspec, kspec, kspec, kspec,
                  pl.BlockSpec((None, th, D), lambda i, j: (layer, 2 * j, 0)),
                  pl.BlockSpec((None, th, D), lambda i, j: (layer, 2 * j + 1, 0)),
                  pl.BlockSpec((None, None, D, th), lambda i, j: (layer, 2 * j, 0, 0)),
                  pl.BlockSpec((None, None, D, th), lambda i, j: (layer, jnp.maximum(2 * j - 1, 1), 0, 0)),
                  pl.BlockSpec((None, None, D, th), lambda i, j: (layer, 2 * steps - 1, 0, 0))]
                 + ([pl.BlockSpec((1, D), lambda i, j: (0, 0))] if g_out is not None else []),
        out_specs=pl.BlockSpec((tm, D), lambda i, j: (i, 0)),
        out_shape=jax.ShapeDtypeStruct((T, D), F32),
        scratch_shapes=[pltpu.VMEM((D, tm), F32), pltpu.VMEM((th, tm), BF16), pltpu.VMEM((th, tm), BF16),
                        pltpu.VMEM((na, NK, tm), BF16), pltpu.VMEM((na, NK, tm), BF16)],
        compiler_params=_params("parallel", "arbitrary"),
        name="peer_dense",
    )(x, hft, n1, p1, r2, e2, u, u, vt, vt, vt, *(() if g_out is None else (g_out,)))


def kernel(x, mem, positions, g_mix, w_in, w_ret_br, w_rnn_br, w_mix_out, conv_w, conv_b, lru_wa, lru_ba, lru_wx, lru_bx, lru_lam, g_x, g_mem, w_xq, w_xk, w_xv, w_xo, g_ffn, w_pq, sub_k1, sub_k2, peer_u, peer_v, g_final):
    B, T, D = x.shape
    assert B == 1 and D == D_MODEL
    depth = g_mix.shape[0]
    tm = min(TM_PROJ, T)
    xs = x.reshape(T, D)
    mems = mem.reshape(mem.shape[1], D)
    pos_col = positions.reshape(T, 1)
    row = lambda a: a.reshape(1, -1)

    cos2, sin2 = _rope_tables(pos_col, tm)
    u_all = peer_u.astype(BF16)
    th = NA_PEER * PEER_KEYS
    vt_all = jnp.swapaxes(peer_v.reshape(depth, -1, th, D), 2, 3).astype(BF16)
    for l in range(depth):
        z = _rms_matmul(xs, row(g_mix[l]), w_in, l, tm=min(TM_IN, T), tn=TN_PROJ, out_dtype=BF16)
        ret = _retention(z, cos2, sin2, min(TB_RET, T))
        rnn = _rglru(z, pos_col, conv_w[l], row(conv_b[l]), lru_wa[l].astype(BF16), row(lru_ba[l]),
                     lru_wx[l].astype(BF16), row(lru_bx[l]), row(lru_lam[l]), min(TB_LRU, T))
        xs = _merge(xs, ret, rnn, z, w_ret_br, w_rnn_br, w_mix_out, l, tm)
        kmem = _rms_matmul(mems, row(g_mem[l]), w_xk, l, tm=mems.shape[0], tn=D, out_dtype=BF16)
        vmem = _rms_matmul(mems, row(g_mem[l]), w_xv, l, tm=mems.shape[0], tn=D, out_dtype=BF16)
        xs = _xattn(xs, row(g_x[l]), w_xq, kmem.T, vmem, w_xo, l, tm)
        tp = min(TM_PEER, T)
        hft, s1, s2 = _peer_scores(xs, row(g_ffn[l]), w_pq[l].T.astype(BF16), sub_k1[l].astype(BF16),
                                   sub_k2[l].astype(BF16), tp)
        n1, p1, r2, e2 = _peer_select(s1, s2, TL_SORT)
        xs = _peer_dense(xs, hft, n1, p1, r2, e2, u_all, vt_all, l, tp, NA_PEER,
                         g_out=row(g_final) if l == depth - 1 else None)
    return xs.reshape(B, T, D)
```

```python
import functools
import math

import numpy as np
import jax
import jax.numpy as jnp
from jax import lax
from jax.experimental import pallas as pl
from jax.experimental.pallas import tpu as pltpu

F32 = jnp.float32
BF16 = jnp.bfloat16

D_MODEL = 1024
EPS = 1e-6
RET_HEADS = 4
RET_QK_DIM = 128
RET_V_DIM = 256
RET_CHUNK = 128
ROPE_BASE = 10000.0
LRU_BLOCKS = 8
LRU_BLOCK_DIM = D_MODEL // LRU_BLOCKS
CONV_WIDTH = 4
LRU_C = 8.0
XATTN_HEADS = 4
XATTN_HEAD_DIM = D_MODEL // XATTN_HEADS
PEER_HEADS = 8
PEER_KEYS = 128
PEER_TOPK = 16
PEER_HALF = 128

LANES = 128
SUBLANES = 8
VMEM_LIMIT_BYTES = 56 * 1024 * 1024

TM_PROJ = 512
TM_IN = 2048
TN_PROJ = 1024
TB_RET = 256
TB_LRU = 512
TM_PEER = 512
NA_PEER = 4
TL_SORT = 256


def _params(*sem):
    return pltpu.CompilerParams(dimension_semantics=sem, vmem_limit_bytes=VMEM_LIMIT_BYTES)


def _rms(x, g):
    return x * lax.rsqrt(jnp.mean(x * x, axis=-1, keepdims=True) + EPS) * g


def _gelu(x):
    c = math.sqrt(2.0 / math.pi)
    return 0.5 * x * (1.0 + jnp.tanh(c * (x + 0.044715 * (x * x * x))))


def _dot(a, b):
    return jnp.dot(a, b, preferred_element_type=F32)


def _dot_nt(a, b):
    return lax.dot_general(a, b, (((1,), (1,)), ((), ())), preferred_element_type=F32)


def _rms_matmul_kernel(x_ref, g_ref, w_ref, o_ref, h_ref):
    @pl.when(pl.program_id(1) == 0)
    def _():
        h_ref[...] = _rms(x_ref[...], g_ref[...]).astype(BF16)

    o_ref[...] = _dot(h_ref[...], w_ref[...].astype(BF16)).astype(o_ref.dtype)


def _rms_matmul(x, g, w, layer, *, tm, tn, out_dtype):
    T, K = x.shape
    N = w.shape[2]
    return pl.pallas_call(
        _rms_matmul_kernel,
        grid=(T // tm, N // tn),
        in_specs=[pl.BlockSpec((tm, K), lambda i, j: (i, 0)),
                  pl.BlockSpec((1, K), lambda i, j: (0, 0)),
                  pl.BlockSpec((None, K, tn), lambda i, j: (layer, 0, j))],
        out_specs=pl.BlockSpec((tm, tn), lambda i, j: (i, j)),
        out_shape=jax.ShapeDtypeStruct((T, N), out_dtype),
        scratch_shapes=[pltpu.VMEM((tm, K), BF16)],
        compiler_params=_params("parallel", "arbitrary"),
        name="rms_matmul",
    )(x, g, w)


def _rope_kernel(pos_ref, invf_ref, cos_ref, sin_ref):
    ang = pos_ref[...].astype(F32) * invf_ref[...]
    lane = lax.broadcasted_iota(jnp.int32, ang.shape, 1)
    s = jnp.sin(ang)
    cos_ref[...] = jnp.cos(ang)
    sin_ref[...] = jnp.where(lane < RET_QK_DIM // 2, -s, s)


def _rope_tables(pos_col, tb):
    T = pos_col.shape[0]
    half = RET_QK_DIM // 2
    inv_freq = ROPE_BASE ** (-jnp.arange(0, RET_QK_DIM, 2, dtype=F32) / RET_QK_DIM)
    invf2 = jnp.concatenate([inv_freq, inv_freq]).reshape(1, 2 * half)
    return pl.pallas_call(
        _rope_kernel,
        grid=(T // tb,),
        in_specs=[pl.BlockSpec((tb, 1), lambda i: (i, 0)),
                  pl.BlockSpec((1, RET_QK_DIM), lambda i: (0, 0))],
        out_specs=[pl.BlockSpec((tb, RET_QK_DIM), lambda i: (i, 0))] * 2,
        out_shape=[jax.ShapeDtypeStruct((T, RET_QK_DIM), F32)] * 2,
        compiler_params=_params("parallel"),
        name="rope_tables",
    )(pos_col, invf2)


def _retention_kernel(q_ref, k_ref, v_ref, g_ref, cos_ref, sin_ref, dec_ref, kdec_ref, qdec_ref,
                      o_ref, s_ref, *, chunk_decay):
    C, dk, dv = RET_CHUNK, RET_QK_DIM, RET_V_DIM

    @pl.when(pl.program_id(0) == 0)
    def _():
        s_ref[...] = jnp.zeros_like(s_ref)

    kscale = dk ** -0.5
    for c in range(q_ref.shape[0] // C):
        rows = slice(c * C, (c + 1) * C)
        cos = cos_ref[rows, :]
        sin = sin_ref[rows, :]
        for h in range(RET_HEADS):
            q = q_ref[rows, h * dk:(h + 1) * dk].astype(F32)
            k = k_ref[rows, h * dk:(h + 1) * dk].astype(F32)
            v = v_ref[rows, h * dv:(h + 1) * dv].astype(BF16)
            qr = q * cos + pltpu.roll(q, dk // 2, 1) * sin
            kr = (k * cos + pltpu.roll(k, dk // 2, 1) * sin) * kscale
            scores = _dot_nt(qr.astype(BF16), kr.astype(BF16)) * dec_ref[h]
            inner = _dot(scores.astype(BF16), v)
            state = s_ref[h]
            cross = _dot((qr * qdec_ref[h]).astype(BF16), state.astype(BF16))
            kd_t = jnp.transpose(kr * kdec_ref[h]).astype(BF16)
            s_ref[h] = chunk_decay[h] * state + _dot(kd_t, v)
            o = inner + cross
            mu = jnp.mean(o, axis=-1, keepdims=True)
            oc = o - mu
            var = jnp.mean(oc * oc, axis=-1, keepdims=True)
            o = oc * lax.rsqrt(var + EPS)
            g = g_ref[rows, h * dv:(h + 1) * dv].astype(F32)
            o_ref[rows, h * dv:(h + 1) * dv] = (g * jax.nn.sigmoid(g) * o).astype(o_ref.dtype)


def _retention(z, cos2, sin2, tb):
    T = z.shape[0]
    H, C, dk, dv = RET_HEADS, RET_CHUNK, RET_QK_DIM, RET_V_DIM
    log_g = jnp.log(1.0 - 2.0 ** (-5.0 - jnp.arange(H, dtype=F32)))
    i = jnp.arange(C, dtype=F32)
    diff = i[:, None] - i[None, :]
    decay = jnp.where(diff[None] >= 0, jnp.exp(diff[None] * log_g[:, None, None]), 0.0)
    k_decay = jnp.exp((C - 1.0 - i)[None, :] * log_g[:, None])
    q_decay = jnp.exp((i + 1.0)[None, :] * log_g[:, None])
    kdec = jnp.broadcast_to(k_decay[:, :, None], (H, C, dk))
    qdec = jnp.broadcast_to(q_decay[:, :, None], (H, C, dk))
    log_g_np = np.log(1.0 - 2.0 ** (-5.0 - np.arange(H, dtype=np.float32))).astype(np.float32)
    chunk_decay = tuple(float(np.exp(np.float32(C) * lg)) for lg in log_g_np)
    nq = H * dk
    const3 = lambda i: (0, 0, 0)
    return pl.pallas_call(
        functools.partial(_retention_kernel, chunk_decay=chunk_decay),
        grid=(T // tb,),
        in_specs=[pl.BlockSpec((tb, nq), lambda i: (i, 0)),
                  pl.BlockSpec((tb, nq), lambda i: (i, 1)),
                  pl.BlockSpec((tb, H * dv), lambda i: (i, 1)),
                  pl.BlockSpec((tb, H * dv), lambda i: (i, 2)),
                  pl.BlockSpec((tb, dk), lambda i: (i, 0)),
                  pl.BlockSpec((tb, dk), lambda i: (i, 0)),
                  pl.BlockSpec((H, C, C), const3),
                  pl.BlockSpec((H, C, dk), const3),
                  pl.BlockSpec((H, C, dk), const3)],
        out_specs=pl.BlockSpec((tb, H * dv), lambda i: (i, 0)),
        out_shape=jax.ShapeDtypeStruct((T, H * dv), BF16),
        scratch_shapes=[pltpu.VMEM((H, dk, dv), F32)],
        compiler_params=_params("arbitrary"),
        name="retention",
    )(z, z, z, z, cos2, sin2, decay, kdec, qdec)


def _rglru_kernel(x_ref, y_ref, pos_ref, cw_ref, cb_ref, wa_ref, ba_ref, wx_ref, bx_ref, lam_ref,
                  o_ref, xpad_ref, a_ref, b_ref, hc_ref):
    tb = x_ref.shape[0]
    pad = SUBLANES
    halo = SUBLANES

    @pl.when(pl.program_id(0) == 0)
    def _():
        xpad_ref[0:halo, :] = jnp.zeros((halo, D_MODEL), F32)
        hc_ref[...] = jnp.zeros_like(hc_ref)
        a_ref[0:pad, :] = jnp.ones((pad, D_MODEL), F32)
        b_ref[0:pad, :] = jnp.zeros((pad, D_MODEL), F32)

    xpad_ref[halo:halo + tb, :] = x_ref[...].astype(F32)
    xc = cb_ref[...] + cw_ref[3:4, :] * xpad_ref[halo:halo + tb, :]
    for w in range(CONV_WIDTH - 1):
        off = halo - (CONV_WIDTH - 1) + w
        xc = xc + cw_ref[w:w + 1, :] * xpad_ref[off:off + tb, :]
    xpad_ref[0:halo, :] = xpad_ref[tb:tb + halo, :]

    reset = pos_ref[...] == 0
    lam = lam_ref[...]
    nl = -lam
    sp = jnp.maximum(nl, 0.0) + jnp.log(1.0 + jnp.exp(-jnp.abs(nl)))
    bd = LRU_BLOCK_DIM
    for n in range(LRU_BLOCKS):
        cols = slice(n * bd, (n + 1) * bd)
        xcn = xc[:, cols]
        xb = xcn.astype(BF16)
        gate_r = jax.nn.sigmoid(_dot(xb, wa_ref[n]) + ba_ref[:, cols])
        gate_i = jax.nn.sigmoid(_dot(xb, wx_ref[n]) + bx_ref[:, cols])
        log_a = (-LRU_C) * gate_r * sp[:, cols]
        a = jnp.exp(log_a)
        mult = jnp.sqrt(jnp.tanh(-log_a) * (a * a + 1.0))
        a = jnp.where(reset, 0.0, a)
        mult = jnp.where(reset, 1.0, mult)
        b = xcn * gate_i * mult
        s = 1
        while s < SUBLANES:
            a_ref[pad:pad + tb, cols] = a
            b_ref[pad:pad + tb, cols] = b
            a_sh = a_ref[pad - s:pad - s + tb, cols]
            b_sh = b_ref[pad - s:pad - s + tb, cols]
            b = a * b_sh + b
            a = a * a_sh
            s *= 2
        hg = hc_ref[:, cols]
        groups = []
        for g in range(tb // SUBLANES):
            rows = slice(g * SUBLANES, (g + 1) * SUBLANES)
            hg = a[rows, :] * hg + b[rows, :]
            groups.append(hg)
        h = jnp.concatenate(groups, axis=0)
        hc_ref[:, cols] = jnp.broadcast_to(hg[SUBLANES - 1:SUBLANES, :], (SUBLANES, bd))
        o_ref[:, cols] = (h * _gelu(y_ref[:, cols].astype(F32))).astype(o_ref.dtype)


def _rglru(z, pos_col, conv_w, conv_b, wa, ba, wx, bx, lam, tb):
    T = z.shape[0]
    W = D_MODEL
    row = lambda i: (0, 0)
    const3 = lambda i: (0, 0, 0)
    return pl.pallas_call(
        _rglru_kernel,
        grid=(T // tb,),
        in_specs=[pl.BlockSpec((tb, W), lambda i: (i, 3)),
                  pl.BlockSpec((tb, W), lambda i: (i, 4)),
                  pl.BlockSpec((tb, 1), lambda i: (i, 0)),
                  pl.BlockSpec((CONV_WIDTH, W), row),
                  pl.BlockSpec((1, W), row),
                  pl.BlockSpec((LRU_BLOCKS, LRU_BLOCK_DIM, LRU_BLOCK_DIM), const3),
                  pl.BlockSpec((1, W), row),
                  pl.BlockSpec((LRU_BLOCKS, LRU_BLOCK_DIM, LRU_BLOCK_DIM), const3),
                  pl.BlockSpec((1, W), row),
                  pl.BlockSpec((1, W), row)],
        out_specs=pl.BlockSpec((tb, W), lambda i: (i, 0)),
        out_shape=jax.ShapeDtypeStruct((T, W), BF16),
        scratch_shapes=[pltpu.VMEM((tb + 2 * SUBLANES, W), F32),
                        pltpu.VMEM((tb + SUBLANES, W), F32),
                        pltpu.VMEM((tb + SUBLANES, W), F32),
                        pltpu.VMEM((SUBLANES, W), F32)],
        compiler_params=_params("arbitrary"),
        name="rglru",
    )(z, z, pos_col, conv_w, conv_b, wa, ba, wx, bx, lam)


def _merge_kernel(x_ref, ret_ref, rnn_ref, ga_ref, gb_ref, wr_ref, wn_ref, wo_ref, o_ref, w_ref):
    @pl.when(pl.program_id(0) == 0)
    def _():
        for k, src in enumerate((wr_ref, wn_ref, wo_ref)):
            w_ref[k] = src[...].astype(BF16)

    p_ret = _dot(ret_ref[...], w_ref[0])
    p_rnn = _dot(rnn_ref[...], w_ref[1])
    merged = (jax.nn.sigmoid(ga_ref[...].astype(F32)) * p_ret
              + jax.nn.sigmoid(gb_ref[...].astype(F32)) * p_rnn)
    o_ref[...] = x_ref[...] + _dot(merged.astype(BF16), w_ref[2])


def _merge(x, ret, rnn, z, w_ret, w_rnn, w_out, layer, tm):
    T, D = x.shape
    tile = lambda i: (i, 0)
    wspec = pl.BlockSpec((None, D, D), lambda i: (layer, 0, 0))
    return pl.pallas_call(
        _merge_kernel,
        grid=(T // tm,),
        in_specs=[pl.BlockSpec((tm, D), tile), pl.BlockSpec((tm, D), tile), pl.BlockSpec((tm, D), tile),
                  pl.BlockSpec((tm, D), lambda i: (i, 5)),
                  pl.BlockSpec((tm, D), lambda i: (i, 6)),
                  wspec, wspec, wspec],
        out_specs=pl.BlockSpec((tm, D), tile),
        out_shape=jax.ShapeDtypeStruct((T, D), F32),
        scratch_shapes=[pltpu.VMEM((3, D, D), BF16)],
        compiler_params=_params("arbitrary"),
        name="merge",
    )(x, ret, rnn, z, z, w_ret, w_rnn, w_out)


def _xattn_kernel(x_ref, g_ref, wq_ref, kt_ref, v_ref, wo_ref, o_ref, att_ref, w_ref):
    @pl.when(pl.program_id(0) == 0)
    def _():
        w_ref[0] = wq_ref[...].astype(BF16)
        w_ref[1] = wo_ref[...].astype(BF16)

    x = x_ref[...]
    q = _dot(_rms(x, g_ref[...]).astype(BF16), w_ref[0])
    dh = XATTN_HEAD_DIM
    for h in range(XATTN_HEADS):
        cols = slice(h * dh, (h + 1) * dh)
        s = _dot(q[:, cols].astype(BF16), kt_ref[cols, :]) * (dh ** -0.5)
        s = s - jnp.max(s, axis=-1, keepdims=True)
        e = jnp.exp(s)
        p = e / jnp.sum(e, axis=-1, keepdims=True)
        att_ref[:, cols] = _dot(p.astype(BF16), v_ref[:, cols]).astype(BF16)
    o_ref[...] = x + _dot(att_ref[...], w_ref[1])


def _xattn(x, g, wq, kt, v, wo, layer, tm):
    T, D = x.shape
    M = v.shape[0]
    tile = lambda i: (i, 0)
    fixed = lambda i: (0, 0)
    wspec = pl.BlockSpec((None, D, D), lambda i: (layer, 0, 0))
    return pl.pallas_call(
        _xattn_kernel,
        grid=(T // tm,),
        in_specs=[pl.BlockSpec((tm, D), tile), pl.BlockSpec((1, D), fixed), wspec,
                  pl.BlockSpec((D, M), fixed), pl.BlockSpec((M, D), fixed), wspec],
        out_specs=pl.BlockSpec((tm, D), tile),
        out_shape=jax.ShapeDtypeStruct((T, D), F32),
        scratch_shapes=[pltpu.VMEM((tm, D), BF16), pltpu.VMEM((2, D, D), BF16)],
        compiler_params=_params("arbitrary"),
        name="xattn",
    )(x, g, wq, kt, v, wo)


def _peer_scores_kernel(x_ref, g_ref, wq_ref, k1_ref, k2_ref, hft_ref, s1_ref, s2_ref):
    hf = _rms(x_ref[...], g_ref[...])
    hft = jnp.transpose(hf).astype(BF16)
    hft_ref[...] = hft
    qt = _dot(wq_ref[...], hft)
    kd = 2 * PEER_HALF
    for h in range(PEER_HEADS):
        q1 = qt[h * kd:h * kd + PEER_HALF, :].astype(BF16)
        q2 = qt[h * kd + PEER_HALF:(h + 1) * kd, :].astype(BF16)
        s1_ref[h] = _dot(k1_ref[h], q1)
        s2_ref[h] = _dot(k2_ref[h], q2)


def _peer_scores(x, g, wq_t, k1, k2, tm):
    T, D = x.shape
    H, NK = PEER_HEADS, PEER_KEYS
    fixed = lambda i: (0, 0)
    const3 = lambda i: (0, 0, 0)
    return pl.pallas_call(
        _peer_scores_kernel,
        grid=(T // tm,),
        in_specs=[pl.BlockSpec((tm, D), lambda i: (i, 0)), pl.BlockSpec((1, D), fixed),
                  pl.BlockSpec(wq_t.shape, fixed),
                  pl.BlockSpec((H, NK, PEER_HALF), const3), pl.BlockSpec((H, NK, PEER_HALF), const3)],
        out_specs=[pl.BlockSpec((D, tm), lambda i: (0, i)),
                   pl.BlockSpec((H, NK, tm), lambda i: (0, 0, i)),
                   pl.BlockSpec((H, NK, tm), lambda i: (0, 0, i))],
        out_shape=[jax.ShapeDtypeStruct((D, T), BF16),
                   jax.ShapeDtypeStruct((H, NK, T), F32),
                   jax.ShapeDtypeStruct((H, NK, T), F32)],
        compiler_params=_params("parallel"),
        name="peer_scores",
    )(x, g, wq_t, k1, k2)


def _sort_pairs(n):
    pairs = []
    p = 1
    while p < n:
        k = p
        while k >= 1:
            for j in range(k % p, n - k, 2 * k):
                for i in range(min(k, n - j - k)):
                    if (i + j) // (2 * p) == (i + j + k) // (2 * p):
                        pairs.append((i + j, i + j + k))
            k //= 2
        p *= 2
    return pairs


def _sort_desc(vs):
    vs = list(vs)
    for lo, hi in _sort_pairs(len(vs)):
        a, b = vs[lo], vs[hi]
        vs[lo], vs[hi] = jnp.maximum(a, b), jnp.minimum(a, b)
    return vs


def _merge_top(a, b):
    n = len(a)
    c = [jnp.maximum(a[i], b[n - 1 - i]) for i in range(n)]
    k = n // 2
    while k >= 1:
        for i in range(n):
            if i & k == 0:
                x, y = c[i], c[i + k]
                c[i], c[i + k] = jnp.maximum(x, y), jnp.minimum(x, y)
        k //= 2
    return c


def _top_of_keys(groups, roll):
    vs = _sort_desc(groups)
    shift = SUBLANES // 2
    while shift >= 1:
        vs = _merge_top(vs, [roll(v, shift, 0) for v in vs])
        shift //= 2
    return vs


def _peer_select_math(s1, s2, roll):
    K = PEER_TOPK
    shape = s1[0][0].shape
    sub = lax.broadcasted_iota(jnp.int32, shape, 0)
    zero = jnp.zeros(shape, F32)
    tops2 = []
    v1 = [zero] * K
    v2 = [zero] * K
    for h in range(PEER_HEADS):
        t1 = _top_of_keys(s1[h], roll)
        t2 = _top_of_keys(s2[h], roll)
        tops2.append(t2)
        v1 = [jnp.where(sub == h, t, v) for t, v in zip(t1, v1)]
        v2 = [jnp.where(sub == h, t, v) for t, v in zip(t2, v2)]
    cands = [v1[i] + v2[j] for i in range(K) for j in range(K // (i + 1))]
    n = 1
    while n < len(cands):
        n *= 2
    cands = cands + [jnp.full(shape, -jnp.inf, F32)] * (n - len(cands))
    top = _sort_desc(cands)[:K]
    z = zero
    for t in top:
        z = z + jnp.exp(t - top[0])
    tau_all, zinv_all = top[K - 1], 1.0 / z
    out = []
    for h in range(PEER_HEADS):
        bc = lambda v: jnp.broadcast_to(v[h:h + 1, :], shape)
        tau, zinv, m1, m2 = bc(tau_all), bc(zinv_all), bc(v1[0]), bc(v2[0])
        t2 = tops2[h]
        cnt, p1, rk, e2 = [], [], [], []
        c_top = zero
        for j in range(K // 2, K):
            c_top = jnp.where(m1 + t2[j] >= tau, float(j + 1), c_top)
        for s1g, s2g in zip(s1[h], s2[h]):
            c = zero
            r = zero
            for j in range(K // 2):
                c = jnp.where(s1g + t2[j] >= tau, float(j + 1), c)
            c = jnp.where(s1g == m1, jnp.maximum(c, c_top), c)
            for j in range(K):
                r = jnp.where(t2[j] > s2g, float(j + 1), r)
            cnt.append(c)
            rk.append(r)
            p1.append(jnp.exp(s1g - m1) * zinv)
            e2.append(jnp.exp(s2g - m2))
        out.append((cnt, p1, rk, e2))
    return out


def _bf16_pair(x):
    u = lax.bitcast_convert_type(x.astype(BF16).astype(F32), jnp.uint32)
    return u | (u >> 16)


def _peer_select_kernel(s1_ref, s2_ref, n_ref, p1_ref, r2_ref, e2_ref):
    ng = PEER_KEYS // SUBLANES
    grp = lambda ref, h: [ref[h, i * SUBLANES:(i + 1) * SUBLANES, :] for i in range(ng)]
    s1 = [grp(s1_ref, h) for h in range(PEER_HEADS)]
    s2 = [grp(s2_ref, h) for h in range(PEER_HEADS)]
    res = _peer_select_math(s1, s2, pltpu.roll)
    pk = 2 * SUBLANES
    for h, (cnt, p1, rk, e2) in enumerate(res):
        for i in range(ng):
            rows = slice(i * SUBLANES, (i + 1) * SUBLANES)
            n_ref[h, rows, :] = _bf16_pair(cnt[i])
            p1_ref[h, rows, :] = _bf16_pair(p1[i])
        for i in range(ng // 2):
            rows = slice(i * pk, (i + 1) * pk)
            r2_ref[h, rows, :] = jnp.concatenate([rk[2 * i], rk[2 * i + 1]], axis=0).astype(BF16)
            e2_ref[h, rows, :] = jnp.concatenate([e2[2 * i], e2[2 * i + 1]], axis=0).astype(BF16)


def _peer_select(s1, s2, tl):
    H, NK, T = s1.shape
    spec = pl.BlockSpec((H, NK, tl), lambda i: (0, 0, i))
    return pl.pallas_call(
        _peer_select_kernel,
        grid=(T // tl,),
        in_specs=[spec] * 2,
        out_specs=[spec] * 4,
        out_shape=[jax.ShapeDtypeStruct((H, NK, T), jnp.uint32), jax.ShapeDtypeStruct((H, NK, T), jnp.uint32),
                   jax.ShapeDtypeStruct((H, NK, T), BF16), jax.ShapeDtypeStruct((H, NK, T), BF16)],
        compiler_params=_params("parallel"),
        name="peer_select",
    )(s1, s2)


def _peer_gate_rows(act, a, n_ref, p1_ref, r2_ref, e2_ref, w_ref, row0):
    NK = PEER_KEYS
    tm = act.shape[-1]
    pk = 2 * SUBLANES
    zero = jnp.zeros((pk, tm), BF16)
    bc = lambda ref, h: pltpu.bitcast(jnp.broadcast_to(ref[h, pl.ds(a, 1), :], (SUBLANES, tm)), BF16)
    nrow = [bc(n_ref, h) for h in range(PEER_HEADS)]
    prow = [bc(p1_ref, h) for h in range(PEER_HEADS)]
    for g in range(NK // pk):
        b_rows = slice(g * pk, (g + 1) * pk)
        gate = zero
        for h in range(PEER_HEADS):
            sel = r2_ref[h, b_rows, :] < nrow[h]
            gate = gate + jnp.where(sel, prow[h] * e2_ref[h, b_rows, :], zero)
        w_ref[row0 + g * pk:row0 + (g + 1) * pk, :] = gate * act[b_rows, :]


def _peer_half(a0, hft, keys, u_ref, w_ref, vt_prev_ref, w_prev_ref, acc_ref, gl_ref, na):
    NK = PEER_KEYS
    dr = acc_ref.shape[0] // na
    w_prev = w_prev_ref[...]
    for ai in range(na):
        if ai == 0:
            gl4 = _gelu(_dot(u_ref[...], hft).astype(BF16))
            for k in range(na):
                gl_ref[k] = gl4[k * NK:(k + 1) * NK]
        if ai % 2 == 0:
            rows = slice(ai * dr, (ai + 2) * dr)
            acc_ref[rows, :] += _dot(vt_prev_ref[rows, :], w_prev)
        _peer_gate_rows(gl_ref.at[ai], a0 + ai, *keys, w_ref, ai * NK)


def _peer_dense_kernel(x_ref, hft_ref, n_ref, p1_ref, r2_ref, e2_ref, ua_ref, ub_ref, vta_ref, vtb_ref,
                       vtl_ref, *rest, na, norm_out):
    g_ref = rest[0] if norm_out else None
    o_ref, acc_ref, wa_ref, wb_ref, gla_ref, glb_ref = rest[-6:]
    j = pl.program_id(1)
    last = pl.num_programs(1) - 1
    keys = (n_ref, p1_ref, r2_ref, e2_ref)

    @pl.when(j == 0)
    def _():
        acc_ref[...] = jnp.zeros_like(acc_ref)
        wb_ref[...] = jnp.zeros_like(wb_ref)

    hft = hft_ref[...]
    _peer_half(2 * j * na, hft, keys, ua_ref, wa_ref, vtb_ref, wb_ref, acc_ref, gla_ref, na)
    _peer_half((2 * j + 1) * na, hft, keys, ub_ref, wb_ref, vta_ref, wa_ref, acc_ref, glb_ref, na)

    @pl.when(j == last)
    def _():
        acc = acc_ref[...] + _dot(vtl_ref[...], wb_ref[...])
        xo = x_ref[...] + jnp.transpose(acc)
        o_ref[...] = _rms(xo, g_ref[...]) if norm_out else xo


def _peer_dense(x, hft, n1, p1, r2, e2, u, vt, layer, tm, na, g_out=None):
    T, D = x.shape
    H, NK = PEER_HEADS, PEER_KEYS
    th = na * NK
    steps = NK // (2 * na)
    kspec = pl.BlockSpec((H, NK, tm), lambda i, j: (0, 0, i))
    return pl.pallas_call(
        functools.partial(_peer_dense_kernel, na=na, norm_out=g_out is not None),
        grid=(T // tm, steps),
        in_specs=[pl.BlockSpec((tm, D), lambda i, j: (i, 0)),
                  pl.BlockSpec((D, tm), lambda i, j: (0, i)),
                  kspec, kspec, kspec, kspec,
                  pl.BlockSpec((None, th, D), lambda i, j: (layer, 2 * j, 0)),
                  pl.BlockSpec((None, th, D), lambda i, j: (layer, 2 * j + 1, 0)),
                  pl.BlockSpec((None, None, D, th), lambda i, j: (layer, 2 * j, 0, 0)),
                  pl.BlockSpec((None, None, D, th), lambda i, j: (layer, jnp.maximum(2 * j - 1, 1), 0, 0)),
                  pl.BlockSpec((None, None, D, th), lambda i, j: (layer, 2 * steps - 1, 0, 0))]
                 + ([pl.BlockSpec((1, D), lambda i, j: (0, 0))] if g_out is not None else []),
        out_specs=pl.BlockSpec((tm, D), lambda i, j: (i, 0)),
        out_shape=jax.ShapeDtypeStruct((T, D), F32),
        scratch_shapes=[pltpu.VMEM((D, tm), F32), pltpu.VMEM((th, tm), BF16), pltpu.VMEM((th, tm), BF16),
                        pltpu.VMEM((na, NK, tm), BF16), pltpu.VMEM((na, NK, tm), BF16)],
        compiler_params=_params("parallel", "arbitrary"),
        name="peer_dense",
    )(x, hft, n1, p1, r2, e2, u, u, vt, vt, vt, *(() if g_out is None else (g_out,)))


def kernel(x, mem, positions, g_mix, w_in, w_ret_br, w_rnn_br, w_mix_out, conv_w, conv_b, lru_wa, lru_ba, lru_wx, lru_bx, lru_lam, g_x, g_mem, w_xq, w_xk, w_xv, w_xo, g_ffn, w_pq, sub_k1, sub_k2, peer_u, peer_v, g_final):
    B, T, D = x.shape
    assert B == 1 and D == D_MODEL
    depth = g_mix.shape[0]
    tm = min(TM_PROJ, T)
    xs = x.reshape(T, D)
    mems = mem.reshape(mem.shape[1], D)
    pos_col = positions.reshape(T, 1)
    row = lambda a: a.reshape(1, -1)

    cos2, sin2 = _rope_tables(pos_col, tm)
    u_all = peer_u.astype(BF16)
    th = NA_PEER * PEER_KEYS
    vt_all = jnp.swapaxes(peer_v.reshape(depth, -1, th, D), 2, 3).astype(BF16)
    for l in range(depth):
        z = _rms_matmul(xs, row(g_mix[l]), w_in, l, tm=min(TM_IN, T), tn=TN_PROJ, out_dtype=BF16)
        ret = _retention(z, cos2, sin2, min(TB_RET, T))
        rnn = _rglru(z, pos_col, conv_w[l], row(conv_b[l]), lru_wa[l].astype(BF16), row(lru_ba[l]),
                     lru_wx[l].astype(BF16), row(lru_bx[l]), row(lru_lam[l]), min(TB_LRU, T))
        xs = _merge(xs, ret, rnn, z, w_ret_br, w_rnn_br, w_mix_out, l, tm)
        kmem = _rms_matmul(mems, row(g_mem[l]), w_xk, l, tm=mems.shape[0], tn=D, out_dtype=BF16)
        vmem = _rms_matmul(mems, row(g_mem[l]), w_xv, l, tm=mems.shape[0], tn=D, out_dtype=BF16)
        xs = _xattn(xs, row(g_x[l]), w_xq, kmem.T, vmem, w_xo, l, tm)
        tp = min(TM_PEER, T)
        hft, s1, s2 = _peer_scores(xs, row(g_ffn[l]), w_pq[l].T.astype(BF16), sub_k1[l].astype(BF16),
                                   sub_k2[l].astype(BF16), tp)
        n1, p1, r2, e2 = _peer_select(s1, s2, TL_SORT)
        xs = _peer_dense(xs, hft, n1, p1, r2, e2, u_all, vt_all, l, tp, NA_PEER,
                         g_out=row(g_final) if l == depth - 1 else None)
    return xs.reshape(B, T, D)
```

```python
import functools
import math

import numpy as np
import jax
import jax.numpy as jnp
from jax import lax
from jax.experimental import pallas as pl
from jax.experimental.pallas import tpu as pltpu

F32 = jnp.float32
BF16 = jnp.bfloat16

D_MODEL = 1024
EPS = 1e-6
RET_HEADS = 4
RET_QK_DIM = 128
RET_V_DIM = 256
RET_CHUNK = 128
ROPE_BASE = 10000.0
LRU_BLOCKS = 8
LRU_BLOCK_DIM = D_MODEL // LRU_BLOCKS
CONV_WIDTH = 4
LRU_C = 8.0
XATTN_HEADS = 4
XATTN_HEAD_DIM = D_MODEL // XATTN_HEADS
PEER_HEADS = 8
PEER_KEYS = 128
PEER_TOPK = 16
PEER_HALF = 128

LANES = 128
SUBLANES = 8
VMEM_LIMIT_BYTES = 56 * 1024 * 1024

TM_PROJ = 512
TM_IN = 2048
TN_PROJ = 1024
TB_RET = 256
TB_LRU = 512
TM_PEER = 512
NA_PEER = 8
TL_SORT = 256


def _params(*sem):
    return pltpu.CompilerParams(dimension_semantics=sem, vmem_limit_bytes=VMEM_LIMIT_BYTES)


def _rms(x, g):
    return x * lax.rsqrt(jnp.mean(x * x, axis=-1, keepdims=True) + EPS) * g


def _gelu(x):
    c = math.sqrt(2.0 / math.pi)
    return 0.5 * x * (1.0 + jnp.tanh(c * (x + 0.044715 * (x * x * x))))


def _dot(a, b):
    return jnp.dot(a, b, preferred_element_type=F32)


def _dot_nt(a, b):
    return lax.dot_general(a, b, (((1,), (1,)), ((), ())), preferred_element_type=F32)


def _rms_matmul_kernel(x_ref, g_ref, w_ref, o_ref, h_ref):
    @pl.when(pl.program_id(1) == 0)
    def _():
        h_ref[...] = _rms(x_ref[...], g_ref[...]).astype(BF16)

    o_ref[...] = _dot(h_ref[...], w_ref[...].astype(BF16)).astype(o_ref.dtype)


def _rms_matmul(x, g, w, layer, *, tm, tn, out_dtype):
    T, K = x.shape
    N = w.shape[2]
    return pl.pallas_call(
        _rms_matmul_kernel,
        grid=(T // tm, N // tn),
        in_specs=[pl.BlockSpec((tm, K), lambda i, j: (i, 0)),
                  pl.BlockSpec((1, K), lambda i, j: (0, 0)),
                  pl.BlockSpec((None, K, tn), lambda i, j: (layer, 0, j))],
        out_specs=pl.BlockSpec((tm, tn), lambda i, j: (i, j)),
        out_shape=jax.ShapeDtypeStruct((T, N), out_dtype),
        scratch_shapes=[pltpu.VMEM((tm, K), BF16)],
        compiler_params=_params("parallel", "arbitrary"),
        name="rms_matmul",
    )(x, g, w)


def _rope_kernel(pos_ref, invf_ref, cos_ref, sin_ref):
    ang = pos_ref[...].astype(F32) * invf_ref[...]
    lane = lax.broadcasted_iota(jnp.int32, ang.shape, 1)
    s = jnp.sin(ang)
    cos_ref[...] = jnp.cos(ang)
    sin_ref[...] = jnp.where(lane < RET_QK_DIM // 2, -s, s)


def _rope_tables(pos_col, tb):
    T = pos_col.shape[0]
    half = RET_QK_DIM // 2
    inv_freq = ROPE_BASE ** (-jnp.arange(0, RET_QK_DIM, 2, dtype=F32) / RET_QK_DIM)
    invf2 = jnp.concatenate([inv_freq, inv_freq]).reshape(1, 2 * half)
    return pl.pallas_call(
        _rope_kernel,
        grid=(T // tb,),
        in_specs=[pl.BlockSpec((tb, 1), lambda i: (i, 0)),
                  pl.BlockSpec((1, RET_QK_DIM), lambda i: (0, 0))],
        out_specs=[pl.BlockSpec((tb, RET_QK_DIM), lambda i: (i, 0))] * 2,
        out_shape=[jax.ShapeDtypeStruct((T, RET_QK_DIM), F32)] * 2,
        compiler_params=_params("parallel"),
        name="rope_tables",
    )(pos_col, invf2)


def _retention_kernel(q_ref, k_ref, v_ref, g_ref, cos_ref, sin_ref, dec_ref, kdec_ref, qdec_ref,
                      o_ref, s_ref, *, chunk_decay):
    C, dk, dv = RET_CHUNK, RET_QK_DIM, RET_V_DIM

    @pl.when(pl.program_id(0) == 0)
    def _():
        s_ref[...] = jnp.zeros_like(s_ref)

    kscale = dk ** -0.5
    for c in range(q_ref.shape[0] // C):
        rows = slice(c * C, (c + 1) * C)
        cos = cos_ref[rows, :]
        sin = sin_ref[rows, :]
        for h in range(RET_HEADS):
            q = q_ref[rows, h * dk:(h + 1) * dk].astype(F32)
            k = k_ref[rows, h * dk:(h + 1) * dk].astype(F32)
            v = v_ref[rows, h * dv:(h + 1) * dv].astype(BF16)
            qr = q * cos + pltpu.roll(q, dk // 2, 1) * sin
            kr = (k * cos + pltpu.roll(k, dk // 2, 1) * sin) * kscale
            scores = _dot_nt(qr.astype(BF16), kr.astype(BF16)) * dec_ref[h]
            inner = _dot(scores.astype(BF16), v)
            state = s_ref[h]
            cross = _dot((qr * qdec_ref[h]).astype(BF16), state.astype(BF16))
            kd_t = jnp.transpose(kr * kdec_ref[h]).astype(BF16)
            s_ref[h] = chunk_decay[h] * state + _dot(kd_t, v)
            o = inner + cross
            mu = jnp.mean(o, axis=-1, keepdims=True)
            oc = o - mu
            var = jnp.mean(oc * oc, axis=-1, keepdims=True)
            o = oc * lax.rsqrt(var + EPS)
            g = g_ref[rows, h * dv:(h + 1) * dv].astype(F32)
            o_ref[rows, h * dv:(h + 1) * dv] = (g * jax.nn.sigmoid(g) * o).astype(o_ref.dtype)


def _retention(z, cos2, sin2, tb):
    T = z.shape[0]
    H, C, dk, dv = RET_HEADS, RET_CHUNK, RET_QK_DIM, RET_V_DIM
    log_g = jnp.log(1.0 - 2.0 ** (-5.0 - jnp.arange(H, dtype=F32)))
    i = jnp.arange(C, dtype=F32)
    diff = i[:, None] - i[None, :]
    decay = jnp.where(diff[None] >= 0, jnp.exp(diff[None] * log_g[:, None, None]), 0.0)
    k_decay = jnp.exp((C - 1.0 - i)[None, :] * log_g[:, None])
    q_decay = jnp.exp((i + 1.0)[None, :] * log_g[:, None])
    kdec = jnp.broadcast_to(k_decay[:, :, None], (H, C, dk))
    qdec = jnp.broadcast_to(q_decay[:, :, None], (H, C, dk))
    log_g_np = np.log(1.0 - 2.0 ** (-5.0 - np.arange(H, dtype=np.float32))).astype(np.float32)
    chunk_decay = tuple(float(np.exp(np.float32(C) * lg)) for lg in log_g_np)
    nq = H * dk
    const3 = lambda i: (0, 0, 0)
    return pl.pallas_call(
        functools.partial(_retention_kernel, chunk_decay=chunk_decay),
        grid=(T // tb,),
        in_specs=[pl.BlockSpec((tb, nq), lambda i: (i, 0)),
                  pl.BlockSpec((tb, nq), lambda i: (i, 1)),
                  pl.BlockSpec((tb, H * dv), lambda i: (i, 1)),
                  pl.BlockSpec((tb, H * dv), lambda i: (i, 2)),
                  pl.BlockSpec((tb, dk), lambda i: (i, 0)),
                  pl.BlockSpec((tb, dk), lambda i: (i, 0)),
                  pl.BlockSpec((H, C, C), const3),
                  pl.BlockSpec((H, C, dk), const3),
                  pl.BlockSpec((H, C, dk), const3)],
        out_specs=pl.BlockSpec((tb, H * dv), lambda i: (i, 0)),
        out_shape=jax.ShapeDtypeStruct((T, H * dv), BF16),
        scratch_shapes=[pltpu.VMEM((H, dk, dv), F32)],
        compiler_params=_params("arbitrary"),
        name="retention",
    )(z, z, z, z, cos2, sin2, decay, kdec, qdec)


def _rglru_kernel(x_ref, y_ref, pos_ref, cw_ref, cb_ref, wa_ref, ba_ref, wx_ref, bx_ref, lam_ref,
                  o_ref, xpad_ref, a_ref, b_ref, hc_ref):
    tb = x_ref.shape[0]
    pad = SUBLANES
    halo = SUBLANES

    @pl.when(pl.program_id(0) == 0)
    def _():
        xpad_ref[0:halo, :] = jnp.zeros((halo, D_MODEL), F32)
        hc_ref[...] = jnp.zeros_like(hc_ref)
        a_ref[0:pad, :] = jnp.ones((pad, D_MODEL), F32)
        b_ref[0:pad, :] = jnp.zeros((pad, D_MODEL), F32)

    xpad_ref[halo:halo + tb, :] = x_ref[...].astype(F32)
    xc = cb_ref[...] + cw_ref[3:4, :] * xpad_ref[halo:halo + tb, :]
    for w in range(CONV_WIDTH - 1):
        off = halo - (CONV_WIDTH - 1) + w
        xc = xc + cw_ref[w:w + 1, :] * xpad_ref[off:off + tb, :]
    xpad_ref[0:halo, :] = xpad_ref[tb:tb + halo, :]

    reset = pos_ref[...] == 0
    lam = lam_ref[...]
    nl = -lam
    sp = jnp.maximum(nl, 0.0) + jnp.log(1.0 + jnp.exp(-jnp.abs(nl)))
    bd = LRU_BLOCK_DIM
    for n in range(LRU_BLOCKS):
        cols = slice(n * bd, (n + 1) * bd)
        xcn = xc[:, cols]
        xb = xcn.astype(BF16)
        gate_r = jax.nn.sigmoid(_dot(xb, wa_ref[n]) + ba_ref[:, cols])
        gate_i = jax.nn.sigmoid(_dot(xb, wx_ref[n]) + bx_ref[:, cols])
        log_a = (-LRU_C) * gate_r * sp[:, cols]
        a = jnp.exp(log_a)
        mult = jnp.sqrt(jnp.tanh(-log_a) * (a * a + 1.0))
        a = jnp.where(reset, 0.0, a)
        mult = jnp.where(reset, 1.0, mult)
        b = xcn * gate_i * mult
        s = 1
        while s < SUBLANES:
            a_ref[pad:pad + tb, cols] = a
            b_ref[pad:pad + tb, cols] = b
            a_sh = a_ref[pad - s:pad - s + tb, cols]
            b_sh = b_ref[pad - s:pad - s + tb, cols]
            b = a * b_sh + b
            a = a * a_sh
            s *= 2
        hg = hc_ref[:, cols]
        groups = []
        for g in range(tb // SUBLANES):
            rows = slice(g * SUBLANES, (g + 1) * SUBLANES)
            hg = a[rows, :] * hg + b[rows, :]
            groups.append(hg)
        h = jnp.concatenate(groups, axis=0)
        hc_ref[:, cols] = jnp.broadcast_to(hg[SUBLANES - 1:SUBLANES, :], (SUBLANES, bd))
        o_ref[:, cols] = (h * _gelu(y_ref[:, cols].astype(F32))).astype(o_ref.dtype)


def _rglru(z, pos_col, conv_w, conv_b, wa, ba, wx, bx, lam, tb):
    T = z.shape[0]
    W = D_MODEL
    row = lambda i: (0, 0)
    const3 = lambda i: (0, 0, 0)
    return pl.pallas_call(
        _rglru_kernel,
        grid=(T // tb,),
        in_specs=[pl.BlockSpec((tb, W), lambda i: (i, 3)),
                  pl.BlockSpec((tb, W), lambda i: (i, 4)),
                  pl.BlockSpec((tb, 1), lambda i: (i, 0)),
                  pl.BlockSpec((CONV_WIDTH, W), row),
                  pl.BlockSpec((1, W), row),
                  pl.BlockSpec((LRU_BLOCKS, LRU_BLOCK_DIM, LRU_BLOCK_DIM), const3),
                  pl.BlockSpec((1, W), row),
                  pl.BlockSpec((LRU_BLOCKS, LRU_BLOCK_DIM, LRU_BLOCK_DIM), const3),
                  pl.BlockSpec((1, W), row),
                  pl.BlockSpec((1, W), row)],
        out_specs=pl.BlockSpec((tb, W), lambda i: (i, 0)),
        out_shape=jax.ShapeDtypeStruct((T, W), BF16),
        scratch_shapes=[pltpu.VMEM((tb + 2 * SUBLANES, W), F32),
                        pltpu.VMEM((tb + SUBLANES, W), F32),
                        pltpu.VMEM((tb + SUBLANES, W), F32),
                        pltpu.VMEM((SUBLANES, W), F32)],
        compiler_params=_params("arbitrary"),
        name="rglru",
    )(z, z, pos_col, conv_w, conv_b, wa, ba, wx, bx, lam)


def _merge_kernel(x_ref, ret_ref, rnn_ref, ga_ref, gb_ref, wr_ref, wn_ref, wo_ref, o_ref, w_ref):
    @pl.when(pl.program_id(0) == 0)
    def _():
        for k, src in enumerate((wr_ref, wn_ref, wo_ref)):
            w_ref[k] = src[...].astype(BF16)

    p_ret = _dot(ret_ref[...], w_ref[0])
    p_rnn = _dot(rnn_ref[...], w_ref[1])
    merged = (jax.nn.sigmoid(ga_ref[...].astype(F32)) * p_ret
              + jax.nn.sigmoid(gb_ref[...].astype(F32)) * p_rnn)
    o_ref[...] = x_ref[...] + _dot(merged.astype(BF16), w_ref[2])


def _merge(x, ret, rnn, z, w_ret, w_rnn, w_out, layer, tm):
    T, D = x.shape
    tile = lambda i: (i, 0)
    wspec = pl.BlockSpec((None, D, D), lambda i: (layer, 0, 0))
    return pl.pallas_call(
        _merge_kernel,
        grid=(T // tm,),
        in_specs=[pl.BlockSpec((tm, D), tile), pl.BlockSpec((tm, D), tile), pl.BlockSpec((tm, D), tile),
                  pl.BlockSpec((tm, D), lambda i: (i, 5)),
                  pl.BlockSpec((tm, D), lambda i: (i, 6)),
                  wspec, wspec, wspec],
        out_specs=pl.BlockSpec((tm, D), tile),
        out_shape=jax.ShapeDtypeStruct((T, D), F32),
        scratch_shapes=[pltpu.VMEM((3, D, D), BF16)],
        compiler_params=_params("arbitrary"),
        name="merge",
    )(x, ret, rnn, z, z, w_ret, w_rnn, w_out)


def _xattn_kernel(x_ref, g_ref, wq_ref, kt_ref, v_ref, wo_ref, o_ref, att_ref, w_ref):
    @pl.when(pl.program_id(0) == 0)
    def _():
        w_ref[0] = wq_ref[...].astype(BF16)
        w_ref[1] = wo_ref[...].astype(BF16)

    x = x_ref[...]
    q = _dot(_rms(x, g_ref[...]).astype(BF16), w_ref[0])
    dh = XATTN_HEAD_DIM
    for h in range(XATTN_HEADS):
        cols = slice(h * dh, (h + 1) * dh)
        s = _dot(q[:, cols].astype(BF16), kt_ref[cols, :]) * (dh ** -0.5)
        s = s - jnp.max(s, axis=-1, keepdims=True)
        e = jnp.exp(s)
        p = e / jnp.sum(e, axis=-1, keepdims=True)
        att_ref[:, cols] = _dot(p.astype(BF16), v_ref[:, cols]).astype(BF16)
    o_ref[...] = x + _dot(att_ref[...], w_ref[1])


def _xattn(x, g, wq, kt, v, wo, layer, tm):
    T, D = x.shape
    M = v.shape[0]
    tile = lambda i: (i, 0)
    fixed = lambda i: (0, 0)
    wspec = pl.BlockSpec((None, D, D), lambda i: (layer, 0, 0))
    return pl.pallas_call(
        _xattn_kernel,
        grid=(T // tm,),
        in_specs=[pl.BlockSpec((tm, D), tile), pl.BlockSpec((1, D), fixed), wspec,
                  pl.BlockSpec((D, M), fixed), pl.BlockSpec((M, D), fixed), wspec],
        out_specs=pl.BlockSpec((tm, D), tile),
        out_shape=jax.ShapeDtypeStruct((T, D), F32),
        scratch_shapes=[pltpu.VMEM((tm, D), BF16), pltpu.VMEM((2, D, D), BF16)],
        compiler_params=_params("arbitrary"),
        name="xattn",
    )(x, g, wq, kt, v, wo)


def _peer_scores_kernel(x_ref, g_ref, wq_ref, k1_ref, k2_ref, hft_ref, s1_ref, s2_ref):
    hf = _rms(x_ref[...], g_ref[...])
    hft = jnp.transpose(hf).astype(BF16)
    hft_ref[...] = hft
    qt = _dot(wq_ref[...], hft)
    kd = 2 * PEER_HALF
    for h in range(PEER_HEADS):
        q1 = qt[h * kd:h * kd + PEER_HALF, :].astype(BF16)
        q2 = qt[h * kd + PEER_HALF:(h + 1) * kd, :].astype(BF16)
        s1_ref[h] = _dot(k1_ref[h], q1)
        s2_ref[h] = _dot(k2_ref[h], q2)


def _peer_scores(x, g, wq_t, k1, k2, tm):
    T, D = x.shape
    H, NK = PEER_HEADS, PEER_KEYS
    fixed = lambda i: (0, 0)
    const3 = lambda i: (0, 0, 0)
    return pl.pallas_call(
        _peer_scores_kernel,
        grid=(T // tm,),
        in_specs=[pl.BlockSpec((tm, D), lambda i: (i, 0)), pl.BlockSpec((1, D), fixed),
                  pl.BlockSpec(wq_t.shape, fixed),
                  pl.BlockSpec((H, NK, PEER_HALF), const3), pl.BlockSpec((H, NK, PEER_HALF), const3)],
        out_specs=[pl.BlockSpec((D, tm), lambda i: (0, i)),
                   pl.BlockSpec((H, NK, tm), lambda i: (0, 0, i)),
                   pl.BlockSpec((H, NK, tm), lambda i: (0, 0, i))],
        out_shape=[jax.ShapeDtypeStruct((D, T), BF16),
                   jax.ShapeDtypeStruct((H, NK, T), F32),
                   jax.ShapeDtypeStruct((H, NK, T), F32)],
        compiler_params=_params("parallel"),
        name="peer_scores",
    )(x, g, wq_t, k1, k2)


def _sort_pairs(n):
    pairs = []
    p = 1
    while p < n:
        k = p
        while k >= 1:
            for j in range(k % p, n - k, 2 * k):
                for i in range(min(k, n - j - k)):
                    if (i + j) // (2 * p) == (i + j + k) // (2 * p):
                        pairs.append((i + j, i + j + k))
            k //= 2
        p *= 2
    return pairs


def _sort_desc(vs):
    vs = list(vs)
    for lo, hi in _sort_pairs(len(vs)):
        a, b = vs[lo], vs[hi]
        vs[lo], vs[hi] = jnp.maximum(a, b), jnp.minimum(a, b)
    return vs


def _merge_top(a, b):
    n = len(a)
    c = [jnp.maximum(a[i], b[n - 1 - i]) for i in range(n)]
    k = n // 2
    while k >= 1:
        for i in range(n):
            if i & k == 0:
                x, y = c[i], c[i + k]
                c[i], c[i + k] = jnp.maximum(x, y), jnp.minimum(x, y)
        k //= 2
    return c


def _top_of_keys(groups, roll):
    vs = _sort_desc(groups)
    shift = SUBLANES // 2
    while shift >= 1:
        vs = _merge_top(vs, [roll(v, shift, 0) for v in vs])
        shift //= 2
    return vs


def _peer_select_math(s1, s2, roll):
    K = PEER_TOPK
    shape = s1[0][0].shape
    sub = lax.broadcasted_iota(jnp.int32, shape, 0)
    zero = jnp.zeros(shape, F32)
    tops2 = []
    v1 = [zero] * K
    v2 = [zero] * K
    for h in range(PEER_HEADS):
        t1 = _top_of_keys(s1[h], roll)
        t2 = _top_of_keys(s2[h], roll)
        tops2.append(t2)
        v1 = [jnp.where(sub == h, t, v) for t, v in zip(t1, v1)]
        v2 = [jnp.where(sub == h, t, v) for t, v in zip(t2, v2)]
    cands = [v1[i] + v2[j] for i in range(K) for j in range(K // (i + 1))]
    n = 1
    while n < len(cands):
        n *= 2
    cands = cands + [jnp.full(shape, -jnp.inf, F32)] * (n - len(cands))
    top = _sort_desc(cands)[:K]
    z = zero
    for t in top:
        z = z + jnp.exp(t - top[0])
    tau_all, zinv_all = top[K - 1], 1.0 / z
    out = []
    for h in range(PEER_HEADS):
        bc = lambda v: jnp.broadcast_to(v[h:h + 1, :], shape)
        tau, zinv, m1, m2 = bc(tau_all), bc(zinv_all), bc(v1[0]), bc(v2[0])
        t2 = tops2[h]
        cnt, p1, rk, e2 = [], [], [], []
        c_top = zero
        for j in range(K // 2, K):
            c_top = jnp.where(m1 + t2[j] >= tau, float(j + 1), c_top)
        for s1g, s2g in zip(s1[h], s2[h]):
            c = zero
            r = zero
            for j in range(K // 2):
                c = jnp.where(s1g + t2[j] >= tau, float(j + 1), c)
            c = jnp.where(s1g == m1, jnp.maximum(c, c_top), c)
            for j in range(K):
                r = jnp.where(t2[j] > s2g, float(j + 1), r)
            cnt.append(c)
            rk.append(r)
            p1.append(jnp.exp(s1g - m1) * zinv)
            e2.append(jnp.exp(s2g - m2))
        out.append((cnt, p1, rk, e2))
    return out


def _bf16_pair(x):
    u = lax.bitcast_convert_type(x.astype(BF16).astype(F32), jnp.uint32)
    return u | (u >> 16)


def _peer_select_kernel(s1_ref, s2_ref, n_ref, p1_ref, r2_ref, e2_ref):
    ng = PEER_KEYS // SUBLANES
    grp = lambda ref, h: [ref[h, i * SUBLANES:(i + 1) * SUBLANES, :] for i in range(ng)]
    s1 = [grp(s1_ref, h) for h in range(PEER_HEADS)]
    s2 = [grp(s2_ref, h) for h in range(PEER_HEADS)]
    res = _peer_select_math(s1, s2, pltpu.roll)
    pk = 2 * SUBLANES
    for h, (cnt, p1, rk, e2) in enumerate(res):
        for i in range(ng):
            rows = slice(i * SUBLANES, (i + 1) * SUBLANES)
            n_ref[h, rows, :] = _bf16_pair(cnt[i])
            p1_ref[h, rows, :] = _bf16_pair(p1[i])
        for i in range(ng // 2):
            rows = slice(i * pk, (i + 1) * pk)
            r2_ref[h, rows, :] = jnp.concatenate([rk[2 * i], rk[2 * i + 1]], axis=0).astype(BF16)
            e2_ref[h, rows, :] = jnp.concatenate([e2[2 * i], e2[2 * i + 1]], axis=0).astype(BF16)


def _peer_select(s1, s2, tl):
    H, NK, T = s1.shape
    spec = pl.BlockSpec((H, NK, tl), lambda i: (0, 0, i))
    return pl.pallas_call(
        _peer_select_kernel,
        grid=(T // tl,),
        in_specs=[spec] * 2,
        out_specs=[spec] * 4,
        out_shape=[jax.ShapeDtypeStruct((H, NK, T), jnp.uint32), jax.ShapeDtypeStruct((H, NK, T), jnp.uint32),
                   jax.ShapeDtypeStruct((H, NK, T), BF16), jax.ShapeDtypeStruct((H, NK, T), BF16)],
        compiler_params=_params("parallel"),
        name="peer_select",
    )(s1, s2)


def _peer_gate_rows(act, a, n_ref, p1_ref, r2_ref, e2_ref, w_ref, row0):
    NK = PEER_KEYS
    tm = act.shape[-1]
    pk = 2 * SUBLANES
    zero = jnp.zeros((pk, tm), BF16)
    bc = lambda ref, h: pltpu.bitcast(jnp.broadcast_to(ref[h, pl.ds(a, 1), :], (SUBLANES, tm)), BF16)
    nrow = [bc(n_ref, h) for h in range(PEER_HEADS)]
    prow = [bc(p1_ref, h) for h in range(PEER_HEADS)]
    for g in range(NK // pk):
        b_rows = slice(g * pk, (g + 1) * pk)
        gate = zero
        for h in range(PEER_HEADS):
            sel = r2_ref[h, b_rows, :] < nrow[h]
            gate = gate + jnp.where(sel, prow[h] * e2_ref[h, b_rows, :], zero)
        w_ref[row0 + g * pk:row0 + (g + 1) * pk, :] = gate * act[b_rows, :]


def _peer_half(a0, hft, keys, u_ref, w_ref, vt_prev_ref, w_prev_ref, acc_ref, gl_ref, na):
    NK = PEER_KEYS
    dr = acc_ref.shape[0] // na
    w_prev = w_prev_ref[...]
    for ai in range(na):
        if ai == 0:
            gl4 = _gelu(_dot(u_ref[...], hft).astype(BF16))
            for k in range(na):
                gl_ref[k] = gl4[k * NK:(k + 1) * NK]
        if ai % 2 == 0:
            rows = slice(ai * dr, (ai + 2) * dr)
            acc_ref[rows, :] += _dot(vt_prev_ref[rows, :], w_prev)
        _peer_gate_rows(gl_ref.at[ai], a0 + ai, *keys, w_ref, ai * NK)


def _peer_dense_kernel(x_ref, hft_ref, n_ref, p1_ref, r2_ref, e2_ref, ua_ref, ub_ref, vta_ref, vtb_ref,
                       vtl_ref, *rest, na, norm_out):
    g_ref = rest[0] if norm_out else None
    o_ref, acc_ref, wa_ref, wb_ref, gla_ref, glb_ref = rest[-6:]
    j = pl.program_id(1)
    last = pl.num_programs(1) - 1
    keys = (n_ref, p1_ref, r2_ref, e2_ref)

    @pl.when(j == 0)
    def _():
        acc_ref[...] = jnp.zeros_like(acc_ref)
        wb_ref[...] = jnp.zeros_like(wb_ref)

    hft = hft_ref[...]
    _peer_half(2 * j * na, hft, keys, ua_ref, wa_ref, vtb_ref, wb_ref, acc_ref, gla_ref, na)
    _peer_half((2 * j + 1) * na, hft, keys, ub_ref, wb_ref, vta_ref, wa_ref, acc_ref, glb_ref, na)

    @pl.when(j == last)
    def _():
        acc = acc_ref[...] + _dot(vtl_ref[...], wb_ref[...])
        xo = x_ref[...] + jnp.transpose(acc)
        o_ref[...] = _rms(xo, g_ref[...]) if norm_out else xo


def _peer_dense(x, hft, n1, p1, r2, e2, u, vt, layer, tm, na, g_out=None):
    T, D = x.shape
    H, NK = PEER_HEADS, PEER_KEYS
    th = na * NK
    steps = NK // (2 * na)
    kspec = pl.BlockSpec((H, NK, tm), lambda i, j: (0, 0, i))
    return pl.pallas_call(
        functools.partial(_peer_dense_kernel, na=na, norm_out=g_out is not None),
        grid=(T // tm, steps),
        in_specs=[pl.BlockSpec((tm, D), lambda i, j: (i, 0)),
                  pl.BlockSpec((D, tm), lambda i, j: (0, i)),
                  kspec, kspec, kspec, kspec,
                  pl.BlockSpec((None, th, D), lambda i, j: (layer, 2 * j, 0)),
                  pl.BlockSpec((None, th, D), lambda i, j: (layer, 2 * j + 1, 0)),
                  pl.BlockSpec((None, None, D, th), lambda i, j: (layer, 2 * j, 0, 0)),
                  pl.BlockSpec((None, None, D, th), lambda i, j: (layer, jnp.maximum(2 * j - 1, 1), 0, 0)),
                  pl.BlockSpec((None, None, D, th), lambda i, j: (layer, 2 * steps - 1, 0, 0))]
                 + ([pl.BlockSpec((1, D), lambda i, j: (0, 0))] if g_out is not None else []),
        out_specs=pl.BlockSpec((tm, D), lambda i, j: (i, 0)),
        out_shape=jax.ShapeDtypeStruct((T, D), F32),
        scratch_shapes=[pltpu.VMEM((D, tm), F32), pltpu.VMEM((th, tm), BF16), pltpu.VMEM((th, tm), BF16),
                        pltpu.VMEM((na, NK, tm), BF16), pltpu.VMEM((na, NK, tm), BF16)],
        compiler_params=_params("parallel", "arbitrary"),
        name="peer_dense",
    )(x, hft, n1, p1, r2, e2, u, u, vt, vt, vt, *(() if g_out is None else (g_out,)))


def kernel(x, mem, positions, g_mix, w_in, w_ret_br, w_rnn_br, w_mix_out, conv_w, conv_b, lru_wa, lru_ba, lru_wx, lru_bx, lru_lam, g_x, g_mem, w_xq, w_xk, w_xv, w_xo, g_ffn, w_pq, sub_k1, sub_k2, peer_u, peer_v, g_final):
    B, T, D = x.shape
    assert B == 1 and D == D_MODEL
    depth = g_mix.shape[0]
    tm = min(TM_PROJ, T)
    xs = x.reshape(T, D)
    mems = mem.reshape(mem.shape[1], D)
    pos_col = positions.reshape(T, 1)
    row = lambda a: a.reshape(1, -1)

    cos2, sin2 = _rope_tables(pos_col, tm)
    u_all = peer_u.astype(BF16)
    th = NA_PEER * PEER_KEYS
    vt_all = jnp.swapaxes(peer_v.reshape(depth, -1, th, D), 2, 3).astype(BF16)
    for l in range(depth):
        z = _rms_matmul(xs, row(g_mix[l]), w_in, l, tm=min(TM_IN, T), tn=TN_PROJ, out_dtype=BF16)
        ret = _retention(z, cos2, sin2, min(TB_RET, T))
        rnn = _rglru(z, pos_col, conv_w[l], row(conv_b[l]), lru_wa[l].astype(BF16), row(lru_ba[l]),
                     lru_wx[l].astype(BF16), row(lru_bx[l]), row(lru_lam[l]), min(TB_LRU, T))
        xs = _merge(xs, ret, rnn, z, w_ret_br, w_rnn_br, w_mix_out, l, tm)
        kmem = _rms_matmul(mems, row(g_mem[l]), w_xk, l, tm=mems.shape[0], tn=D, out_dtype=BF16)
        vmem = _rms_matmul(mems, row(g_mem[l]), w_xv, l, tm=mems.shape[0], tn=D, out_dtype=BF16)
        xs = _xattn(xs, row(g_x[l]), w_xq, kmem.T, vmem, w_xo, l, tm)
        tp = min(TM_PEER, T)
        hft, s1, s2 = _peer_scores(xs, row(g_ffn[l]), w_pq[l].T.astype(BF16), sub_k1[l].astype(BF16),
                                   sub_k2[l].astype(BF16), tp)
        n1, p1, r2, e2 = _peer_select(s1, s2, TL_SORT)
        xs = _peer_dense(xs, hft, n1, p1, r2, e2, u_all, vt_all, l, tp, NA_PEER,
                         g_out=row(g_final) if l == depth - 1 else None)
    return xs.reshape(B, T, D)
```

```python
import functools
import math

import numpy as np
import jax
import jax.numpy as jnp
from jax import lax
from jax.experimental import pallas as pl
from jax.experimental.pallas import tpu as pltpu

F32 = jnp.float32
BF16 = jnp.bfloat16

D_MODEL = 1024
EPS = 1e-6
RET_HEADS = 4
RET_QK_DIM = 128
RET_V_DIM = 256
RET_CHUNK = 128
ROPE_BASE = 10000.0
LRU_BLOCKS = 8
LRU_BLOCK_DIM = D_MODEL // LRU_BLOCKS
CONV_WIDTH = 4
LRU_C = 8.0
XATTN_HEADS = 4
XATTN_HEAD_DIM = D_MODEL // XATTN_HEADS
PEER_HEADS = 8
PEER_KEYS = 128
PEER_TOPK = 16
PEER_HALF = 128

LANES = 128
SUBLANES = 8
VMEM_LIMIT_BYTES = 56 * 1024 * 1024

TM_PROJ = 512
TM_IN = 2048
TN_PROJ = 1024
TB_RET = 256
TB_LRU = 512
TM_PEER = 512
NA_PEER = 4
TL_SORT = 256


def _params(*sem):
    return pltpu.CompilerParams(dimension_semantics=sem, vmem_limit_bytes=VMEM_LIMIT_BYTES)


def _rms(x, g):
    return x * lax.rsqrt(jnp.mean(x * x, axis=-1, keepdims=True) + EPS) * g


def _gelu(x):
    c = math.sqrt(2.0 / math.pi)
    return 0.5 * x * (1.0 + jnp.tanh(c * (x + 0.044715 * (x * x * x))))


def _dot(a, b):
    return jnp.dot(a, b, preferred_element_type=F32)


def _dot_nt(a, b):
    return lax.dot_general(a, b, (((1,), (1,)), ((), ())), preferred_element_type=F32)


def _rms_matmul_kernel(x_ref, g_ref, w_ref, o_ref, h_ref):
    @pl.when(pl.program_id(1) == 0)
    def _():
        h_ref[...] = _rms(x_ref[...], g_ref[...]).astype(BF16)

    o_ref[...] = _dot(h_ref[...], w_ref[...].astype(BF16)).astype(o_ref.dtype)


def _rms_matmul(x, g, w, layer, *, tm, tn, out_dtype):
    T, K = x.shape
    N = w.shape[2]
    return pl.pallas_call(
        _rms_matmul_kernel,
        grid=(T // tm, N // tn),
        in_specs=[pl.BlockSpec((tm, K), lambda i, j: (i, 0)),
                  pl.BlockSpec((1, K), lambda i, j: (0, 0)),
                  pl.BlockSpec((None, K, tn), lambda i, j: (layer, 0, j))],
        out_specs=pl.BlockSpec((tm, tn), lambda i, j: (i, j)),
        out_shape=jax.ShapeDtypeStruct((T, N), out_dtype),
        scratch_shapes=[pltpu.VMEM((tm, K), BF16)],
        compiler_params=_params("parallel", "arbitrary"),
        name="rms_matmul",
    )(x, g, w)


def _rope_kernel(pos_ref, invf_ref, cos_ref, sin_ref):
    ang = pos_ref[...].astype(F32) * invf_ref[...]
    lane = lax.broadcasted_iota(jnp.int32, ang.shape, 1)
    s = jnp.sin(ang)
    cos_ref[...] = jnp.cos(ang)
    sin_ref[...] = jnp.where(lane < RET_QK_DIM // 2, -s, s)


def _rope_tables(pos_col, tb):
    T = pos_col.shape[0]
    half = RET_QK_DIM // 2
    inv_freq = ROPE_BASE ** (-jnp.arange(0, RET_QK_DIM, 2, dtype=F32) / RET_QK_DIM)
    invf2 = jnp.concatenate([inv_freq, inv_freq]).reshape(1, 2 * half)
    return pl.pallas_call(
        _rope_kernel,
        grid=(T // tb,),
        in_specs=[pl.BlockSpec((tb, 1), lambda i: (i, 0)),
                  pl.BlockSpec((1, RET_QK_DIM), lambda i: (0, 0))],
        out_specs=[pl.BlockSpec((tb, RET_QK_DIM), lambda i: (i, 0))] * 2,
        out_shape=[jax.ShapeDtypeStruct((T, RET_QK_DIM), F32)] * 2,
        compiler_params=_params("parallel"),
        name="rope_tables",
    )(pos_col, invf2)


def _retention_kernel(q_ref, k_ref, v_ref, g_ref, cos_ref, sin_ref, dec_ref, kdec_ref, qdec_ref,
                      o_ref, s_ref, *, chunk_decay):
    C, dk, dv = RET_CHUNK, RET_QK_DIM, RET_V_DIM

    @pl.when(pl.program_id(0) == 0)
    def _():
        s_ref[...] = jnp.zeros_like(s_ref)

    kscale = dk ** -0.5
    for c in range(q_ref.shape[0] // C):
        rows = slice(c * C, (c + 1) * C)
        cos = cos_ref[rows, :]
        sin = sin_ref[rows, :]
        for h in range(RET_HEADS):
            q = q_ref[rows, h * dk:(h + 1) * dk].astype(F32)
            k = k_ref[rows, h * dk:(h + 1) * dk].astype(F32)
            v = v_ref[rows, h * dv:(h + 1) * dv].astype(BF16)
            qr = q * cos + pltpu.roll(q, dk // 2, 1) * sin
            kr = (k * cos + pltpu.roll(k, dk // 2, 1) * sin) * kscale
            scores = _dot_nt(qr.astype(BF16), kr.astype(BF16)) * dec_ref[h]
            inner = _dot(scores.astype(BF16), v)
            state = s_ref[h]
            cross = _dot((qr * qdec_ref[h]).astype(BF16), state.astype(BF16))
            kd_t = jnp.transpose(kr * kdec_ref[h]).astype(BF16)
            s_ref[h] = chunk_decay[h] * state + _dot(kd_t, v)
            o = inner + cross
            mu = jnp.mean(o, axis=-1, keepdims=True)
            oc = o - mu
            var = jnp.mean(oc * oc, axis=-1, keepdims=True)
            o = oc * lax.rsqrt(var + EPS)
            g = g_ref[rows, h * dv:(h + 1) * dv].astype(F32)
            o_ref[rows, h * dv:(h + 1) * dv] = (g * jax.nn.sigmoid(g) * o).astype(o_ref.dtype)


def _retention(z, cos2, sin2, tb):
    T = z.shape[0]
    H, C, dk, dv = RET_HEADS, RET_CHUNK, RET_QK_DIM, RET_V_DIM
    log_g = jnp.log(1.0 - 2.0 ** (-5.0 - jnp.arange(H, dtype=F32)))
    i = jnp.arange(C, dtype=F32)
    diff = i[:, None] - i[None, :]
    decay = jnp.where(diff[None] >= 0, jnp.exp(diff[None] * log_g[:, None, None]), 0.0)
    k_decay = jnp.exp((C - 1.0 - i)[None, :] * log_g[:, None])
    q_decay = jnp.exp((i + 1.0)[None, :] * log_g[:, None])
    kdec = jnp.broadcast_to(k_decay[:, :, None], (H, C, dk))
    qdec = jnp.broadcast_to(q_decay[:, :, None], (H, C, dk))
    log_g_np = np.log(1.0 - 2.0 ** (-5.0 - np.arange(H, dtype=np.float32))).astype(np.float32)
    chunk_decay = tuple(float(np.exp(np.float32(C) * lg)) for lg in log_g_np)
    nq = H * dk
    const3 = lambda i: (0, 0, 0)
    return pl.pallas_call(
        functools.partial(_retention_kernel, chunk_decay=chunk_decay),
        grid=(T // tb,),
        in_specs=[pl.BlockSpec((tb, nq), lambda i: (i, 0)),
                  pl.BlockSpec((tb, nq), lambda i: (i, 1)),
                  pl.BlockSpec((tb, H * dv), lambda i: (i, 1)),
                  pl.BlockSpec((tb, H * dv), lambda i: (i, 2)),
                  pl.BlockSpec((tb, dk), lambda i: (i, 0)),
                  pl.BlockSpec((tb, dk), lambda i: (i, 0)),
                  pl.BlockSpec((H, C, C), const3),
                  pl.BlockSpec((H, C, dk), const3),
                  pl.BlockSpec((H, C, dk), const3)],
        out_specs=pl.BlockSpec((tb, H * dv), lambda i: (i, 0)),
        out_shape=jax.ShapeDtypeStruct((T, H * dv), BF16),
        scratch_shapes=[pltpu.VMEM((H, dk, dv), F32)],
        compiler_params=_params("arbitrary"),
        name="retention",
    )(z, z, z, z, cos2, sin2, decay, kdec, qdec)


def _rglru_kernel(x_ref, y_ref, pos_ref, cw_ref, cb_ref, wa_ref, ba_ref, wx_ref, bx_ref, lam_ref,
                  o_ref, xpad_ref, a_ref, b_ref, hc_ref):
    tb = x_ref.shape[0]
    pad = SUBLANES
    halo = SUBLANES

    @pl.when(pl.program_id(0) == 0)
    def _():
        xpad_ref[0:halo, :] = jnp.zeros((halo, D_MODEL), F32)
        hc_ref[...] = jnp.zeros_like(hc_ref)
        a_ref[0:pad, :] = jnp.ones((pad, D_MODEL), F32)
        b_ref[0:pad, :] = jnp.zeros((pad, D_MODEL), F32)

    xpad_ref[halo:halo + tb, :] = x_ref[...].astype(F32)
    xc = cb_ref[...] + cw_ref[3:4, :] * xpad_ref[halo:halo + tb, :]
    for w in range(CONV_WIDTH - 1):
        off = halo - (CONV_WIDTH - 1) + w
        xc = xc + cw_ref[w:w + 1, :] * xpad_ref[off:off + tb, :]
    xpad_ref[0:halo, :] = xpad_ref[tb:tb + halo, :]

    reset = pos_ref[...] == 0
    lam = lam_ref[...]
    nl = -lam
    sp = jnp.maximum(nl, 0.0) + jnp.log(1.0 + jnp.exp(-jnp.abs(nl)))
    bd = LRU_BLOCK_DIM
    for n in range(LRU_BLOCKS):
        cols = slice(n * bd, (n + 1) * bd)
        xcn = xc[:, cols]
        xb = xcn.astype(BF16)
        gate_r = jax.nn.sigmoid(_dot(xb, wa_ref[n]) + ba_ref[:, cols])
        gate_i = jax.nn.sigmoid(_dot(xb, wx_ref[n]) + bx_ref[:, cols])
        log_a = (-LRU_C) * gate_r * sp[:, cols]
        a = jnp.exp(log_a)
        mult = jnp.sqrt(jnp.tanh(-log_a) * (a * a + 1.0))
        a = jnp.where(reset, 0.0, a)
        mult = jnp.where(reset, 1.0, mult)
        b = xcn * gate_i * mult
        s = 1
        while s < SUBLANES:
            a_ref[pad:pad + tb, cols] = a
            b_ref[pad:pad + tb, cols] = b
            a_sh = a_ref[pad - s:pad - s + tb, cols]
            b_sh = b_ref[pad - s:pad - s + tb, cols]
            b = a * b_sh + b
            a = a * a_sh
            s *= 2
        hg = hc_ref[:, cols]
        groups = []
        for g in range(tb // SUBLANES):
            rows = slice(g * SUBLANES, (g + 1) * SUBLANES)
            hg = a[rows, :] * hg + b[rows, :]
            groups.append(hg)
        h = jnp.concatenate(groups, axis=0)
        hc_ref[:, cols] = jnp.broadcast_to(hg[SUBLANES - 1:SUBLANES, :], (SUBLANES, bd))
        o_ref[:, cols] = (h * _gelu(y_ref[:, cols].astype(F32))).astype(o_ref.dtype)


def _rglru(z, pos_col, conv_w, conv_b, wa, ba, wx, bx, lam, tb):
    T = z.shape[0]
    W = D_MODEL
    row = lambda i: (0, 0)
    const3 = lambda i: (0, 0, 0)
    return pl.pallas_call(
        _rglru_kernel,
        grid=(T // tb,),
        in_specs=[pl.BlockSpec((tb, W), lambda i: (i, 3)),
                  pl.BlockSpec((tb, W), lambda i: (i, 4)),
                  pl.BlockSpec((tb, 1), lambda i: (i, 0)),
                  pl.BlockSpec((CONV_WIDTH, W), row),
                  pl.BlockSpec((1, W), row),
                  pl.BlockSpec((LRU_BLOCKS, LRU_BLOCK_DIM, LRU_BLOCK_DIM), const3),
                  pl.BlockSpec((1, W), row),
                  pl.BlockSpec((LRU_BLOCKS, LRU_BLOCK_DIM, LRU_BLOCK_DIM), const3),
                  pl.BlockSpec((1, W), row),
                  pl.BlockSpec((1, W), row)],
        out_specs=pl.BlockSpec((tb, W), lambda i: (i, 0)),
        out_shape=jax.ShapeDtypeStruct((T, W), BF16),
        scratch_shapes=[pltpu.VMEM((tb + 2 * SUBLANES, W), F32),
                        pltpu.VMEM((tb + SUBLANES, W), F32),
                        pltpu.VMEM((tb + SUBLANES, W), F32),
                        pltpu.VMEM((SUBLANES, W), F32)],
        compiler_params=_params("arbitrary"),
        name="rglru",
    )(z, z, pos_col, conv_w, conv_b, wa, ba, wx, bx, lam)


def _merge_kernel(x_ref, ret_ref, rnn_ref, ga_ref, gb_ref, wr_ref, wn_ref, wo_ref, o_ref, w_ref):
    @pl.when(pl.program_id(0) == 0)
    def _():
        for k, src in enumerate((wr_ref, wn_ref, wo_ref)):
            w_ref[k] = src[...].astype(BF16)

    p_ret = _dot(ret_ref[...], w_ref[0])
    p_rnn = _dot(rnn_ref[...], w_ref[1])
    merged = (jax.nn.sigmoid(ga_ref[...].astype(F32)) * p_ret
              + jax.nn.sigmoid(gb_ref[...].astype(F32)) * p_rnn)
    o_ref[...] = x_ref[...] + _dot(merged.astype(BF16), w_ref[2])


def _merge(x, ret, rnn, z, w_ret, w_rnn, w_out, layer, tm):
    T, D = x.shape
    tile = lambda i: (i, 0)
    wspec = pl.BlockSpec((None, D, D), lambda i: (layer, 0, 0))
    return pl.pallas_call(
        _merge_kernel,
        grid=(T // tm,),
        in_specs=[pl.BlockSpec((tm, D), tile), pl.BlockSpec((tm, D), tile), pl.BlockSpec((tm, D), tile),
                  pl.BlockSpec((tm, D), lambda i: (i, 5)),
                  pl.BlockSpec((tm, D), lambda i: (i, 6)),
                  wspec, wspec, wspec],
        out_specs=pl.BlockSpec((tm, D), tile),
        out_shape=jax.ShapeDtypeStruct((T, D), F32),
        scratch_shapes=[pltpu.VMEM((3, D, D), BF16)],
        compiler_params=_params("arbitrary"),
        name="merge",
    )(x, ret, rnn, z, z, w_ret, w_rnn, w_out)


def _xattn_kernel(x_ref, g_ref, wq_ref, kt_ref, v_ref, wo_ref, o_ref, att_ref, w_ref):
    @pl.when(pl.program_id(0) == 0)
    def _():
        w_ref[0] = wq_ref[...].astype(BF16)
        w_ref[1] = wo_ref[...].astype(BF16)

    x = x_ref[...]
    q = _dot(_rms(x, g_ref[...]).astype(BF16), w_ref[0])
    dh = XATTN_HEAD_DIM
    for h in range(XATTN_HEADS):
        cols = slice(h * dh, (h + 1) * dh)
        s = _dot(q[:, cols].astype(BF16), kt_ref[cols, :]) * (dh ** -0.5)
        s = s - jnp.max(s, axis=-1, keepdims=True)
        e = jnp.exp(s)
        p = e / jnp.sum(e, axis=-1, keepdims=True)
        att_ref[:, cols] = _dot(p.astype(BF16), v_ref[:, cols]).astype(BF16)
    o_ref[...] = x + _dot(att_ref[...], w_ref[1])


def _xattn(x, g, wq, kt, v, wo, layer, tm):
    T, D = x.shape
    M = v.shape[0]
    tile = lambda i: (i, 0)
    fixed = lambda i: (0, 0)
    wspec = pl.BlockSpec((None, D, D), lambda i: (layer, 0, 0))
    return pl.pallas_call(
        _xattn_kernel,
        grid=(T // tm,),
        in_specs=[pl.BlockSpec((tm, D), tile), pl.BlockSpec((1, D), fixed), wspec,
                  pl.BlockSpec((D, M), fixed), pl.BlockSpec((M, D), fixed), wspec],
        out_specs=pl.BlockSpec((tm, D), tile),
        out_shape=jax.ShapeDtypeStruct((T, D), F32),
        scratch_shapes=[pltpu.VMEM((tm, D), BF16), pltpu.VMEM((2, D, D), BF16)],
        compiler_params=_params("arbitrary"),
        name="xattn",
    )(x, g, wq, kt, v, wo)


def _peer_scores_kernel(x_ref, g_ref, wq_ref, k1_ref, k2_ref, hft_ref, s1_ref, s2_ref):
    hf = _rms(x_ref[...], g_ref[...])
    hft = jnp.transpose(hf).astype(BF16)
    hft_ref[...] = hft
    qt = _dot(wq_ref[...], hft)
    kd = 2 * PEER_HALF
    for h in range(PEER_HEADS):
        q1 = qt[h * kd:h * kd + PEER_HALF, :].astype(BF16)
        q2 = qt[h * kd + PEER_HALF:(h + 1) * kd, :].astype(BF16)
        s1_ref[h] = _dot(k1_ref[h], q1)
        s2_ref[h] = _dot(k2_ref[h], q2)


def _peer_scores(x, g, wq_t, k1, k2, tm):
    T, D = x.shape
    H, NK = PEER_HEADS, PEER_KEYS
    fixed = lambda i: (0, 0)
    const3 = lambda i: (0, 0, 0)
    return pl.pallas_call(
        _peer_scores_kernel,
        grid=(T // tm,),
        in_specs=[pl.BlockSpec((tm, D), lambda i: (i, 0)), pl.BlockSpec((1, D), fixed),
                  pl.BlockSpec(wq_t.shape, fixed),
                  pl.BlockSpec((H, NK, PEER_HALF), const3), pl.BlockSpec((H, NK, PEER_HALF), const3)],
        out_specs=[pl.BlockSpec((D, tm), lambda i: (0, i)),
                   pl.BlockSpec((H, NK, tm), lambda i: (0, 0, i)),
                   pl.BlockSpec((H, NK, tm), lambda i: (0, 0, i))],
        out_shape=[jax.ShapeDtypeStruct((D, T), BF16),
                   jax.ShapeDtypeStruct((H, NK, T), F32),
                   jax.ShapeDtypeStruct((H, NK, T), F32)],
        compiler_params=_params("parallel"),
        name="peer_scores",
    )(x, g, wq_t, k1, k2)


def _sort_pairs(n):
    pairs = []
    p = 1
    while p < n:
        k = p
        while k >= 1:
            for j in range(k % p, n - k, 2 * k):
                for i in range(min(k, n - j - k)):
                    if (i + j) // (2 * p) == (i + j + k) // (2 * p):
                        pairs.append((i + j, i + j + k))
            k //= 2
        p *= 2
    return pairs


def _sort_desc(vs):
    vs = list(vs)
    for lo, hi in _sort_pairs(len(vs)):
        a, b = vs[lo], vs[hi]
        vs[lo], vs[hi] = jnp.maximum(a, b), jnp.minimum(a, b)
    return vs


def _merge_top(a, b):
    n = len(a)
    c = [jnp.maximum(a[i], b[n - 1 - i]) for i in range(n)]
    k = n // 2
    while k >= 1:
        for i in range(n):
            if i & k == 0:
                x, y = c[i], c[i + k]
                c[i], c[i + k] = jnp.maximum(x, y), jnp.minimum(x, y)
        k //= 2
    return c


def _top_of_keys(groups, roll):
    vs = _sort_desc(groups)
    shift = SUBLANES // 2
    while shift >= 1:
        vs = _merge_top(vs, [roll(v, shift, 0) for v in vs])
        shift //= 2
    return vs


def _peer_select_math(s1, s2, roll):
    K = PEER_TOPK
    shape = s1[0][0].shape
    sub = lax.broadcasted_iota(jnp.int32, shape, 0)
    zero = jnp.zeros(shape, F32)
    tops2 = []
    v1 = [zero] * K
    v2 = [zero] * K
    for h in range(PEER_HEADS):
        t1 = _top_of_keys(s1[h], roll)
        t2 = _top_of_keys(s2[h], roll)
        tops2.append(t2)
        v1 = [jnp.where(sub == h, t, v) for t, v in zip(t1, v1)]
        v2 = [jnp.where(sub == h, t, v) for t, v in zip(t2, v2)]
    cands = [v1[i] + v2[j] for i in range(K) for j in range(K // (i + 1))]
    n = 1
    while n < len(cands):
        n *= 2
    cands = cands + [jnp.full(shape, -jnp.inf, F32)] * (n - len(cands))
    top = _sort_desc(cands)[:K]
    z = zero
    for t in top:
        z = z + jnp.exp(t - top[0])
    tau_all, zinv_all = top[K - 1], 1.0 / z
    out = []
    for h in range(PEER_HEADS):
        bc = lambda v: jnp.broadcast_to(v[h:h + 1, :], shape)
        tau, zinv, m1, m2 = bc(tau_all), bc(zinv_all), bc(v1[0]), bc(v2[0])
        t2 = tops2[h]
        cnt, p1, rk, e2 = [], [], [], []
        c_top = zero
        for j in range(K // 2, K):
            c_top = jnp.where(m1 + t2[j] >= tau, float(j + 1), c_top)
        for s1g, s2g in zip(s1[h], s2[h]):
            c = zero
            r = zero
            for j in range(K // 2):
                c = jnp.where(s1g + t2[j] >= tau, float(j + 1), c)
            c = jnp.where(s1g == m1, jnp.maximum(c, c_top), c)
            for j in range(K):
                r = jnp.where(t2[j] > s2g, float(j + 1), r)
            cnt.append(c)
            rk.append(r)
            p1.append(jnp.exp(s1g - m1) * zinv)
            e2.append(jnp.exp(s2g - m2))
        out.append((cnt, p1, rk, e2))
    return out


def _bf16_pair(x):
    u = lax.bitcast_convert_type(x.astype(BF16).astype(F32), jnp.uint32)
    return u | (u >> 16)


def _peer_select_kernel(s1_ref, s2_ref, n_ref, p1_ref, r2_ref, e2_ref):
    ng = PEER_KEYS // SUBLANES
    grp = lambda ref, h: [ref[h, i * SUBLANES:(i + 1) * SUBLANES, :] for i in range(ng)]
    s1 = [grp(s1_ref, h) for h in range(PEER_HEADS)]
    s2 = [grp(s2_ref, h) for h in range(PEER_HEADS)]
    res = _peer_select_math(s1, s2, pltpu.roll)
    pk = 2 * SUBLANES
    for h, (cnt, p1, rk, e2) in enumerate(res):
        for i in range(ng):
            rows = slice(i * SUBLANES, (i + 1) * SUBLANES)
            n_ref[h, rows, :] = _bf16_pair(cnt[i])
            p1_ref[h, rows, :] = _bf16_pair(p1[i])
        for i in range(ng // 2):
            rows = slice(i * pk, (i + 1) * pk)
            r2_ref[h, rows, :] = jnp.concatenate([rk[2 * i], rk[2 * i + 1]], axis=0).astype(BF16)
            e2_ref[h, rows, :] = jnp.concatenate([e2[2 * i], e2[2 * i + 1]], axis=0).astype(BF16)


def _peer_select(s1, s2, tl):
    H, NK, T = s1.shape
    spec = pl.BlockSpec((H, NK, tl), lambda i: (0, 0, i))
    return pl.pallas_call(
        _peer_select_kernel,
        grid=(T // tl,),
        in_specs=[spec] * 2,
        out_specs=[spec] * 4,
        out_shape=[jax.ShapeDtypeStruct((H, NK, T), jnp.uint32), jax.ShapeDtypeStruct((H, NK, T), jnp.uint32),
                   jax.ShapeDtypeStruct((H, NK, T), BF16), jax.ShapeDtypeStruct((H, NK, T), BF16)],
        compiler_params=_params("parallel"),
        name="peer_select",
    )(s1, s2)


def _peer_gate_rows(act, a, n_ref, p1_ref, r2_ref, e2_ref, w_ref, row0):
    NK = PEER_KEYS
    tm = act.shape[-1]
    pk = 2 * SUBLANES
    zero = jnp.zeros((pk, tm), BF16)
    bc = lambda ref, h: pltpu.bitcast(jnp.broadcast_to(ref[h, pl.ds(a, 1), :], (SUBLANES, tm)), BF16)
    nrow = [bc(n_ref, h) for h in range(PEER_HEADS)]
    prow = [bc(p1_ref, h) for h in range(PEER_HEADS)]
    for g in range(NK // pk):
        b_rows = slice(g * pk, (g + 1) * pk)
        gate = None
        for h in range(PEER_HEADS):
            sel = r2_ref[h, b_rows, :] < nrow[h]
            term = jnp.where(sel, prow[h] * e2_ref[h, b_rows, :], zero)
            gate = term if gate is None else gate + term
        w_ref[row0 + g * pk:row0 + (g + 1) * pk, :] = gate * act[b_rows, :]


def _peer_half(a0, hft, keys, u_ref, w_ref, vt_prev_ref, w_prev_ref, acc_ref, gl_ref, na):
    NK = PEER_KEYS
    gl = _gelu(_dot(u_ref[...], hft).astype(BF16))
    for k in range(na):
        gl_ref[k] = gl[k * NK:(k + 1) * NK]
    acc_ref[...] += _dot(vt_prev_ref[...], w_prev_ref[...])
    for ai in range(na):
        _peer_gate_rows(gl_ref.at[ai], a0 + ai, *keys, w_ref, ai * NK)


def _peer_dense_kernel(x_ref, hft_ref, n_ref, p1_ref, r2_ref, e2_ref, ua_ref, ub_ref, vta_ref, vtb_ref,
                       vtl_ref, *rest, na, norm_out):
    g_ref = rest[0] if norm_out else None
    o_ref, acc_ref, wa_ref, wb_ref, gla_ref, glb_ref = rest[-6:]
    j = pl.program_id(1)
    last = pl.num_programs(1) - 1
    keys = (n_ref, p1_ref, r2_ref, e2_ref)

    @pl.when(j == 0)
    def _():
        acc_ref[...] = jnp.zeros_like(acc_ref)
        wb_ref[...] = jnp.zeros_like(wb_ref)

    hft = hft_ref[...]
    _peer_half(2 * j * na, hft, keys, ua_ref, wa_ref, vtb_ref, wb_ref, acc_ref, gla_ref, na)
    _peer_half((2 * j + 1) * na, hft, keys, ub_ref, wb_ref, vta_ref, wa_ref, acc_ref, glb_ref, na)

    @pl.when(j == last)
    def _():
        acc = acc_ref[...] + _dot(vtl_ref[...], wb_ref[...])
        xo = x_ref[...] + jnp.transpose(acc)
        o_ref[...] = _rms(xo, g_ref[...]) if norm_out else xo


def _peer_dense(x, hft, n1, p1, r2, e2, u, vt, layer, tm, na, g_out=None):
    T, D = x.shape
    H, NK = PEER_HEADS, PEER_KEYS
    th = na * NK
    steps = NK // (2 * na)
    kspec = pl.BlockSpec((H, NK, tm), lambda i, j: (0, 0, i))
    return pl.pallas_call(
        functools.partial(_peer_dense_kernel, na=na, norm_out=g_out is not None),
        grid=(T // tm, steps),
        in_specs=[pl.BlockSpec((tm, D), lambda i, j: (i, 0)),
                  pl.BlockSpec((D, tm), lambda i, j: (0, i)),
                  kspec, kspec, kspec, kspec,
                  pl.BlockSpec((None, th, D), lambda i, j: (layer, 2 * j, 0)),
                  pl.BlockSpec((None, th, D), lambda i, j: (layer, 2 * j + 1, 0)),
                  pl.BlockSpec((None, None, D, th), lambda i, j: (layer, 2 * j, 0, 0)),
                  pl.BlockSpec((None, None, D, th), lambda i, j: (layer, jnp.maximum(2 * j - 1, 1), 0, 0)),
                  pl.BlockSpec((None, None, D, th), lambda i, j: (layer, 2 * steps - 1, 0, 0))]
                 + ([pl.BlockSpec((1, D), lambda i, j: (0, 0))] if g_out is not None else []),
        out_specs=pl.BlockSpec((tm, D), lambda i, j: (i, 0)),
        out_shape=jax.ShapeDtypeStruct((T, D), F32),
        scratch_shapes=[pltpu.VMEM((D, tm), F32), pltpu.VMEM((th, tm), BF16), pltpu.VMEM((th, tm), BF16),
                        pltpu.VMEM((na, NK, tm), BF16), pltpu.VMEM((na, NK, tm), BF16)],
        compiler_params=_params("parallel", "arbitrary"),
        name="peer_dense",
    )(x, hft, n1, p1, r2, e2, u, u, vt, vt, vt, *(() if g_out is None else (g_out,)))


def kernel(x, mem, positions, g_mix, w_in, w_ret_br, w_rnn_br, w_mix_out, conv_w, conv_b, lru_wa, lru_ba, lru_wx, lru_bx, lru_lam, g_x, g_mem, w_xq, w_xk, w_xv, w_xo, g_ffn, w_pq, sub_k1, sub_k2, peer_u, peer_v, g_final):
    B, T, D = x.shape
    assert B == 1 and D == D_MODEL
    depth = g_mix.shape[0]
    tm = min(TM_PROJ, T)
    xs = x.reshape(T, D)
    mems = mem.reshape(mem.shape[1], D)
    pos_col = positions.reshape(T, 1)
    row = lambda a: a.reshape(1, -1)

    cos2, sin2 = _rope_tables(pos_col, tm)
    u_all = peer_u.astype(BF16)
    th = NA_PEER * PEER_KEYS
    vt_all = jnp.swapaxes(peer_v.reshape(depth, -1, th, D), 2, 3).astype(BF16)
    for l in range(depth):
        z = _rms_matmul(xs, row(g_mix[l]), w_in, l, tm=min(TM_IN, T), tn=TN_PROJ, out_dtype=BF16)
        ret = _retention(z, cos2, sin2, min(TB_RET, T))
        rnn = _rglru(z, pos_col, conv_w[l], row(conv_b[l]), lru_wa[l].astype(BF16), row(lru_ba[l]),
                     lru_wx[l].astype(BF16), row(lru_bx[l]), row(lru_lam[l]), min(TB_LRU, T))
        xs = _merge(xs, ret, rnn, z, w_ret_br, w_rnn_br, w_mix_out, l, tm)
        kmem = _rms_matmul(mems, row(g_mem[l]), w_xk, l, tm=mems.shape[0], tn=D, out_dtype=BF16)
        vmem = _rms_matmul(mems, row(g_mem[l]), w_xv, l, tm=mems.shape[0], tn=D, out_dtype=BF16)
        xs = _xattn(xs, row(g_x[l]), w_xq, kmem.T, vmem, w_xo, l, tm)
        tp = min(TM_PEER, T)
        hft, s1, s2 = _peer_scores(xs, row(g_ffn[l]), w_pq[l].T.astype(BF16), sub_k1[l].astype(BF16),
                                   sub_k2[l].astype(BF16), tp)
        n1, p1, r2, e2 = _peer_select(s1, s2, TL_SORT)
        xs = _peer_dense(xs, hft, n1, p1, r2, e2, u_all, vt_all, l, tp, NA_PEER,
                         g_out=row(g_final) if l == depth - 1 else None)
    return xs.reshape(B, T, D)
```

```python
import functools
import math

import numpy as np
import jax
import jax.numpy as jnp
from jax import lax
from jax.experimental import pallas as pl
from jax.experimental.pallas import tpu as pltpu

F32 = jnp.float32
BF16 = jnp.bfloat16

D_MODEL = 1024
EPS = 1e-6
RET_HEADS = 4
RET_QK_DIM = 128
RET_V_DIM = 256
RET_CHUNK = 128
ROPE_BASE = 10000.0
LRU_BLOCKS = 8
LRU_BLOCK_DIM = D_MODEL // LRU_BLOCKS
CONV_WIDTH = 4
LRU_C = 8.0
XATTN_HEADS = 4
XATTN_HEAD_DIM = D_MODEL // XATTN_HEADS
PEER_HEADS = 8
PEER_KEYS = 128
PEER_TOPK = 16
PEER_HALF = 128

SUBLANES = 8
VMEM_LIMIT_BYTES = 56 * 1024 * 1024

TM_PROJ = 512
TM_IN = 2048
TN_PROJ = 1024
TB_RET = 256
TB_LRU = 512
TM_PEER = 512
NA_PEER = 4
TL_SORT = 256


def _params(*sem):
    return pltpu.CompilerParams(dimension_semantics=sem, vmem_limit_bytes=VMEM_LIMIT_BYTES)


def _rms(x, g):
    return x * lax.rsqrt(jnp.mean(x * x, axis=-1, keepdims=True) + EPS) * g


def _gelu(x):
    c = math.sqrt(2.0 / math.pi)
    return 0.5 * x * (1.0 + jnp.tanh(c * (x + 0.044715 * (x * x * x))))


def _dot(a, b):
    return jnp.dot(a, b, preferred_element_type=F32)


def _dot_nt(a, b):
    return lax.dot_general(a, b, (((1,), (1,)), ((), ())), preferred_element_type=F32)


def _rms_matmul_kernel(x_ref, g_ref, w_ref, o_ref, h_ref):
    @pl.when(pl.program_id(1) == 0)
    def _():
        h_ref[...] = _rms(x_ref[...], g_ref[...]).astype(BF16)

    o_ref[...] = _dot(h_ref[...], w_ref[...].astype(BF16)).astype(o_ref.dtype)


def _rms_matmul(x, g, w, layer, *, tm, tn, out_dtype):
    T, K = x.shape
    N = w.shape[2]
    return pl.pallas_call(
        _rms_matmul_kernel,
        grid=(T // tm, N // tn),
        in_specs=[pl.BlockSpec((tm, K), lambda i, j: (i, 0)),
                  pl.BlockSpec((1, K), lambda i, j: (0, 0)),
                  pl.BlockSpec((None, K, tn), lambda i, j: (layer, 0, j))],
        out_specs=pl.BlockSpec((tm, tn), lambda i, j: (i, j)),
        out_shape=jax.ShapeDtypeStruct((T, N), out_dtype),
        scratch_shapes=[pltpu.VMEM((tm, K), BF16)],
        compiler_params=_params("parallel", "arbitrary"),
        name="rms_matmul",
    )(x, g, w)


def _rope_kernel(pos_ref, invf_ref, cos_ref, sin_ref):
    ang = pos_ref[...].astype(F32) * invf_ref[...]
    lane = lax.broadcasted_iota(jnp.int32, ang.shape, 1)
    s = jnp.sin(ang)
    cos_ref[...] = jnp.cos(ang)
    sin_ref[...] = jnp.where(lane < RET_QK_DIM // 2, -s, s)


def _rope_tables(pos_col, tb):
    T = pos_col.shape[0]
    half = RET_QK_DIM // 2
    inv_freq = ROPE_BASE ** (-jnp.arange(0, RET_QK_DIM, 2, dtype=F32) / RET_QK_DIM)
    invf2 = jnp.concatenate([inv_freq, inv_freq]).reshape(1, 2 * half)
    return pl.pallas_call(
        _rope_kernel,
        grid=(T // tb,),
        in_specs=[pl.BlockSpec((tb, 1), lambda i: (i, 0)),
                  pl.BlockSpec((1, RET_QK_DIM), lambda i: (0, 0))],
        out_specs=[pl.BlockSpec((tb, RET_QK_DIM), lambda i: (i, 0))] * 2,
        out_shape=[jax.ShapeDtypeStruct((T, RET_QK_DIM), F32)] * 2,
        compiler_params=_params("parallel"),
        name="rope_tables",
    )(pos_col, invf2)


def _retention_kernel(q_ref, k_ref, v_ref, g_ref, cos_ref, sin_ref, dec_ref, kdec_ref, qdec_ref,
                      o_ref, s_ref, *, chunk_decay):
    C, dk, dv = RET_CHUNK, RET_QK_DIM, RET_V_DIM

    @pl.when(pl.program_id(0) == 0)
    def _():
        s_ref[...] = jnp.zeros_like(s_ref)

    kscale = dk ** -0.5
    for c in range(q_ref.shape[0] // C):
        rows = slice(c * C, (c + 1) * C)
        cos = cos_ref[rows, :]
        sin = sin_ref[rows, :]
        for h in range(RET_HEADS):
            q = q_ref[rows, h * dk:(h + 1) * dk].astype(F32)
            k = k_ref[rows, h * dk:(h + 1) * dk].astype(F32)
            v = v_ref[rows, h * dv:(h + 1) * dv].astype(BF16)
            qr = q * cos + pltpu.roll(q, dk // 2, 1) * sin
            kr = (k * cos + pltpu.roll(k, dk // 2, 1) * sin) * kscale
            scores = _dot_nt(qr.astype(BF16), kr.astype(BF16)) * dec_ref[h]
            inner = _dot(scores.astype(BF16), v)
            state = s_ref[h]
            cross = _dot((qr * qdec_ref[h]).astype(BF16), state.astype(BF16))
            kd_t = jnp.transpose(kr * kdec_ref[h]).astype(BF16)
            s_ref[h] = chunk_decay[h] * state + _dot(kd_t, v)
            o = inner + cross
            mu = jnp.mean(o, axis=-1, keepdims=True)
            oc = o - mu
            var = jnp.mean(oc * oc, axis=-1, keepdims=True)
            o = oc * lax.rsqrt(var + EPS)
            g = g_ref[rows, h * dv:(h + 1) * dv].astype(F32)
            o_ref[rows, h * dv:(h + 1) * dv] = (g * jax.nn.sigmoid(g) * o).astype(o_ref.dtype)


def _retention(z, cos2, sin2, tb):
    T = z.shape[0]
    H, C, dk, dv = RET_HEADS, RET_CHUNK, RET_QK_DIM, RET_V_DIM
    log_g = jnp.log(1.0 - 2.0 ** (-5.0 - jnp.arange(H, dtype=F32)))
    i = jnp.arange(C, dtype=F32)
    diff = i[:, None] - i[None, :]
    decay = jnp.where(diff[None] >= 0, jnp.exp(diff[None] * log_g[:, None, None]), 0.0)
    k_decay = jnp.exp((C - 1.0 - i)[None, :] * log_g[:, None])
    q_decay = jnp.exp((i + 1.0)[None, :] * log_g[:, None])
    kdec = jnp.broadcast_to(k_decay[:, :, None], (H, C, dk))
    qdec = jnp.broadcast_to(q_decay[:, :, None], (H, C, dk))
    log_g_np = np.log(1.0 - 2.0 ** (-5.0 - np.arange(H, dtype=np.float32))).astype(np.float32)
    chunk_decay = tuple(float(np.exp(np.float32(C) * lg)) for lg in log_g_np)
    nq = H * dk
    const3 = lambda i: (0, 0, 0)
    return pl.pallas_call(
        functools.partial(_retention_kernel, chunk_decay=chunk_decay),
        grid=(T // tb,),
        in_specs=[pl.BlockSpec((tb, nq), lambda i: (i, 0)),
                  pl.BlockSpec((tb, nq), lambda i: (i, 1)),
                  pl.BlockSpec((tb, H * dv), lambda i: (i, 1)),
                  pl.BlockSpec((tb, H * dv), lambda i: (i, 2)),
                  pl.BlockSpec((tb, dk), lambda i: (i, 0)),
                  pl.BlockSpec((tb, dk), lambda i: (i, 0)),
                  pl.BlockSpec((H, C, C), const3),
                  pl.BlockSpec((H, C, dk), const3),
                  pl.BlockSpec((H, C, dk), const3)],
        out_specs=pl.BlockSpec((tb, H * dv), lambda i: (i, 0)),
        out_shape=jax.ShapeDtypeStruct((T, H * dv), BF16),
        scratch_shapes=[pltpu.VMEM((H, dk, dv), F32)],
        compiler_params=_params("arbitrary"),
        name="retention",
    )(z, z, z, z, cos2, sin2, decay, kdec, qdec)


def _rglru_kernel(x_ref, y_ref, pos_ref, cw_ref, cb_ref, wa_ref, ba_ref, wx_ref, bx_ref, lam_ref,
                  o_ref, xpad_ref, a_ref, b_ref, hc_ref):
    tb = x_ref.shape[0]
    pad = SUBLANES
    halo = SUBLANES

    @pl.when(pl.program_id(0) == 0)
    def _():
        xpad_ref[0:halo, :] = jnp.zeros((halo, D_MODEL), F32)
        hc_ref[...] = jnp.zeros_like(hc_ref)
        a_ref[0:pad, :] = jnp.ones((pad, D_MODEL), F32)
        b_ref[0:pad, :] = jnp.zeros((pad, D_MODEL), F32)

    xpad_ref[halo:halo + tb, :] = x_ref[...].astype(F32)
    xc = cb_ref[...] + cw_ref[3:4, :] * xpad_ref[halo:halo + tb, :]
    for w in range(CONV_WIDTH - 1):
        off = halo - (CONV_WIDTH - 1) + w
        xc = xc + cw_ref[w:w + 1, :] * xpad_ref[off:off + tb, :]
    xpad_ref[0:halo, :] = xpad_ref[tb:tb + halo, :]

    reset = pos_ref[...] == 0
    lam = lam_ref[...]
    nl = -lam
    sp = jnp.maximum(nl, 0.0) + jnp.log(1.0 + jnp.exp(-jnp.abs(nl)))
    bd = LRU_BLOCK_DIM
    for n in range(LRU_BLOCKS):
        cols = slice(n * bd, (n + 1) * bd)
        xcn = xc[:, cols]
        xb = xcn.astype(BF16)
        gate_r = jax.nn.sigmoid(_dot(xb, wa_ref[n]) + ba_ref[:, cols])
        gate_i = jax.nn.sigmoid(_dot(xb, wx_ref[n]) + bx_ref[:, cols])
        log_a = (-LRU_C) * gate_r * sp[:, cols]
        a = jnp.exp(log_a)
        mult = jnp.sqrt(jnp.tanh(-log_a) * (a * a + 1.0))
        a = jnp.where(reset, 0.0, a)
        mult = jnp.where(reset, 1.0, mult)
        b = xcn * gate_i * mult
        s = 1
        while s < SUBLANES:
            a_ref[pad:pad + tb, cols] = a
            b_ref[pad:pad + tb, cols] = b
            a_sh = a_ref[pad - s:pad - s + tb, cols]
            b_sh = b_ref[pad - s:pad - s + tb, cols]
            b = a * b_sh + b
            a = a * a_sh
            s *= 2
        hg = hc_ref[:, cols]
        groups = []
        for g in range(tb // SUBLANES):
            rows = slice(g * SUBLANES, (g + 1) * SUBLANES)
            hg = a[rows, :] * hg + b[rows, :]
            groups.append(hg)
        h = jnp.concatenate(groups, axis=0)
        hc_ref[:, cols] = jnp.broadcast_to(hg[SUBLANES - 1:SUBLANES, :], (SUBLANES, bd))
        o_ref[:, cols] = (h * _gelu(y_ref[:, cols].astype(F32))).astype(o_ref.dtype)


def _rglru(z, pos_col, conv_w, conv_b, wa, ba, wx, bx, lam, tb):
    T = z.shape[0]
    W = D_MODEL
    row = lambda i: (0, 0)
    const3 = lambda i: (0, 0, 0)
    return pl.pallas_call(
        _rglru_kernel,
        grid=(T // tb,),
        in_specs=[pl.BlockSpec((tb, W), lambda i: (i, 3)),
                  pl.BlockSpec((tb, W), lambda i: (i, 4)),
                  pl.BlockSpec((tb, 1), lambda i: (i, 0)),
                  pl.BlockSpec((CONV_WIDTH, W), row),
                  pl.BlockSpec((1, W), row),
                  pl.BlockSpec((LRU_BLOCKS, LRU_BLOCK_DIM, LRU_BLOCK_DIM), const3),
                  pl.BlockSpec((1, W), row),
                  pl.BlockSpec((LRU_BLOCKS, LRU_BLOCK_DIM, LRU_BLOCK_DIM), const3),
                  pl.BlockSpec((1, W), row),
                  pl.BlockSpec((1, W), row)],
        out_specs=pl.BlockSpec((tb, W), lambda i: (i, 0)),
        out_shape=jax.ShapeDtypeStruct((T, W), BF16),
        scratch_shapes=[pltpu.VMEM((tb + 2 * SUBLANES, W), F32),
                        pltpu.VMEM((tb + SUBLANES, W), F32),
                        pltpu.VMEM((tb + SUBLANES, W), F32),
                        pltpu.VMEM((SUBLANES, W), F32)],
        compiler_params=_params("arbitrary"),
        name="rglru",
    )(z, z, pos_col, conv_w, conv_b, wa, ba, wx, bx, lam)


def _merge_kernel(x_ref, ret_ref, rnn_ref, ga_ref, gb_ref, wr_ref, wn_ref, wo_ref, o_ref, w_ref):
    @pl.when(pl.program_id(0) == 0)
    def _():
        for k, src in enumerate((wr_ref, wn_ref, wo_ref)):
            w_ref[k] = src[...].astype(BF16)

    p_ret = _dot(ret_ref[...], w_ref[0])
    p_rnn = _dot(rnn_ref[...], w_ref[1])
    merged = (jax.nn.sigmoid(ga_ref[...].astype(F32)) * p_ret
              + jax.nn.sigmoid(gb_ref[...].astype(F32)) * p_rnn)
    o_ref[...] = x_ref[...] + _dot(merged.astype(BF16), w_ref[2])


def _merge(x, ret, rnn, z, w_ret, w_rnn, w_out, layer, tm):
    T, D = x.shape
    tile = lambda i: (i, 0)
    wspec = pl.BlockSpec((None, D, D), lambda i: (layer, 0, 0))
    return pl.pallas_call(
        _merge_kernel,
        grid=(T // tm,),
        in_specs=[pl.BlockSpec((tm, D), tile), pl.BlockSpec((tm, D), tile), pl.BlockSpec((tm, D), tile),
                  pl.BlockSpec((tm, D), lambda i: (i, 5)),
                  pl.BlockSpec((tm, D), lambda i: (i, 6)),
                  wspec, wspec, wspec],
        out_specs=pl.BlockSpec((tm, D), tile),
        out_shape=jax.ShapeDtypeStruct((T, D), F32),
        scratch_shapes=[pltpu.VMEM((3, D, D), BF16)],
        compiler_params=_params("arbitrary"),
        name="merge",
    )(x, ret, rnn, z, z, w_ret, w_rnn, w_out)


def _xattn_kernel(x_ref, g_ref, wq_ref, kt_ref, v_ref, wo_ref, o_ref, att_ref, w_ref):
    @pl.when(pl.program_id(0) == 0)
    def _():
        w_ref[0] = wq_ref[...].astype(BF16)
        w_ref[1] = wo_ref[...].astype(BF16)

    x = x_ref[...]
    q = _dot(_rms(x, g_ref[...]).astype(BF16), w_ref[0])
    dh = XATTN_HEAD_DIM
    for h in range(XATTN_HEADS):
        cols = slice(h * dh, (h + 1) * dh)
        s = _dot(q[:, cols].astype(BF16), kt_ref[cols, :]) * (dh ** -0.5)
        s = s - jnp.max(s, axis=-1, keepdims=True)
        e = jnp.exp(s)
        p = e / jnp.sum(e, axis=-1, keepdims=True)
        att_ref[:, cols] = _dot(p.astype(BF16), v_ref[:, cols]).astype(BF16)
    o_ref[...] = x + _dot(att_ref[...], w_ref[1])


def _xattn(x, g, wq, kt, v, wo, layer, tm):
    T, D = x.shape
    M = v.shape[0]
    tile = lambda i: (i, 0)
    fixed = lambda i: (0, 0)
    wspec = pl.BlockSpec((None, D, D), lambda i: (layer, 0, 0))
    return pl.pallas_call(
        _xattn_kernel,
        grid=(T // tm,),
        in_specs=[pl.BlockSpec((tm, D), tile), pl.BlockSpec((1, D), fixed), wspec,
                  pl.BlockSpec((D, M), fixed), pl.BlockSpec((M, D), fixed), wspec],
        out_specs=pl.BlockSpec((tm, D), tile),
        out_shape=jax.ShapeDtypeStruct((T, D), F32),
        scratch_shapes=[pltpu.VMEM((tm, D), BF16), pltpu.VMEM((2, D, D), BF16)],
        compiler_params=_params("arbitrary"),
        name="xattn",
    )(x, g, wq, kt, v, wo)


def _peer_scores_kernel(x_ref, g_ref, wq_ref, k1_ref, k2_ref, hft_ref, s1_ref, s2_ref):
    hf = _rms(x_ref[...], g_ref[...])
    hft = jnp.transpose(hf).astype(BF16)
    hft_ref[...] = hft
    qt = _dot(wq_ref[...], hft)
    kd = 2 * PEER_HALF
    for h in range(PEER_HEADS):
        q1 = qt[h * kd:h * kd + PEER_HALF, :].astype(BF16)
        q2 = qt[h * kd + PEER_HALF:(h + 1) * kd, :].astype(BF16)
        s1_ref[h] = _dot(k1_ref[h], q1)
        s2_ref[h] = _dot(k2_ref[h], q2)


def _peer_scores(x, g, wq_t, k1, k2, tm):
    T, D = x.shape
    H, NK = PEER_HEADS, PEER_KEYS
    fixed = lambda i: (0, 0)
    const3 = lambda i: (0, 0, 0)
    return pl.pallas_call(
        _peer_scores_kernel,
        grid=(T // tm,),
        in_specs=[pl.BlockSpec((tm, D), lambda i: (i, 0)), pl.BlockSpec((1, D), fixed),
                  pl.BlockSpec(wq_t.shape, fixed),
                  pl.BlockSpec((H, NK, PEER_HALF), const3), pl.BlockSpec((H, NK, PEER_HALF), const3)],
        out_specs=[pl.BlockSpec((D, tm), lambda i: (0, i)),
                   pl.BlockSpec((H, NK, tm), lambda i: (0, 0, i)),
                   pl.BlockSpec((H, NK, tm), lambda i: (0, 0, i))],
        out_shape=[jax.ShapeDtypeStruct((D, T), BF16),
                   jax.ShapeDtypeStruct((H, NK, T), F32),
                   jax.ShapeDtypeStruct((H, NK, T), F32)],
        compiler_params=_params("parallel"),
        name="peer_scores",
    )(x, g, wq_t, k1, k2)


def _sort_pairs(n):
    pairs = []
    p = 1
    while p < n:
        k = p
        while k >= 1:
            for j in range(k % p, n - k, 2 * k):
                for i in range(min(k, n - j - k)):
                    if (i + j) // (2 * p) == (i + j + k) // (2 * p):
                        pairs.append((i + j, i + j + k))
            k //= 2
        p *= 2
    return pairs


def _sort_desc(vs):
    vs = list(vs)
    for lo, hi in _sort_pairs(len(vs)):
        a, b = vs[lo], vs[hi]
        vs[lo], vs[hi] = jnp.maximum(a, b), jnp.minimum(a, b)
    return vs


def _merge_top(a, b):
    n = len(a)
    c = [jnp.maximum(a[i], b[n - 1 - i]) for i in range(n)]
    k = n // 2
    while k >= 1:
        for i in range(n):
            if i & k == 0:
                x, y = c[i], c[i + k]
                c[i], c[i + k] = jnp.maximum(x, y), jnp.minimum(x, y)
        k //= 2
    return c


def _top_of_keys(groups, roll):
    vs = _sort_desc(groups)
    shift = SUBLANES // 2
    while shift >= 1:
        vs = _merge_top(vs, [roll(v, shift, 0) for v in vs])
        shift //= 2
    return vs


def _peer_select_math(s1, s2, roll):
    K = PEER_TOPK
    shape = s1[0][0].shape
    sub = lax.broadcasted_iota(jnp.int32, shape, 0)
    zero = jnp.zeros(shape, F32)
    tops2 = []
    v1 = [zero] * K
    v2 = [zero] * K
    for h in range(PEER_HEADS):
        t1 = _top_of_keys(s1[h], roll)
        t2 = _top_of_keys(s2[h], roll)
        tops2.append(t2)
        v1 = [jnp.where(sub == h, t, v) for t, v in zip(t1, v1)]
        v2 = [jnp.where(sub == h, t, v) for t, v in zip(t2, v2)]
    cands = [v1[i] + v2[j] for i in range(K) for j in range(K // (i + 1))]
    n = 1
    while n < len(cands):
        n *= 2
    cands = cands + [jnp.full(shape, -jnp.inf, F32)] * (n - len(cands))
    top = _sort_desc(cands)[:K]
    z = zero
    for t in top:
        z = z + jnp.exp(t - top[0])
    tau_all, zinv_all = top[K - 1], 1.0 / z
    out = []
    for h in range(PEER_HEADS):
        bc = lambda v: jnp.broadcast_to(v[h:h + 1, :], shape)
        tau, zinv, m1, m2 = bc(tau_all), bc(zinv_all), bc(v1[0]), bc(v2[0])
        t2 = tops2[h]
        cnt, p1, rk, e2 = [], [], [], []
        c_top = zero
        for j in range(K // 2, K):
            c_top = jnp.where(m1 + t2[j] >= tau, float(j + 1), c_top)
        for s1g, s2g in zip(s1[h], s2[h]):
            c = zero
            r = zero
            for j in range(K // 2):
                c = jnp.where(s1g + t2[j] >= tau, float(j + 1), c)
            c = jnp.where(s1g == m1, jnp.maximum(c, c_top), c)
            for j in range(K):
                r = jnp.where(t2[j] > s2g, float(j + 1), r)
            cnt.append(c)
            rk.append(r)
            p1.append(jnp.exp(s1g - m1) * zinv)
            e2.append(jnp.exp(s2g - m2))
        out.append((cnt, p1, rk, e2))
    return out


def _bf16_pair(x):
    u = lax.bitcast_convert_type(x.astype(BF16).astype(F32), jnp.uint32)
    return u | (u >> 16)


def _peer_select_kernel(s1_ref, s2_ref, n_ref, p1_ref, r2_ref, e2_ref):
    ng = PEER_KEYS // SUBLANES
    grp = lambda ref, h: [ref[h, i * SUBLANES:(i + 1) * SUBLANES, :] for i in range(ng)]
    s1 = [grp(s1_ref, h) for h in range(PEER_HEADS)]
    s2 = [grp(s2_ref, h) for h in range(PEER_HEADS)]
    res = _peer_select_math(s1, s2, pltpu.roll)
    pk = 2 * SUBLANES
    for h, (cnt, p1, rk, e2) in enumerate(res):
        for i in range(ng):
            rows = slice(i * SUBLANES, (i + 1) * SUBLANES)
            n_ref[h, rows, :] = _bf16_pair(cnt[i])
            p1_ref[h, rows, :] = _bf16_pair(p1[i])
        for i in range(ng // 2):
            rows = slice(i * pk, (i + 1) * pk)
            r2_ref[h, rows, :] = jnp.concatenate([rk[2 * i], rk[2 * i + 1]], axis=0).astype(BF16)
            e2_ref[h, rows, :] = jnp.concatenate([e2[2 * i], e2[2 * i + 1]], axis=0).astype(BF16)


def _peer_select(s1, s2, tl):
    H, NK, T = s1.shape
    spec = pl.BlockSpec((H, NK, tl), lambda i: (0, 0, i))
    return pl.pallas_call(
        _peer_select_kernel,
        grid=(T // tl,),
        in_specs=[spec] * 2,
        out_specs=[spec] * 4,
        out_shape=[jax.ShapeDtypeStruct((H, NK, T), jnp.uint32), jax.ShapeDtypeStruct((H, NK, T), jnp.uint32),
                   jax.ShapeDtypeStruct((H, NK, T), BF16), jax.ShapeDtypeStruct((H, NK, T), BF16)],
        compiler_params=_params("parallel"),
        name="peer_select",
    )(s1, s2)


def _peer_gate_rows(act, a, n_ref, p1_ref, r2_ref, e2_ref, w_ref, row0):
    NK = PEER_KEYS
    tm = act.shape[-1]
    pk = 2 * SUBLANES
    zero = jnp.zeros((pk, tm), BF16)
    bc = lambda ref, h: pltpu.bitcast(jnp.broadcast_to(ref[h, pl.ds(a, 1), :], (SUBLANES, tm)), BF16)
    nrow = [bc(n_ref, h) for h in range(PEER_HEADS)]
    prow = [bc(p1_ref, h) for h in range(PEER_HEADS)]
    for g in range(NK // pk):
        b_rows = slice(g * pk, (g + 1) * pk)
        gate = None
        for h in range(PEER_HEADS):
            sel = r2_ref[h, b_rows, :] < nrow[h]
            term = jnp.where(sel, prow[h] * e2_ref[h, b_rows, :], zero)
            gate = term if gate is None else gate + term
        w_ref[row0 + g * pk:row0 + (g + 1) * pk, :] = gate * act[b_rows, :]


def _peer_half(a0, hft, keys, u_ref, w_ref, vt_prev_ref, w_prev_ref, acc_ref, na):
    NK = PEER_KEYS
    gl = _gelu(_dot(u_ref[...], hft).astype(BF16))
    acc_ref[...] += _dot(vt_prev_ref[...], w_prev_ref[...])
    for ai in range(na):
        _peer_gate_rows(gl[ai * NK:(ai + 1) * NK], a0 + ai, *keys, w_ref, ai * NK)


def _peer_dense_kernel(x_ref, hft_ref, n_ref, p1_ref, r2_ref, e2_ref, ua_ref, ub_ref, vta_ref, vtb_ref,
                       vtl_ref, *rest, na, norm_out):
    g_ref = rest[0] if norm_out else None
    o_ref, acc_ref, wa_ref, wb_ref = rest[-4:]
    j = pl.program_id(1)
    last = pl.num_programs(1) - 1
    keys = (n_ref, p1_ref, r2_ref, e2_ref)

    @pl.when(j == 0)
    def _():
        acc_ref[...] = jnp.zeros_like(acc_ref)
        wb_ref[...] = jnp.zeros_like(wb_ref)

    hft = hft_ref[...]
    _peer_half(2 * j * na, hft, keys, ua_ref, wa_ref, vtb_ref, wb_ref, acc_ref, na)
    _peer_half((2 * j + 1) * na, hft, keys, ub_ref, wb_ref, vta_ref, wa_ref, acc_ref, na)

    @pl.when(j == last)
    def _():
        acc = acc_ref[...] + _dot(vtl_ref[...], wb_ref[...])
        xo = x_ref[...] + jnp.transpose(acc)
        o_ref[...] = _rms(xo, g_ref[...]) if norm_out else xo


def _peer_dense(x, hft, n1, p1, r2, e2, u, vt, layer, tm, na, g_out=None):
    T, D = x.shape
    H, NK = PEER_HEADS, PEER_KEYS
    th = na * NK
    steps = NK // (2 * na)
    kspec = pl.BlockSpec((H, NK, tm), lambda i, j: (0, 0, i))
    return pl.pallas_call(
        functools.partial(_peer_dense_kernel, na=na, norm_out=g_out is not None),
        grid=(T // tm, steps),
        in_specs=[pl.BlockSpec((tm, D), lambda i, j: (i, 0)),
                  pl.BlockSpec((D, tm), lambda i, j: (0, i)),
                  kspec, kspec, kspec, kspec,
                  pl.BlockSpec((None, th, D), lambda i, j: (layer, 2 * j, 0)),
                  pl.BlockSpec((None, th, D), lambda i, j: (layer, 2 * j + 1, 0)),
                  pl.BlockSpec((None, None, D, th), lambda i, j: (layer, 2 * j, 0, 0)),
                  pl.BlockSpec((None, None, D, th), lambda i, j: (layer, jnp.maximum(2 * j - 1, 1), 0, 0)),
                  pl.BlockSpec((None, None, D, th), lambda i, j: (layer, 2 * steps - 1, 0, 0))]
                 + ([pl.BlockSpec((1, D), lambda i, j: (0, 0))] if g_out is not None else []),
        out_specs=pl.BlockSpec((tm, D), lambda i, j: (i, 0)),
        out_shape=jax.ShapeDtypeStruct((T, D), F32),
        scratch_shapes=[pltpu.VMEM((D, tm), F32), pltpu.VMEM((th, tm), BF16), pltpu.VMEM((th, tm), BF16)],
        compiler_params=_params("parallel", "arbitrary"),
        name="peer_dense",
    )(x, hft, n1, p1, r2, e2, u, u, vt, vt, vt, *(() if g_out is None else (g_out,)))


def kernel(x, mem, positions, g_mix, w_in, w_ret_br, w_rnn_br, w_mix_out, conv_w, conv_b, lru_wa, lru_ba, lru_wx, lru_bx, lru_lam, g_x, g_mem, w_xq, w_xk, w_xv, w_xo, g_ffn, w_pq, sub_k1, sub_k2, peer_u, peer_v, g_final):
    B, T, D = x.shape
    assert B == 1 and D == D_MODEL
    depth = g_mix.shape[0]
    tm = min(TM_PROJ, T)
    xs = x.reshape(T, D)
    mems = mem.reshape(mem.shape[1], D)
    pos_col = positions.reshape(T, 1)
    row = lambda a: a.reshape(1, -1)

    cos2, sin2 = _rope_tables(pos_col, tm)
    u_all = peer_u.astype(BF16)
    th = NA_PEER * PEER_KEYS
    vt_all = jnp.swapaxes(peer_v.reshape(depth, -1, th, D), 2, 3).astype(BF16)
    for l in range(depth):
        z = _rms_matmul(xs, row(g_mix[l]), w_in, l, tm=min(TM_IN, T), tn=TN_PROJ, out_dtype=BF16)
        ret = _retention(z, cos2, sin2, min(TB_RET, T))
        rnn = _rglru(z, pos_col, conv_w[l], row(conv_b[l]), lru_wa[l].astype(BF16), row(lru_ba[l]),
                     lru_wx[l].astype(BF16), row(lru_bx[l]), row(lru_lam[l]), min(TB_LRU, T))
        xs = _merge(xs, ret, rnn, z, w_ret_br, w_rnn_br, w_mix_out, l, tm)
        kmem = _rms_matmul(mems, row(g_mem[l]), w_xk, l, tm=mems.shape[0], tn=D, out_dtype=BF16)
        vmem = _rms_matmul(mems, row(g_mem[l]), w_xv, l, tm=mems.shape[0], tn=D, out_dtype=BF16)
        xs = _xattn(xs, row(g_x[l]), w_xq, kmem.T, vmem, w_xo, l, tm)
        tp = min(TM_PEER, T)
        hft, s1, s2 = _peer_scores(xs, row(g_ffn[l]), w_pq[l].T.astype(BF16), sub_k1[l].astype(BF16),
                                   sub_k2[l].astype(BF16), tp)
        n1, p1, r2, e2 = _peer_select(s1, s2, TL_SORT)
        xs = _peer_dense(xs, hft, n1, p1, r2, e2, u_all, vt_all, l, tp, NA_PEER,
                         g_out=row(g_final) if l == depth - 1 else None)
    return xs.reshape(B, T, D)
```

```python
import functools
import math

import numpy as np
import jax
import jax.numpy as jnp
from jax import lax
from jax.experimental import pallas as pl
from jax.experimental.pallas import tpu as pltpu

F32 = jnp.float32
BF16 = jnp.bfloat16

D_MODEL = 1024
EPS = 1e-6
RET_HEADS = 4
RET_QK_DIM = 128
RET_V_DIM = 256
RET_CHUNK = 128
ROPE_BASE = 10000.0
LRU_BLOCKS = 8
LRU_BLOCK_DIM = D_MODEL // LRU_BLOCKS
CONV_WIDTH = 4
LRU_C = 8.0
XATTN_HEADS = 4
XATTN_HEAD_DIM = D_MODEL // XATTN_HEADS
PEER_HEADS = 8
PEER_KEYS = 128
PEER_TOPK = 16
PEER_HALF = 128

SUBLANES = 8
VMEM_LIMIT_BYTES = 56 * 1024 * 1024

TM_PROJ = 512
TM_IN = 2048
TN_PROJ = 1024
TB_RET = 256
TB_LRU = 512
TM_PEER = 512
NA_PEER = 4
TL_SORT = 256


def _params(*sem):
    return pltpu.CompilerParams(dimension_semantics=sem, vmem_limit_bytes=VMEM_LIMIT_BYTES)


def _rms(x, g):
    return x * lax.rsqrt(jnp.mean(x * x, axis=-1, keepdims=True) + EPS) * g


def _gelu(x):
    c = math.sqrt(2.0 / math.pi)
    return 0.5 * x * (1.0 + jnp.tanh(c * (x + 0.044715 * (x * x * x))))


def _dot(a, b):
    return jnp.dot(a, b, preferred_element_type=F32)


def _dot_nt(a, b):
    return lax.dot_general(a, b, (((1,), (1,)), ((), ())), preferred_element_type=F32)


def _rms_matmul_kernel(x_ref, g_ref, w_ref, o_ref, h_ref):
    @pl.when(pl.program_id(1) == 0)
    def _():
        h_ref[...] = _rms(x_ref[...], g_ref[...]).astype(BF16)

    o_ref[...] = _dot(h_ref[...], w_ref[...].astype(BF16)).astype(o_ref.dtype)


def _rms_matmul(x, g, w, layer, *, tm, tn, out_dtype):
    T, K = x.shape
    N = w.shape[2]
    return pl.pallas_call(
        _rms_matmul_kernel,
        grid=(T // tm, N // tn),
        in_specs=[pl.BlockSpec((tm, K), lambda i, j: (i, 0)),
                  pl.BlockSpec((1, K), lambda i, j: (0, 0)),
                  pl.BlockSpec((None, K, tn), lambda i, j: (layer, 0, j))],
        out_specs=pl.BlockSpec((tm, tn), lambda i, j: (i, j)),
        out_shape=jax.ShapeDtypeStruct((T, N), out_dtype),
        scratch_shapes=[pltpu.VMEM((tm, K), BF16)],
        compiler_params=_params("parallel", "arbitrary"),
        name="rms_matmul",
    )(x, g, w)


def _rope_kernel(pos_ref, invf_ref, cos_ref, sin_ref):
    ang = pos_ref[...].astype(F32) * invf_ref[...]
    lane = lax.broadcasted_iota(jnp.int32, ang.shape, 1)
    s = jnp.sin(ang)
    cos_ref[...] = jnp.cos(ang)
    sin_ref[...] = jnp.where(lane < RET_QK_DIM // 2, -s, s)


def _rope_tables(pos_col, tb):
    T = pos_col.shape[0]
    half = RET_QK_DIM // 2
    inv_freq = ROPE_BASE ** (-jnp.arange(0, RET_QK_DIM, 2, dtype=F32) / RET_QK_DIM)
    invf2 = jnp.concatenate([inv_freq, inv_freq]).reshape(1, 2 * half)
    return pl.pallas_call(
        _rope_kernel,
        grid=(T // tb,),
        in_specs=[pl.BlockSpec((tb, 1), lambda i: (i, 0)),
                  pl.BlockSpec((1, RET_QK_DIM), lambda i: (0, 0))],
        out_specs=[pl.BlockSpec((tb, RET_QK_DIM), lambda i: (i, 0))] * 2,
        out_shape=[jax.ShapeDtypeStruct((T, RET_QK_DIM), F32)] * 2,
        compiler_params=_params("parallel"),
        name="rope_tables",
    )(pos_col, invf2)


def _retention_kernel(q_ref, k_ref, v_ref, g_ref, cos_ref, sin_ref, dec_ref, kdec_ref, qdec_ref,
                      o_ref, s_ref, *, chunk_decay):
    C, dk, dv = RET_CHUNK, RET_QK_DIM, RET_V_DIM

    @pl.when(pl.program_id(0) == 0)
    def _():
        s_ref[...] = jnp.zeros_like(s_ref)

    kscale = dk ** -0.5
    for c in range(q_ref.shape[0] // C):
        rows = slice(c * C, (c + 1) * C)
        cos = cos_ref[rows, :]
        sin = sin_ref[rows, :]
        for h in range(RET_HEADS):
            q = q_ref[rows, h * dk:(h + 1) * dk].astype(F32)
            k = k_ref[rows, h * dk:(h + 1) * dk].astype(F32)
            v = v_ref[rows, h * dv:(h + 1) * dv].astype(BF16)
            qr = q * cos + pltpu.roll(q, dk // 2, 1) * sin
            kr = (k * cos + pltpu.roll(k, dk // 2, 1) * sin) * kscale
            scores = _dot_nt(qr.astype(BF16), kr.astype(BF16)) * dec_ref[h]
            inner = _dot(scores.astype(BF16), v)
            state = s_ref[h]
            cross = _dot((qr * qdec_ref[h]).astype(BF16), state.astype(BF16))
            kd_t = jnp.transpose(kr * kdec_ref[h]).astype(BF16)
            s_ref[h] = chunk_decay[h] * state + _dot(kd_t, v)
            o = inner + cross
            mu = jnp.mean(o, axis=-1, keepdims=True)
            oc = o - mu
            var = jnp.mean(oc * oc, axis=-1, keepdims=True)
            o = oc * lax.rsqrt(var + EPS)
            g = g_ref[rows, h * dv:(h + 1) * dv].astype(F32)
            o_ref[rows, h * dv:(h + 1) * dv] = (g * jax.nn.sigmoid(g) * o).astype(o_ref.dtype)


def _retention(z, cos2, sin2, tb):
    T = z.shape[0]
    H, C, dk, dv = RET_HEADS, RET_CHUNK, RET_QK_DIM, RET_V_DIM
    log_g = jnp.log(1.0 - 2.0 ** (-5.0 - jnp.arange(H, dtype=F32)))
    i = jnp.arange(C, dtype=F32)
    diff = i[:, None] - i[None, :]
    decay = jnp.where(diff[None] >= 0, jnp.exp(diff[None] * log_g[:, None, None]), 0.0)
    k_decay = jnp.exp((C - 1.0 - i)[None, :] * log_g[:, None])
    q_decay = jnp.exp((i + 1.0)[None, :] * log_g[:, None])
    kdec = jnp.broadcast_to(k_decay[:, :, None], (H, C, dk))
    qdec = jnp.broadcast_to(q_decay[:, :, None], (H, C, dk))
    log_g_np = np.log(1.0 - 2.0 ** (-5.0 - np.arange(H, dtype=np.float32))).astype(np.float32)
    chunk_decay = tuple(float(np.exp(np.float32(C) * lg)) for lg in log_g_np)
    nq = H * dk
    const3 = lambda i: (0, 0, 0)
    return pl.pallas_call(
        functools.partial(_retention_kernel, chunk_decay=chunk_decay),
        grid=(T // tb,),
        in_specs=[pl.BlockSpec((tb, nq), lambda i: (i, 0)),
                  pl.BlockSpec((tb, nq), lambda i: (i, 1)),
                  pl.BlockSpec((tb, H * dv), lambda i: (i, 1)),
                  pl.BlockSpec((tb, H * dv), lambda i: (i, 2)),
                  pl.BlockSpec((tb, dk), lambda i: (i, 0)),
                  pl.BlockSpec((tb, dk), lambda i: (i, 0)),
                  pl.BlockSpec((H, C, C), const3),
                  pl.BlockSpec((H, C, dk), const3),
                  pl.BlockSpec((H, C, dk), const3)],
        out_specs=pl.BlockSpec((tb, H * dv), lambda i: (i, 0)),
        out_shape=jax.ShapeDtypeStruct((T, H * dv), BF16),
        scratch_shapes=[pltpu.VMEM((H, dk, dv), F32)],
        compiler_params=_params("arbitrary"),
        name="retention",
    )(z, z, z, z, cos2, sin2, decay, kdec, qdec)


def _rglru_kernel(x_ref, y_ref, pos_ref, cw_ref, cb_ref, wa_ref, ba_ref, wx_ref, bx_ref, lam_ref,
                  o_ref, xpad_ref, a_ref, b_ref, hc_ref):
    tb = x_ref.shape[0]
    pad = SUBLANES
    halo = SUBLANES

    @pl.when(pl.program_id(0) == 0)
    def _():
        xpad_ref[0:halo, :] = jnp.zeros((halo, D_MODEL), F32)
        hc_ref[...] = jnp.zeros_like(hc_ref)
        a_ref[0:pad, :] = jnp.ones((pad, D_MODEL), F32)
        b_ref[0:pad, :] = jnp.zeros((pad, D_MODEL), F32)

    xpad_ref[halo:halo + tb, :] = x_ref[...].astype(F32)
    xc = cb_ref[...] + cw_ref[3:4, :] * xpad_ref[halo:halo + tb, :]
    for w in range(CONV_WIDTH - 1):
        off = halo - (CONV_WIDTH - 1) + w
        xc = xc + cw_ref[w:w + 1, :] * xpad_ref[off:off + tb, :]
    xpad_ref[0:halo, :] = xpad_ref[tb:tb + halo, :]

    reset = pos_ref[...] == 0
    lam = lam_ref[...]
    nl = -lam
    sp = jnp.maximum(nl, 0.0) + jnp.log(1.0 + jnp.exp(-jnp.abs(nl)))
    bd = LRU_BLOCK_DIM
    for n in range(LRU_BLOCKS):
        cols = slice(n * bd, (n + 1) * bd)
        xcn = xc[:, cols]
        xb = xcn.astype(BF16)
        gate_r = jax.nn.sigmoid(_dot(xb, wa_ref[n]) + ba_ref[:, cols])
        gate_i = jax.nn.sigmoid(_dot(xb, wx_ref[n]) + bx_ref[:, cols])
        log_a = (-LRU_C) * gate_r * sp[:, cols]
        a = jnp.exp(log_a)
        mult = jnp.sqrt(jnp.tanh(-log_a) * (a * a + 1.0))
        a = jnp.where(reset, 0.0, a)
        mult = jnp.where(reset, 1.0, mult)
        b = xcn * gate_i * mult
        s = 1
        while s < SUBLANES:
            a_ref[pad:pad + tb, cols] = a
            b_ref[pad:pad + tb, cols] = b
            a_sh = a_ref[pad - s:pad - s + tb, cols]
            b_sh = b_ref[pad - s:pad - s + tb, cols]
            b = a * b_sh + b
            a = a * a_sh
            s *= 2
        hg = hc_ref[:, cols]
        groups = []
        for g in range(tb // SUBLANES):
            rows = slice(g * SUBLANES, (g + 1) * SUBLANES)
            hg = a[rows, :] * hg + b[rows, :]
            groups.append(hg)
        h = jnp.concatenate(groups, axis=0)
        hc_ref[:, cols] = jnp.broadcast_to(hg[SUBLANES - 1:SUBLANES, :], (SUBLANES, bd))
        o_ref[:, cols] = (h * _gelu(y_ref[:, cols].astype(F32))).astype(o_ref.dtype)


def _rglru(z, pos_col, conv_w, conv_b, wa, ba, wx, bx, lam, tb):
    T = z.shape[0]
    W = D_MODEL
    row = lambda i: (0, 0)
    const3 = lambda i: (0, 0, 0)
    return pl.pallas_call(
        _rglru_kernel,
        grid=(T // tb,),
        in_specs=[pl.BlockSpec((tb, W), lambda i: (i, 3)),
                  pl.BlockSpec((tb, W), lambda i: (i, 4)),
                  pl.BlockSpec((tb, 1), lambda i: (i, 0)),
                  pl.BlockSpec((CONV_WIDTH, W), row),
                  pl.BlockSpec((1, W), row),
                  pl.BlockSpec((LRU_BLOCKS, LRU_BLOCK_DIM, LRU_BLOCK_DIM), const3),
                  pl.BlockSpec((1, W), row),
                  pl.BlockSpec((LRU_BLOCKS, LRU_BLOCK_DIM, LRU_BLOCK_DIM), const3),
                  pl.BlockSpec((1, W), row),
                  pl.BlockSpec((1, W), row)],
        out_specs=pl.BlockSpec((tb, W), lambda i: (i, 0)),
        out_shape=jax.ShapeDtypeStruct((T, W), BF16),
        scratch_shapes=[pltpu.VMEM((tb + 2 * SUBLANES, W), F32),
                        pltpu.VMEM((tb + SUBLANES, W), F32),
                        pltpu.VMEM((tb + SUBLANES, W), F32),
                        pltpu.VMEM((SUBLANES, W), F32)],
        compiler_params=_params("arbitrary"),
        name="rglru",
    )(z, z, pos_col, conv_w, conv_b, wa, ba, wx, bx, lam)


def _merge_kernel(x_ref, ret_ref, rnn_ref, ga_ref, gb_ref, wr_ref, wn_ref, wo_ref, o_ref, w_ref):
    @pl.when(pl.program_id(0) == 0)
    def _():
        for k, src in enumerate((wr_ref, wn_ref, wo_ref)):
            w_ref[k] = src[...].astype(BF16)

    p_ret = _dot(ret_ref[...], w_ref[0])
    p_rnn = _dot(rnn_ref[...], w_ref[1])
    merged = (jax.nn.sigmoid(ga_ref[...].astype(F32)) * p_ret
              + jax.nn.sigmoid(gb_ref[...].astype(F32)) * p_rnn)
    o_ref[...] = x_ref[...] + _dot(merged.astype(BF16), w_ref[2])


def _merge(x, ret, rnn, z, w_ret, w_rnn, w_out, layer, tm):
    T, D = x.shape
    tile = lambda i: (i, 0)
    wspec = pl.BlockSpec((None, D, D), lambda i: (layer, 0, 0))
    return pl.pallas_call(
        _merge_kernel,
        grid=(T // tm,),
        in_specs=[pl.BlockSpec((tm, D), tile), pl.BlockSpec((tm, D), tile), pl.BlockSpec((tm, D), tile),
                  pl.BlockSpec((tm, D), lambda i: (i, 5)),
                  pl.BlockSpec((tm, D), lambda i: (i, 6)),
                  wspec, wspec, wspec],
        out_specs=pl.BlockSpec((tm, D), tile),
        out_shape=jax.ShapeDtypeStruct((T, D), F32),
        scratch_shapes=[pltpu.VMEM((3, D, D), BF16)],
        compiler_params=_params("arbitrary"),
        name="merge",
    )(x, ret, rnn, z, z, w_ret, w_rnn, w_out)


def _xattn_kernel(x_ref, g_ref, mem_ref, gm_ref, wq_ref, wk_ref, wv_ref, wo_ref, o_ref,
                  att_ref, w_ref, kt_ref, v_ref):
    @pl.when(pl.program_id(0) == 0)
    def _():
        w_ref[0] = wq_ref[...].astype(BF16)
        w_ref[1] = wo_ref[...].astype(BF16)
        m = _rms(mem_ref[...], gm_ref[...]).astype(BF16)
        k = _dot(m, wk_ref[...].astype(BF16)).astype(BF16)
        kt_ref[...] = jnp.transpose(k.astype(F32)).astype(BF16)
        v_ref[...] = _dot(m, wv_ref[...].astype(BF16)).astype(BF16)

    x = x_ref[...]
    q = _dot(_rms(x, g_ref[...]).astype(BF16), w_ref[0])
    dh = XATTN_HEAD_DIM
    for h in range(XATTN_HEADS):
        cols = slice(h * dh, (h + 1) * dh)
        s = _dot(q[:, cols].astype(BF16), kt_ref[cols, :]) * (dh ** -0.5)
        s = s - jnp.max(s, axis=-1, keepdims=True)
        e = jnp.exp(s)
        p = e / jnp.sum(e, axis=-1, keepdims=True)
        att_ref[:, cols] = _dot(p.astype(BF16), v_ref[:, cols]).astype(BF16)
    o_ref[...] = x + _dot(att_ref[...], w_ref[1])


def _xattn(x, g, mem, g_mem, wq, wk, wv, wo, layer, tm):
    T, D = x.shape
    M = mem.shape[0]
    tile = lambda i: (i, 0)
    fixed = lambda i: (0, 0)
    wspec = pl.BlockSpec((None, D, D), lambda i: (layer, 0, 0))
    return pl.pallas_call(
        _xattn_kernel,
        grid=(T // tm,),
        in_specs=[pl.BlockSpec((tm, D), tile), pl.BlockSpec((1, D), fixed),
                  pl.BlockSpec((M, D), fixed), pl.BlockSpec((1, D), fixed), wspec, wspec, wspec, wspec],
        out_specs=pl.BlockSpec((tm, D), tile),
        out_shape=jax.ShapeDtypeStruct((T, D), F32),
        scratch_shapes=[pltpu.VMEM((tm, D), BF16), pltpu.VMEM((2, D, D), BF16),
                        pltpu.VMEM((D, M), BF16), pltpu.VMEM((M, D), BF16)],
        compiler_params=_params("arbitrary"),
        name="xattn",
    )(x, g, mem, g_mem, wq, wk, wv, wo)


def _peer_scores_kernel(x_ref, g_ref, wq_ref, k1_ref, k2_ref, hft_ref, s1_ref, s2_ref):
    hf = _rms(x_ref[...], g_ref[...])
    hft = jnp.transpose(hf).astype(BF16)
    hft_ref[...] = hft
    qt = _dot(wq_ref[...], hft)
    kd = 2 * PEER_HALF
    for h in range(PEER_HEADS):
        q1 = qt[h * kd:h * kd + PEER_HALF, :].astype(BF16)
        q2 = qt[h * kd + PEER_HALF:(h + 1) * kd, :].astype(BF16)
        s1_ref[h] = _dot(k1_ref[h], q1)
        s2_ref[h] = _dot(k2_ref[h], q2)


def _peer_scores(x, g, wq_t, k1, k2, tm):
    T, D = x.shape
    H, NK = PEER_HEADS, PEER_KEYS
    fixed = lambda i: (0, 0)
    const3 = lambda i: (0, 0, 0)
    return pl.pallas_call(
        _peer_scores_kernel,
        grid=(T // tm,),
        in_specs=[pl.BlockSpec((tm, D), lambda i: (i, 0)), pl.BlockSpec((1, D), fixed),
                  pl.BlockSpec(wq_t.shape, fixed),
                  pl.BlockSpec((H, NK, PEER_HALF), const3), pl.BlockSpec((H, NK, PEER_HALF), const3)],
        out_specs=[pl.BlockSpec((D, tm), lambda i: (0, i)),
                   pl.BlockSpec((H, NK, tm), lambda i: (0, 0, i)),
                   pl.BlockSpec((H, NK, tm), lambda i: (0, 0, i))],
        out_shape=[jax.ShapeDtypeStruct((D, T), BF16),
                   jax.ShapeDtypeStruct((H, NK, T), F32),
                   jax.ShapeDtypeStruct((H, NK, T), F32)],
        compiler_params=_params("parallel"),
        name="peer_scores",
    )(x, g, wq_t, k1, k2)


def _sort_pairs(n):
    pairs = []
    p = 1
    while p < n:
        k = p
        while k >= 1:
            for j in range(k % p, n - k, 2 * k):
                for i in range(min(k, n - j - k)):
                    if (i + j) // (2 * p) == (i + j + k) // (2 * p):
                        pairs.append((i + j, i + j + k))
            k //= 2
        p *= 2
    return pairs


def _sort_desc(vs):
    vs = list(vs)
    for lo, hi in _sort_pairs(len(vs)):
        a, b = vs[lo], vs[hi]
        vs[lo], vs[hi] = jnp.maximum(a, b), jnp.minimum(a, b)
    return vs


def _merge_top(a, b):
    n = len(a)
    c = [jnp.maximum(a[i], b[n - 1 - i]) for i in range(n)]
    k = n // 2
    while k >= 1:
        for i in range(n):
            if i & k == 0:
                x, y = c[i], c[i + k]
                c[i], c[i + k] = jnp.maximum(x, y), jnp.minimum(x, y)
        k //= 2
    return c


def _top_of_keys(groups, roll):
    vs = _sort_desc(groups)
    shift = SUBLANES // 2
    while shift >= 1:
        vs = _merge_top(vs, [roll(v, shift, 0) for v in vs])
        shift //= 2
    return vs


def _peer_select_math(s1, s2, roll):
    K = PEER_TOPK
    shape = s1[0][0].shape
    sub = lax.broadcasted_iota(jnp.int32, shape, 0)
    zero = jnp.zeros(shape, F32)
    tops2 = []
    v1 = [zero] * K
    v2 = [zero] * K
    for h in range(PEER_HEADS):
        t1 = _top_of_keys(s1[h], roll)
        t2 = _top_of_keys(s2[h], roll)
        tops2.append(t2)
        v1 = [jnp.where(sub == h, t, v) for t, v in zip(t1, v1)]
        v2 = [jnp.where(sub == h, t, v) for t, v in zip(t2, v2)]
    cands = [v1[i] + v2[j] for i in range(K) for j in range(K // (i + 1))]
    n = 1
    while n < len(cands):
        n *= 2
    cands = cands + [jnp.full(shape, -jnp.inf, F32)] * (n - len(cands))
    top = _sort_desc(cands)[:K]
    z = zero
    for t in top:
        z = z + jnp.exp(t - top[0])
    tau_all, zinv_all = top[K - 1], 1.0 / z
    out = []
    for h in range(PEER_HEADS):
        bc = lambda v: jnp.broadcast_to(v[h:h + 1, :], shape)
        tau, zinv, m1, m2 = bc(tau_all), bc(zinv_all), bc(v1[0]), bc(v2[0])
        t2 = tops2[h]
        cnt, p1, rk, e2 = [], [], [], []
        c_top = zero
        for j in range(K // 2, K):
            c_top = jnp.where(m1 + t2[j] >= tau, float(j + 1), c_top)
        for s1g, s2g in zip(s1[h], s2[h]):
            c = zero
            r = zero
            for j in range(K // 2):
                c = jnp.where(s1g + t2[j] >= tau, float(j + 1), c)
            c = jnp.where(s1g == m1, jnp.maximum(c, c_top), c)
            for j in range(K):
                r = jnp.where(t2[j] > s2g, float(j + 1), r)
            cnt.append(c)
            rk.append(r)
            p1.append(jnp.exp(s1g - m1) * zinv)
            e2.append(jnp.exp(s2g - m2))
        out.append((cnt, p1, rk, e2))
    return out


def _bf16_pair(x):
    u = lax.bitcast_convert_type(x.astype(BF16).astype(F32), jnp.uint32)
    return u | (u >> 16)


def _peer_select_kernel(s1_ref, s2_ref, n_ref, p1_ref, r2_ref, e2_ref):
    ng = PEER_KEYS // SUBLANES
    grp = lambda ref, h: [ref[h, i * SUBLANES:(i + 1) * SUBLANES, :] for i in range(ng)]
    s1 = [grp(s1_ref, h) for h in range(PEER_HEADS)]
    s2 = [grp(s2_ref, h) for h in range(PEER_HEADS)]
    res = _peer_select_math(s1, s2, pltpu.roll)
    pk = 2 * SUBLANES
    for h, (cnt, p1, rk, e2) in enumerate(res):
        for i in range(ng):
            rows = slice(i * SUBLANES, (i + 1) * SUBLANES)
            n_ref[h, rows, :] = _bf16_pair(cnt[i])
            p1_ref[h, rows, :] = _bf16_pair(p1[i])
        for i in range(ng // 2):
            rows = slice(i * pk, (i + 1) * pk)
            r2_ref[h, rows, :] = jnp.concatenate([rk[2 * i], rk[2 * i + 1]], axis=0).astype(BF16)
            e2_ref[h, rows, :] = jnp.concatenate([e2[2 * i], e2[2 * i + 1]], axis=0).astype(BF16)


def _peer_select(s1, s2, tl):
    H, NK, T = s1.shape
    spec = pl.BlockSpec((H, NK, tl), lambda i: (0, 0, i))
    return pl.pallas_call(
        _peer_select_kernel,
        grid=(T // tl,),
        in_specs=[spec] * 2,
        out_specs=[spec] * 4,
        out_shape=[jax.ShapeDtypeStruct((H, NK, T), jnp.uint32), jax.ShapeDtypeStruct((H, NK, T), jnp.uint32),
                   jax.ShapeDtypeStruct((H, NK, T), BF16), jax.ShapeDtypeStruct((H, NK, T), BF16)],
        compiler_params=_params("parallel"),
        name="peer_select",
    )(s1, s2)


def _peer_gate_rows(act, a, n_ref, p1_ref, r2_ref, e2_ref, w_ref, row0):
    NK = PEER_KEYS
    tm = act.shape[-1]
    pk = 2 * SUBLANES
    zero = jnp.zeros((pk, tm), BF16)
    bc = lambda ref, h: pltpu.bitcast(jnp.broadcast_to(ref[h, pl.ds(a, 1), :], (SUBLANES, tm)), BF16)
    nrow = [bc(n_ref, h) for h in range(PEER_HEADS)]
    prow = [bc(p1_ref, h) for h in range(PEER_HEADS)]
    for g in range(NK // pk):
        b_rows = slice(g * pk, (g + 1) * pk)
        gate = None
        for h in range(PEER_HEADS):
            sel = r2_ref[h, b_rows, :] < nrow[h]
            term = jnp.where(sel, prow[h] * e2_ref[h, b_rows, :], zero)
            gate = term if gate is None else gate + term
        w_ref[row0 + g * pk:row0 + (g + 1) * pk, :] = gate * act[b_rows, :]


def _peer_half(a0, hft, keys, u_ref, w_ref, vt_prev_ref, w_prev_ref, acc_ref, na):
    NK = PEER_KEYS
    gl = _gelu(_dot(u_ref[...], hft).astype(BF16))
    acc_ref[...] += _dot(vt_prev_ref[...], w_prev_ref[...])
    for ai in range(na):
        _peer_gate_rows(gl[ai * NK:(ai + 1) * NK], a0 + ai, *keys, w_ref, ai * NK)


def _peer_dense_kernel(x_ref, hft_ref, n_ref, p1_ref, r2_ref, e2_ref, ua_ref, ub_ref, vta_ref, vtb_ref,
                       vtl_ref, *rest, na, norm_out):
    g_ref = rest[0] if norm_out else None
    o_ref, acc_ref, wa_ref, wb_ref = rest[-4:]
    j = pl.program_id(1)
    last = pl.num_programs(1) - 1
    keys = (n_ref, p1_ref, r2_ref, e2_ref)

    @pl.when(j == 0)
    def _():
        acc_ref[...] = jnp.zeros_like(acc_ref)
        wb_ref[...] = jnp.zeros_like(wb_ref)

    hft = hft_ref[...]
    _peer_half(2 * j * na, hft, keys, ua_ref, wa_ref, vtb_ref, wb_ref, acc_ref, na)
    _peer_half((2 * j + 1) * na, hft, keys, ub_ref, wb_ref, vta_ref, wa_ref, acc_ref, na)

    @pl.when(j == last)
    def _():
        acc = acc_ref[...] + _dot(vtl_ref[...], wb_ref[...])
        xo = x_ref[...] + jnp.transpose(acc)
        o_ref[...] = _rms(xo, g_ref[...]) if norm_out else xo


def _peer_dense(x, hft, n1, p1, r2, e2, u, vt, layer, tm, na, g_out=None):
    T, D = x.shape
    H, NK = PEER_HEADS, PEER_KEYS
    th = na * NK
    steps = NK // (2 * na)
    kspec = pl.BlockSpec((H, NK, tm), lambda i, j: (0, 0, i))
    return pl.pallas_call(
        functools.partial(_peer_dense_kernel, na=na, norm_out=g_out is not None),
        grid=(T // tm, steps),
        in_specs=[pl.BlockSpec((tm, D), lambda i, j: (i, 0)),
                  pl.BlockSpec((D, tm), lambda i, j: (0, i)),
                  kspec, kspec, kspec, kspec,
                  pl.BlockSpec((None, th, D), lambda i, j: (layer, 2 * j, 0)),
                  pl.BlockSpec((None, th, D), lambda i, j: (layer, 2 * j + 1, 0)),
                  pl.BlockSpec((None, None, D, th), lambda i, j: (layer, 2 * j, 0, 0)),
                  pl.BlockSpec((None, None, D, th), lambda i, j: (layer, jnp.maximum(2 * j - 1, 1), 0, 0)),
                  pl.BlockSpec((None, None, D, th), lambda i, j: (layer, 2 * steps - 1, 0, 0))]
                 + ([pl.BlockSpec((1, D), lambda i, j: (0, 0))] if g_out is not None else []),
        out_specs=pl.BlockSpec((tm, D), lambda i, j: (i, 0)),
        out_shape=jax.ShapeDtypeStruct((T, D), F32),
        scratch_shapes=[pltpu.VMEM((D, tm), F32), pltpu.VMEM((th, tm), BF16), pltpu.VMEM((th, tm), BF16)],
        compiler_params=_params("parallel", "arbitrary"),
        name="peer_dense",
    )(x, hft, n1, p1, r2, e2, u, u, vt, vt, vt, *(() if g_out is None else (g_out,)))


def kernel(x, mem, positions, g_mix, w_in, w_ret_br, w_rnn_br, w_mix_out, conv_w, conv_b, lru_wa, lru_ba, lru_wx, lru_bx, lru_lam, g_x, g_mem, w_xq, w_xk, w_xv, w_xo, g_ffn, w_pq, sub_k1, sub_k2, peer_u, peer_v, g_final):
    B, T, D = x.shape
    assert B == 1 and D == D_MODEL
    depth = g_mix.shape[0]
    tm = min(TM_PROJ, T)
    xs = x.reshape(T, D)
    mems = mem.reshape(mem.shape[1], D)
    pos_col = positions.reshape(T, 1)
    row = lambda a: a.reshape(1, -1)

    cos2, sin2 = _rope_tables(pos_col, tm)
    u_all = peer_u.astype(BF16)
    th = NA_PEER * PEER_KEYS
    vt_all = jnp.swapaxes(peer_v.reshape(depth, -1, th, D), 2, 3).astype(BF16)
    for l in range(depth):
        z = _rms_matmul(xs, row(g_mix[l]), w_in, l, tm=min(TM_IN, T), tn=TN_PROJ, out_dtype=BF16)
        ret = _retention(z, cos2, sin2, min(TB_RET, T))
        rnn = _rglru(z, pos_col, conv_w[l], row(conv_b[l]), lru_wa[l].astype(BF16), row(lru_ba[l]),
                     lru_wx[l].astype(BF16), row(lru_bx[l]), row(lru_lam[l]), min(TB_LRU, T))
        xs = _merge(xs, ret, rnn, z, w_ret_br, w_rnn_br, w_mix_out, l, tm)
        xs = _xattn(xs, row(g_x[l]), mems, row(g_mem[l]), w_xq, w_xk, w_xv, w_xo, l, tm)
        tp = min(TM_PEER, T)
        hft, s1, s2 = _peer_scores(xs, row(g_ffn[l]), w_pq[l].T.astype(BF16), sub_k1[l].astype(BF16),
                                   sub_k2[l].astype(BF16), tp)
        n1, p1, r2, e2 = _peer_select(s1, s2, TL_SORT)
        xs = _peer_dense(xs, hft, n1, p1, r2, e2, u_all, vt_all, l, tp, NA_PEER,
                         g_out=row(g_final) if l == depth - 1 else None)
    return xs.reshape(B, T, D)
```

```python
import functools
import math

import numpy as np
import jax
import jax.numpy as jnp
from jax import lax
from jax.experimental import pallas as pl
from jax.experimental.pallas import tpu as pltpu

F32 = jnp.float32
BF16 = jnp.bfloat16

D_MODEL = 1024
EPS = 1e-6
RET_HEADS = 4
RET_QK_DIM = 128
RET_V_DIM = 256
RET_CHUNK = 128
ROPE_BASE = 10000.0
LRU_BLOCKS = 8
LRU_BLOCK_DIM = D_MODEL // LRU_BLOCKS
CONV_WIDTH = 4
LRU_C = 8.0
XATTN_HEADS = 4
XATTN_HEAD_DIM = D_MODEL // XATTN_HEADS
PEER_HEADS = 8
PEER_KEYS = 128
PEER_TOPK = 16
PEER_HALF = 128

SUBLANES = 8
VMEM_LIMIT_BYTES = 56 * 1024 * 1024

TM_PROJ = 512
TM_IN = 2048
TN_PROJ = 1024
TB_RET = 256
TB_LRU = 512
TM_PEER = 512
NA_PEER = 4
TL_SORT = 256


def _params(*sem):
    return pltpu.CompilerParams(dimension_semantics=sem, vmem_limit_bytes=VMEM_LIMIT_BYTES)


def _rms(x, g):
    return x * lax.rsqrt(jnp.mean(x * x, axis=-1, keepdims=True) + EPS) * g


def _gelu(x):
    c = math.sqrt(2.0 / math.pi)
    return 0.5 * x * (1.0 + jnp.tanh(c * (x + 0.044715 * (x * x * x))))


def _dot(a, b):
    return jnp.dot(a, b, preferred_element_type=F32)


def _dot_nt(a, b):
    return lax.dot_general(a, b, (((1,), (1,)), ((), ())), preferred_element_type=F32)


def _rms_matmul_kernel(x_ref, g_ref, w_ref, o_ref, h_ref):
    @pl.when(pl.program_id(1) == 0)
    def _():
        h_ref[...] = _rms(x_ref[...], g_ref[...]).astype(BF16)

    o_ref[...] = _dot(h_ref[...], w_ref[...].astype(BF16)).astype(o_ref.dtype)


def _rms_matmul(x, g, w, layer, *, tm, tn, out_dtype):
    T, K = x.shape
    N = w.shape[2]
    return pl.pallas_call(
        _rms_matmul_kernel,
        grid=(T // tm, N // tn),
        in_specs=[pl.BlockSpec((tm, K), lambda i, j: (i, 0)),
                  pl.BlockSpec((1, K), lambda i, j: (0, 0)),
                  pl.BlockSpec((None, K, tn), lambda i, j: (layer, 0, j))],
        out_specs=pl.BlockSpec((tm, tn), lambda i, j: (i, j)),
        out_shape=jax.ShapeDtypeStruct((T, N), out_dtype),
        scratch_shapes=[pltpu.VMEM((tm, K), BF16)],
        compiler_params=_params("parallel", "arbitrary"),
        name="rms_matmul",
    )(x, g, w)


def _rope_kernel(pos_ref, invf_ref, cos_ref, sin_ref):
    ang = pos_ref[...].astype(F32) * invf_ref[...]
    lane = lax.broadcasted_iota(jnp.int32, ang.shape, 1)
    s = jnp.sin(ang)
    cos_ref[...] = jnp.cos(ang)
    sin_ref[...] = jnp.where(lane < RET_QK_DIM // 2, -s, s)


def _rope_tables(pos_col, tb):
    T = pos_col.shape[0]
    half = RET_QK_DIM // 2
    inv_freq = ROPE_BASE ** (-jnp.arange(0, RET_QK_DIM, 2, dtype=F32) / RET_QK_DIM)
    invf2 = jnp.concatenate([inv_freq, inv_freq]).reshape(1, 2 * half)
    return pl.pallas_call(
        _rope_kernel,
        grid=(T // tb,),
        in_specs=[pl.BlockSpec((tb, 1), lambda i: (i, 0)),
                  pl.BlockSpec((1, RET_QK_DIM), lambda i: (0, 0))],
        out_specs=[pl.BlockSpec((tb, RET_QK_DIM), lambda i: (i, 0))] * 2,
        out_shape=[jax.ShapeDtypeStruct((T, RET_QK_DIM), F32)] * 2,
        compiler_params=_params("parallel"),
        name="rope_tables",
    )(pos_col, invf2)


def _retention_kernel(q_ref, k_ref, v_ref, g_ref, cos_ref, sin_ref, dec_ref, kdec_ref, qdec_ref,
                      o_ref, s_ref, *, chunk_decay):
    C, dk, dv = RET_CHUNK, RET_QK_DIM, RET_V_DIM

    @pl.when(pl.program_id(0) == 0)
    def _():
        s_ref[...] = jnp.zeros_like(s_ref)

    kscale = dk ** -0.5
    for c in range(q_ref.shape[0] // C):
        rows = slice(c * C, (c + 1) * C)
        cos = cos_ref[rows, :]
        sin = sin_ref[rows, :]
        for h in range(RET_HEADS):
            q = q_ref[rows, h * dk:(h + 1) * dk].astype(F32)
            k = k_ref[rows, h * dk:(h + 1) * dk].astype(F32)
            v = v_ref[rows, h * dv:(h + 1) * dv].astype(BF16)
            qr = q * cos + pltpu.roll(q, dk // 2, 1) * sin
            kr = (k * cos + pltpu.roll(k, dk // 2, 1) * sin) * kscale
            scores = _dot_nt(qr.astype(BF16), kr.astype(BF16)) * dec_ref[h]
            inner = _dot(scores.astype(BF16), v)
            state = s_ref[h]
            cross = _dot((qr * qdec_ref[h]).astype(BF16), state.astype(BF16))
            kd_t = jnp.transpose(kr * kdec_ref[h]).astype(BF16)
            s_ref[h] = chunk_decay[h] * state + _dot(kd_t, v)
            o = inner + cross
            mu = jnp.mean(o, axis=-1, keepdims=True)
            oc = o - mu
            var = jnp.mean(oc * oc, axis=-1, keepdims=True)
            o = oc * lax.rsqrt(var + EPS)
            g = g_ref[rows, h * dv:(h + 1) * dv].astype(F32)
            o_ref[rows, h * dv:(h + 1) * dv] = (g * jax.nn.sigmoid(g) * o).astype(o_ref.dtype)


def _retention(z, cos2, sin2, tb):
    T = z.shape[0]
    H, C, dk, dv = RET_HEADS, RET_CHUNK, RET_QK_DIM, RET_V_DIM
    log_g = jnp.log(1.0 - 2.0 ** (-5.0 - jnp.arange(H, dtype=F32)))
    i = jnp.arange(C, dtype=F32)
    diff = i[:, None] - i[None, :]
    decay = jnp.where(diff[None] >= 0, jnp.exp(diff[None] * log_g[:, None, None]), 0.0)
    k_decay = jnp.exp((C - 1.0 - i)[None, :] * log_g[:, None])
    q_decay = jnp.exp((i + 1.0)[None, :] * log_g[:, None])
    kdec = jnp.broadcast_to(k_decay[:, :, None], (H, C, dk))
    qdec = jnp.broadcast_to(q_decay[:, :, None], (H, C, dk))
    log_g_np = np.log(1.0 - 2.0 ** (-5.0 - np.arange(H, dtype=np.float32))).astype(np.float32)
    chunk_decay = tuple(float(np.exp(np.float32(C) * lg)) for lg in log_g_np)
    nq = H * dk
    const3 = lambda i: (0, 0, 0)
    return pl.pallas_call(
        functools.partial(_retention_kernel, chunk_decay=chunk_decay),
        grid=(T // tb,),
        in_specs=[pl.BlockSpec((tb, nq), lambda i: (i, 0)),
                  pl.BlockSpec((tb, nq), lambda i: (i, 1)),
                  pl.BlockSpec((tb, H * dv), lambda i: (i, 1)),
                  pl.BlockSpec((tb, H * dv), lambda i: (i, 2)),
                  pl.BlockSpec((tb, dk), lambda i: (i, 0)),
                  pl.BlockSpec((tb, dk), lambda i: (i, 0)),
                  pl.BlockSpec((H, C, C), const3),
                  pl.BlockSpec((H, C, dk), const3),
                  pl.BlockSpec((H, C, dk), const3)],
        out_specs=pl.BlockSpec((tb, H * dv), lambda i: (i, 0)),
        out_shape=jax.ShapeDtypeStruct((T, H * dv), BF16),
        scratch_shapes=[pltpu.VMEM((H, dk, dv), F32)],
        compiler_params=_params("arbitrary"),
        name="retention",
    )(z, z, z, z, cos2, sin2, decay, kdec, qdec)


def _rglru_kernel(x_ref, y_ref, pos_ref, cw_ref, cb_ref, wa_ref, ba_ref, wx_ref, bx_ref, lam_ref,
                  o_ref, xpad_ref, a_ref, b_ref, hc_ref):
    tb = x_ref.shape[0]
    pad = SUBLANES
    halo = SUBLANES

    @pl.when(pl.program_id(0) == 0)
    def _():
        xpad_ref[0:halo, :] = jnp.zeros((halo, D_MODEL), F32)
        hc_ref[...] = jnp.zeros_like(hc_ref)
        a_ref[0:pad, :] = jnp.ones((pad, D_MODEL), F32)
        b_ref[0:pad, :] = jnp.zeros((pad, D_MODEL), F32)

    xpad_ref[halo:halo + tb, :] = x_ref[...].astype(F32)
    xc = cb_ref[...] + cw_ref[3:4, :] * xpad_ref[halo:halo + tb, :]
    for w in range(CONV_WIDTH - 1):
        off = halo - (CONV_WIDTH - 1) + w
        xc = xc + cw_ref[w:w + 1, :] * xpad_ref[off:off + tb, :]
    xpad_ref[0:halo, :] = xpad_ref[tb:tb + halo, :]

    reset = pos_ref[...] == 0
    lam = lam_ref[...]
    nl = -lam
    sp = jnp.maximum(nl, 0.0) + jnp.log(1.0 + jnp.exp(-jnp.abs(nl)))
    bd = LRU_BLOCK_DIM
    for n in range(LRU_BLOCKS):
        cols = slice(n * bd, (n + 1) * bd)
        xcn = xc[:, cols]
        xb = xcn.astype(BF16)
        gate_r = jax.nn.sigmoid(_dot(xb, wa_ref[n]) + ba_ref[:, cols])
        gate_i = jax.nn.sigmoid(_dot(xb, wx_ref[n]) + bx_ref[:, cols])
        log_a = (-LRU_C) * gate_r * sp[:, cols]
        a = jnp.exp(log_a)
        mult = jnp.sqrt(jnp.tanh(-log_a) * (a * a + 1.0))
        a = jnp.where(reset, 0.0, a)
        mult = jnp.where(reset, 1.0, mult)
        b = xcn * gate_i * mult
        s = 1
        while s < SUBLANES:
            a_ref[pad:pad + tb, cols] = a
            b_ref[pad:pad + tb, cols] = b
            a_sh = a_ref[pad - s:pad - s + tb, cols]
            b_sh = b_ref[pad - s:pad - s + tb, cols]
            b = a * b_sh + b
            a = a * a_sh
            s *= 2
        hg = hc_ref[:, cols]
        groups = []
        for g in range(tb // SUBLANES):
            rows = slice(g * SUBLANES, (g + 1) * SUBLANES)
            hg = a[rows, :] * hg + b[rows, :]
            groups.append(hg)
        h = jnp.concatenate(groups, axis=0)
        hc_ref[:, cols] = jnp.broadcast_to(hg[SUBLANES - 1:SUBLANES, :], (SUBLANES, bd))
        o_ref[:, cols] = (h * _gelu(y_ref[:, cols].astype(F32))).astype(o_ref.dtype)


def _rglru(z, pos_col, conv_w, conv_b, wa, ba, wx, bx, lam, tb):
    T = z.shape[0]
    W = D_MODEL
    row = lambda i: (0, 0)
    const3 = lambda i: (0, 0, 0)
    return pl.pallas_call(
        _rglru_kernel,
        grid=(T // tb,),
        in_specs=[pl.BlockSpec((tb, W), lambda i: (i, 3)),
                  pl.BlockSpec((tb, W), lambda i: (i, 4)),
                  pl.BlockSpec((tb, 1), lambda i: (i, 0)),
                  pl.BlockSpec((CONV_WIDTH, W), row),
                  pl.BlockSpec((1, W), row),
                  pl.BlockSpec((LRU_BLOCKS, LRU_BLOCK_DIM, LRU_BLOCK_DIM), const3),
                  pl.BlockSpec((1, W), row),
                  pl.BlockSpec((LRU_BLOCKS, LRU_BLOCK_DIM, LRU_BLOCK_DIM), const3),
                  pl.BlockSpec((1, W), row),
                  pl.BlockSpec((1, W), row)],
        out_specs=pl.BlockSpec((tb, W), lambda i: (i, 0)),
        out_shape=jax.ShapeDtypeStruct((T, W), BF16),
        scratch_shapes=[pltpu.VMEM((tb + 2 * SUBLANES, W), F32),
                        pltpu.VMEM((tb + SUBLANES, W), F32),
                        pltpu.VMEM((tb + SUBLANES, W), F32),
                        pltpu.VMEM((SUBLANES, W), F32)],
        compiler_params=_params("arbitrary"),
        name="rglru",
    )(z, z, pos_col, conv_w, conv_b, wa, ba, wx, bx, lam)


def _merge_kernel(x_ref, ret_ref, rnn_ref, ga_ref, gb_ref, wr_ref, wn_ref, wo_ref, o_ref, w_ref):
    @pl.when(pl.program_id(0) == 0)
    def _():
        for k, src in enumerate((wr_ref, wn_ref, wo_ref)):
            w_ref[k] = src[...].astype(BF16)

    p_ret = _dot(ret_ref[...], w_ref[0])
    p_rnn = _dot(rnn_ref[...], w_ref[1])
    merged = (jax.nn.sigmoid(ga_ref[...].astype(F32)) * p_ret
              + jax.nn.sigmoid(gb_ref[...].astype(F32)) * p_rnn)
    o_ref[...] = x_ref[...] + _dot(merged.astype(BF16), w_ref[2])


def _merge(x, ret, rnn, z, w_ret, w_rnn, w_out, layer, tm):
    T, D = x.shape
    tile = lambda i: (i, 0)
    wspec = pl.BlockSpec((None, D, D), lambda i: (layer, 0, 0))
    return pl.pallas_call(
        _merge_kernel,
        grid=(T // tm,),
        in_specs=[pl.BlockSpec((tm, D), tile), pl.BlockSpec((tm, D), tile), pl.BlockSpec((tm, D), tile),
                  pl.BlockSpec((tm, D), lambda i: (i, 5)),
                  pl.BlockSpec((tm, D), lambda i: (i, 6)),
                  wspec, wspec, wspec],
        out_specs=pl.BlockSpec((tm, D), tile),
        out_shape=jax.ShapeDtypeStruct((T, D), F32),
        scratch_shapes=[pltpu.VMEM((3, D, D), BF16)],
        compiler_params=_params("arbitrary"),
        name="merge",
    )(x, ret, rnn, z, z, w_ret, w_rnn, w_out)


def _xattn_kernel(x_ref, g_ref, mem_ref, gm_ref, wq_ref, wk_ref, wv_ref, wo_ref, o_ref,
                  att_ref, w_ref, kt_ref, v_ref):
    @pl.when(pl.program_id(0) == 0)
    def _():
        w_ref[0] = wq_ref[...].astype(BF16)
        w_ref[1] = wo_ref[...].astype(BF16)
        m = _rms(mem_ref[...], gm_ref[...]).astype(BF16)
        k = _dot(m, wk_ref[...].astype(BF16)).astype(BF16)
        kt_ref[...] = jnp.transpose(k.astype(F32)).astype(BF16)
        v_ref[...] = _dot(m, wv_ref[...].astype(BF16)).astype(BF16)

    x = x_ref[...]
    q = _dot(_rms(x, g_ref[...]).astype(BF16), w_ref[0])
    dh = XATTN_HEAD_DIM
    for h in range(XATTN_HEADS):
        cols = slice(h * dh, (h + 1) * dh)
        s = _dot(q[:, cols].astype(BF16), kt_ref[cols, :]) * (dh ** -0.5)
        s = s - jnp.max(s, axis=-1, keepdims=True)
        e = jnp.exp(s)
        p = e / jnp.sum(e, axis=-1, keepdims=True)
        att_ref[:, cols] = _dot(p.astype(BF16), v_ref[:, cols]).astype(BF16)
    o_ref[...] = x + _dot(att_ref[...], w_ref[1])


def _xattn(x, g, mem, g_mem, wq, wk, wv, wo, layer, tm):
    T, D = x.shape
    M = mem.shape[0]
    tile = lambda i: (i, 0)
    fixed = lambda i: (0, 0)
    wspec = pl.BlockSpec((None, D, D), lambda i: (layer, 0, 0))
    return pl.pallas_call(
        _xattn_kernel,
        grid=(T // tm,),
        in_specs=[pl.BlockSpec((tm, D), tile), pl.BlockSpec((1, D), fixed),
                  pl.BlockSpec((M, D), fixed), pl.BlockSpec((1, D), fixed), wspec, wspec, wspec, wspec],
        out_specs=pl.BlockSpec((tm, D), tile),
        out_shape=jax.ShapeDtypeStruct((T, D), F32),
        scratch_shapes=[pltpu.VMEM((tm, D), BF16), pltpu.VMEM((2, D, D), BF16),
                        pltpu.VMEM((D, M), BF16), pltpu.VMEM((M, D), BF16)],
        compiler_params=_params("arbitrary"),
        name="xattn",
    )(x, g, mem, g_mem, wq, wk, wv, wo)


def _peer_scores_kernel(x_ref, g_ref, wq_ref, k1_ref, k2_ref, hft_ref, s1_ref, s2_ref, wqt_ref):
    @pl.when(pl.program_id(0) == 0)
    def _():
        cw = wqt_ref.shape[1] // 4
        for c in range(wqt_ref.shape[0] // cw):
            wqt_ref[c * cw:(c + 1) * cw, :] = jnp.transpose(wq_ref[:, c * cw:(c + 1) * cw]).astype(BF16)

    hf = _rms(x_ref[...], g_ref[...])
    hft = jnp.transpose(hf).astype(BF16)
    hft_ref[...] = hft
    qt = _dot(wqt_ref[...], hft)
    kd = 2 * PEER_HALF
    for h in range(PEER_HEADS):
        q1 = qt[h * kd:h * kd + PEER_HALF, :].astype(BF16)
        q2 = qt[h * kd + PEER_HALF:(h + 1) * kd, :].astype(BF16)
        s1_ref[h] = _dot(k1_ref[h], q1)
        s2_ref[h] = _dot(k2_ref[h], q2)


def _peer_scores(x, g, wq, layer, k1, k2, tm):
    T, D = x.shape
    NQ = wq.shape[2]
    H, NK = PEER_HEADS, PEER_KEYS
    fixed = lambda i: (0, 0)
    const3 = lambda i: (0, 0, 0)
    return pl.pallas_call(
        _peer_scores_kernel,
        grid=(T // tm,),
        in_specs=[pl.BlockSpec((tm, D), lambda i: (i, 0)), pl.BlockSpec((1, D), fixed),
                  pl.BlockSpec((None, D, NQ), lambda i: (layer, 0, 0)),
                  pl.BlockSpec((H, NK, PEER_HALF), const3), pl.BlockSpec((H, NK, PEER_HALF), const3)],
        out_specs=[pl.BlockSpec((D, tm), lambda i: (0, i)),
                   pl.BlockSpec((H, NK, tm), lambda i: (0, 0, i)),
                   pl.BlockSpec((H, NK, tm), lambda i: (0, 0, i))],
        out_shape=[jax.ShapeDtypeStruct((D, T), BF16),
                   jax.ShapeDtypeStruct((H, NK, T), F32),
                   jax.ShapeDtypeStruct((H, NK, T), F32)],
        scratch_shapes=[pltpu.VMEM((NQ, D), BF16)],
        compiler_params=_params("arbitrary"),
        name="peer_scores",
    )(x, g, wq, k1, k2)


def _sort_pairs(n):
    pairs = []
    p = 1
    while p < n:
        k = p
        while k >= 1:
            for j in range(k % p, n - k, 2 * k):
                for i in range(min(k, n - j - k)):
                    if (i + j) // (2 * p) == (i + j + k) // (2 * p):
                        pairs.append((i + j, i + j + k))
            k //= 2
        p *= 2
    return pairs


def _sort_desc(vs):
    vs = list(vs)
    for lo, hi in _sort_pairs(len(vs)):
        a, b = vs[lo], vs[hi]
        vs[lo], vs[hi] = jnp.maximum(a, b), jnp.minimum(a, b)
    return vs


def _merge_top(a, b):
    n = len(a)
    c = [jnp.maximum(a[i], b[n - 1 - i]) for i in range(n)]
    k = n // 2
    while k >= 1:
        for i in range(n):
            if i & k == 0:
                x, y = c[i], c[i + k]
                c[i], c[i + k] = jnp.maximum(x, y), jnp.minimum(x, y)
        k //= 2
    return c


def _top_of_keys(groups, roll):
    vs = _sort_desc(groups)
    shift = SUBLANES // 2
    while shift >= 1:
        vs = _merge_top(vs, [roll(v, shift, 0) for v in vs])
        shift //= 2
    return vs


def _peer_select_math(s1, s2, roll):
    K = PEER_TOPK
    shape = s1[0][0].shape
    sub = lax.broadcasted_iota(jnp.int32, shape, 0)
    zero = jnp.zeros(shape, F32)
    tops2 = []
    v1 = [zero] * K
    v2 = [zero] * K
    for h in range(PEER_HEADS):
        t1 = _top_of_keys(s1[h], roll)
        t2 = _top_of_keys(s2[h], roll)
        tops2.append(t2)
        v1 = [jnp.where(sub == h, t, v) for t, v in zip(t1, v1)]
        v2 = [jnp.where(sub == h, t, v) for t, v in zip(t2, v2)]
    cands = [v1[i] + v2[j] for i in range(K) for j in range(K // (i + 1))]
    n = 1
    while n < len(cands):
        n *= 2
    cands = cands + [jnp.full(shape, -jnp.inf, F32)] * (n - len(cands))
    top = _sort_desc(cands)[:K]
    z = zero
    for t in top:
        z = z + jnp.exp(t - top[0])
    tau_all, zinv_all = top[K - 1], 1.0 / z
    out = []
    for h in range(PEER_HEADS):
        bc = lambda v: jnp.broadcast_to(v[h:h + 1, :], shape)
        tau, zinv, m1, m2 = bc(tau_all), bc(zinv_all), bc(v1[0]), bc(v2[0])
        t2 = tops2[h]
        cnt, p1, rk, e2 = [], [], [], []
        c_top = zero
        for j in range(K // 2, K):
            c_top = jnp.where(m1 + t2[j] >= tau, float(j + 1), c_top)
        for s1g, s2g in zip(s1[h], s2[h]):
            c = zero
            r = zero
            for j in range(K // 2):
                c = jnp.where(s1g + t2[j] >= tau, float(j + 1), c)
            c = jnp.where(s1g == m1, jnp.maximum(c, c_top), c)
            for j in range(K):
                r = jnp.where(t2[j] > s2g, float(j + 1), r)
            cnt.append(c)
            rk.append(r)
            p1.append(jnp.exp(s1g - m1) * zinv)
            e2.append(jnp.exp(s2g - m2))
        out.append((cnt, p1, rk, e2))
    return out


def _bf16_pair(x):
    u = lax.bitcast_convert_type(x.astype(BF16).astype(F32), jnp.uint32)
    return u | (u >> 16)


def _peer_select_kernel(s1_ref, s2_ref, n_ref, p1_ref, r2_ref, e2_ref):
    ng = PEER_KEYS // SUBLANES
    grp = lambda ref, h: [ref[h, i * SUBLANES:(i + 1) * SUBLANES, :] for i in range(ng)]
    s1 = [grp(s1_ref, h) for h in range(PEER_HEADS)]
    s2 = [grp(s2_ref, h) for h in range(PEER_HEADS)]
    res = _peer_select_math(s1, s2, pltpu.roll)
    pk = 2 * SUBLANES
    for h, (cnt, p1, rk, e2) in enumerate(res):
        for i in range(ng):
            rows = slice(i * SUBLANES, (i + 1) * SUBLANES)
            n_ref[h, rows, :] = _bf16_pair(cnt[i])
            p1_ref[h, rows, :] = _bf16_pair(p1[i])
        for i in range(ng // 2):
            rows = slice(i * pk, (i + 1) * pk)
            r2_ref[h, rows, :] = jnp.concatenate([rk[2 * i], rk[2 * i + 1]], axis=0).astype(BF16)
            e2_ref[h, rows, :] = jnp.concatenate([e2[2 * i], e2[2 * i + 1]], axis=0).astype(BF16)


def _peer_select(s1, s2, tl):
    H, NK, T = s1.shape
    spec = pl.BlockSpec((H, NK, tl), lambda i: (0, 0, i))
    return pl.pallas_call(
        _peer_select_kernel,
        grid=(T // tl,),
        in_specs=[spec] * 2,
        out_specs=[spec] * 4,
        out_shape=[jax.ShapeDtypeStruct((H, NK, T), jnp.uint32), jax.ShapeDtypeStruct((H, NK, T), jnp.uint32),
                   jax.ShapeDtypeStruct((H, NK, T), BF16), jax.ShapeDtypeStruct((H, NK, T), BF16)],
        compiler_params=_params("parallel"),
        name="peer_select",
    )(s1, s2)


def _peer_gate_rows(act, a, n_ref, p1_ref, r2_ref, e2_ref, w_ref, row0):
    NK = PEER_KEYS
    tm = act.shape[-1]
    pk = 2 * SUBLANES
    zero = jnp.zeros((pk, tm), BF16)
    bc = lambda ref, h: pltpu.bitcast(jnp.broadcast_to(ref[h, pl.ds(a, 1), :], (SUBLANES, tm)), BF16)
    nrow = [bc(n_ref, h) for h in range(PEER_HEADS)]
    prow = [bc(p1_ref, h) for h in range(PEER_HEADS)]
    for g in range(NK // pk):
        b_rows = slice(g * pk, (g + 1) * pk)
        gate = None
        for h in range(PEER_HEADS):
            sel = r2_ref[h, b_rows, :] < nrow[h]
            term = jnp.where(sel, prow[h] * e2_ref[h, b_rows, :], zero)
            gate = term if gate is None else gate + term
        w_ref[row0 + g * pk:row0 + (g + 1) * pk, :] = gate * act[b_rows, :]


def _peer_half(a0, hft, keys, u_ref, w_ref, vt_prev_ref, w_prev_ref, acc_ref, na):
    NK = PEER_KEYS
    gl = _gelu(_dot(u_ref[...], hft).astype(BF16))
    acc_ref[...] += _dot(vt_prev_ref[...], w_prev_ref[...])
    for ai in range(na):
        _peer_gate_rows(gl[ai * NK:(ai + 1) * NK], a0 + ai, *keys, w_ref, ai * NK)


def _peer_dense_kernel(x_ref, hft_ref, n_ref, p1_ref, r2_ref, e2_ref, ua_ref, ub_ref, vta_ref, vtb_ref,
                       vtl_ref, *rest, na, norm_out):
    g_ref = rest[0] if norm_out else None
    o_ref, acc_ref, wa_ref, wb_ref = rest[-4:]
    j = pl.program_id(1)
    last = pl.num_programs(1) - 1
    keys = (n_ref, p1_ref, r2_ref, e2_ref)

    @pl.when(j == 0)
    def _():
        acc_ref[...] = jnp.zeros_like(acc_ref)
        wb_ref[...] = jnp.zeros_like(wb_ref)

    hft = hft_ref[...]
    _peer_half(2 * j * na, hft, keys, ua_ref, wa_ref, vtb_ref, wb_ref, acc_ref, na)
    _peer_half((2 * j + 1) * na, hft, keys, ub_ref, wb_ref, vta_ref, wa_ref, acc_ref, na)

    @pl.when(j == last)
    def _():
        acc = acc_ref[...] + _dot(vtl_ref[...], wb_ref[...])
        xo = x_ref[...] + jnp.transpose(acc)
        o_ref[...] = _rms(xo, g_ref[...]) if norm_out else xo


def _peer_dense(x, hft, n1, p1, r2, e2, u, vt, layer, tm, na, g_out=None):
    T, D = x.shape
    H, NK = PEER_HEADS, PEER_KEYS
    th = na * NK
    steps = NK // (2 * na)
    kspec = pl.BlockSpec((H, NK, tm), lambda i, j: (0, 0, i))
    return pl.pallas_call(
        functools.partial(_peer_dense_kernel, na=na, norm_out=g_out is not None),
        grid=(T // tm, steps),
        in_specs=[pl.BlockSpec((tm, D), lambda i, j: (i, 0)),
                  pl.BlockSpec((D, tm), lambda i, j: (0, i)),
                  kspec, kspec, kspec, kspec,
                  pl.BlockSpec((None, th, D), lambda i, j: (layer, 2 * j, 0)),
                  pl.BlockSpec((None, th, D), lambda i, j: (layer, 2 * j + 1, 0)),
                  pl.BlockSpec((None, None, D, th), lambda i, j: (layer, 2 * j, 0, 0)),
                  pl.BlockSpec((None, None, D, th), lambda i, j: (layer, jnp.maximum(2 * j - 1, 1), 0, 0)),
                  pl.BlockSpec((None, None, D, th), lambda i, j: (layer, 2 * steps - 1, 0, 0))]
                 + ([pl.BlockSpec((1, D), lambda i, j: (0, 0))] if g_out is not None else []),
        out_specs=pl.BlockSpec((tm, D), lambda i, j: (i, 0)),
        out_shape=jax.ShapeDtypeStruct((T, D), F32),
        scratch_shapes=[pltpu.VMEM((D, tm), F32), pltpu.VMEM((th, tm), BF16), pltpu.VMEM((th, tm), BF16)],
        compiler_params=_params("parallel", "arbitrary"),
        name="peer_dense",
    )(x, hft, n1, p1, r2, e2, u, u, vt, vt, vt, *(() if g_out is None else (g_out,)))


def kernel(x, mem, positions, g_mix, w_in, w_ret_br, w_rnn_br, w_mix_out, conv_w, conv_b, lru_wa, lru_ba, lru_wx, lru_bx, lru_lam, g_x, g_mem, w_xq, w_xk, w_xv, w_xo, g_ffn, w_pq, sub_k1, sub_k2, peer_u, peer_v, g_final):
    B, T, D = x.shape
    assert B == 1 and D == D_MODEL
    depth = g_mix.shape[0]
    tm = min(TM_PROJ, T)
    xs = x.reshape(T, D)
    mems = mem.reshape(mem.shape[1], D)
    pos_col = positions.reshape(T, 1)
    row = lambda a: a.reshape(1, -1)

    cos2, sin2 = _rope_tables(pos_col, tm)
    u_all = peer_u.astype(BF16)
    th = NA_PEER * PEER_KEYS
    vt_all = jnp.swapaxes(peer_v.reshape(depth, -1, th, D), 2, 3).astype(BF16)
    for l in range(depth):
        z = _rms_matmul(xs, row(g_mix[l]), w_in, l, tm=min(TM_IN, T), tn=TN_PROJ, out_dtype=BF16)
        ret = _retention(z, cos2, sin2, min(TB_RET, T))
        rnn = _rglru(z, pos_col, conv_w[l], row(conv_b[l]), lru_wa[l].astype(BF16), row(lru_ba[l]),
                     lru_wx[l].astype(BF16), row(lru_bx[l]), row(lru_lam[l]), min(TB_LRU, T))
        xs = _merge(xs, ret, rnn, z, w_ret_br, w_rnn_br, w_mix_out, l, tm)
        xs = _xattn(xs, row(g_x[l]), mems, row(g_mem[l]), w_xq, w_xk, w_xv, w_xo, l, tm)
        tp = min(TM_PEER, T)
        hft, s1, s2 = _peer_scores(xs, row(g_ffn[l]), w_pq, l, sub_k1[l].astype(BF16),
                                   sub_k2[l].astype(BF16), tp)
        n1, p1, r2, e2 = _peer_select(s1, s2, TL_SORT)
        xs = _peer_dense(xs, hft, n1, p1, r2, e2, u_all, vt_all, l, tp, NA_PEER,
                         g_out=row(g_final) if l == depth - 1 else None)
    return xs.reshape(B, T, D)
```
